```python
import jax
import jax.numpy as jnp
from jax import lax
import numpy as np

D_MODEL = 1024
BATCH = 8
SEQ = 2048
DEPTH = 4

GRID_W = 64
CTX_LEN = 256
D_MIX = D_MODEL
N_GROUPS = 4
GROUP_W = D_MIX // N_GROUPS
HEAD_DIM = 64
NORM_EPS = 1e-6
ROPE_BASE = 10000.0
NEG_INF = -1e30

CONV_W = 4
LRU_BLOCKS = GROUP_W // HEAD_DIM
LRU_BLOCK = GROUP_W // LRU_BLOCKS
LRU_C = 8.0
GDN_HEADS = GROUP_W // HEAD_DIM
GDN_CHUNK = 64
GDN_INV_STEPS = GDN_CHUNK.bit_length() - 2
RET_HEADS = GROUP_W // HEAD_DIM
RET_CHUNK = 64
SWA_HEADS = GROUP_W // HEAD_DIM
SWA_KV_HEADS = 2
SWA_GROUP = SWA_HEADS // SWA_KV_HEADS
SWA_KV_W = SWA_KV_HEADS * HEAD_DIM
WINDOW = 128
SWA_BLOCK = WINDOW

IN_SPLITS = (
    ('lru_x', GROUP_W), ('lru_z', GROUP_W),
    ('gdn_qkv', 3 * GROUP_W), ('gdn_z', GROUP_W), ('gdn_alpha', 2 * GDN_HEADS), ('gdn_beta', 2 * GDN_HEADS),
    ('ret_q', GROUP_W), ('ret_k', GROUP_W), ('ret_v', GROUP_W), ('ret_z', GROUP_W),
    ('swa_q', GROUP_W), ('swa_k', SWA_KV_W), ('swa_v', SWA_KV_W), ('swa_z', GROUP_W),
)
IN_W = sum(w for _, w in IN_SPLITS)

kernel_name = 'hybrid_parallel_group_diffusion_block'


def rmsnorm(x, g):
    xf = x.astype(jnp.float32)
    y = xf * lax.rsqrt(jnp.mean(xf * xf, axis=-1, keepdims=True) + NORM_EPS)
    return (y * g.astype(jnp.float32)).astype(x.dtype)


def l2norm(x):
    return x * lax.rsqrt(jnp.sum(x * x, axis=-1, keepdims=True) + NORM_EPS)


def head_groupnorm(o):
    mu = jnp.mean(o, axis=-1, keepdims=True)
    d = o - mu
    return d * lax.rsqrt(jnp.mean(d * d, axis=-1, keepdims=True) + NORM_EPS)


def split_in(u):
    parts, off = {}, 0
    for name, w in IN_SPLITS:
        parts[name] = u[..., off:off + w]
        off += w
    return parts


def dwconv(x, w):
    k = w.shape[0]
    return lax.conv_general_dilated(
        x, w[:, None, :].astype(x.dtype), window_strides=(1,),
        padding=[(k // 2, k - 1 - k // 2)],
        dimension_numbers=('NWC', 'WIO', 'NWC'), feature_group_count=x.shape[-1])


def rope_freqs(pos, n):
    inv = ROPE_BASE ** (-jnp.arange(0, n, 2, dtype=jnp.float32) / n)
    return pos[:, None] * inv[None, :]


def apply_rope(x, ang):
    half = x.shape[-1] // 2
    cos = jnp.cos(ang)[None, :, None, :]
    sin = jnp.sin(ang)[None, :, None, :]
    x1, x2 = x[..., :half], x[..., half:]
    return jnp.concatenate([x1 * cos - x2 * sin, x1 * sin + x2 * cos], axis=-1).astype(x.dtype)


def _chunks4(t, c):
    bsz, tl, h, d = t.shape
    return t.reshape(bsz, tl // c, c, h, d).transpose(1, 0, 3, 2, 4)


def _chunks3(t, c):
    bsz, tl, h = t.shape
    return t.reshape(bsz, tl // c, c, h).transpose(1, 0, 3, 2)


def _unchunk(o):
    n, bsz, h, c, d = o.shape
    return o.transpose(1, 0, 3, 2, 4).reshape(bsz, n * c, h, d)


def linear_scan(a, b, h0):
    b = b.at[:, 0].add(a[:, 0] * h0)

    def combine(left, right):
        a_l, b_l = left
        a_r, b_r = right
        return a_l * a_r, a_r * b_l + b_r

    _, h = lax.associative_scan(combine, (a, b), axis=1)
    return h


def rglru_dir(u, w_r, b_r, w_i, b_i, lam, h0):
    bsz, t, w = u.shape
    ub = u.reshape(bsz, t, LRU_BLOCKS, LRU_BLOCK)
    r = jax.nn.sigmoid(jnp.einsum('btnc,ncd->btnd', ub, w_r.astype(jnp.float32)).reshape(bsz, t, w) + b_r)
    i = jax.nn.sigmoid(jnp.einsum('btnc,ncd->btnd', ub, w_i.astype(jnp.float32)).reshape(bsz, t, w) + b_i)
    log_a = -LRU_C * r * jax.nn.softplus(-lam.astype(jnp.float32))
    a = jnp.exp(log_a)
    b = jnp.sqrt(-jnp.expm1(2.0 * log_a)) * (i * u)
    return linear_scan(a, b, h0)


def mixer_rglru(u, uc, conv_w, conv_b, w_r, b_r, w_i, b_i, lam, with_ctx):
    f32 = jnp.float32
    x_l, x_c = u['lru_x'], uc['lru_x']
    u_l = (dwconv(x_l, conv_w) + conv_b).astype(f32)
    u_c = (dwconv(x_c, conv_w) + conv_b).astype(f32)
    h0 = jnp.zeros((x_l.shape[0], GROUP_W), f32)
    hc_f = rglru_dir(u_c, w_r[0], b_r[0], w_i[0], b_i[0], lam[0], h0)
    hl_f = rglru_dir(u_l, w_r[0], b_r[0], w_i[0], b_i[0], lam[0], hc_f[:, -1])
    hc_b = rglru_dir(u_c[:, ::-1], w_r[1], b_r[1], w_i[1], b_i[1], lam[1], h0)
    hl_b = rglru_dir(u_l[:, ::-1], w_r[1], b_r[1], w_i[1], b_i[1], lam[1], hc_b[:, -1])[:, ::-1]
    y_l = ((hl_f + hl_b) * jax.nn.silu(u['lru_z'].astype(f32))).astype(x_l.dtype)
    y_c = None
    if with_ctx:
        y_c = ((hc_f + hc_b[:, ::-1]) * jax.nn.silu(uc['lru_z'].astype(f32))).astype(x_c.dtype)
    return y_l, y_c


def unit_lower_inverse(l_mat):
    eye = jnp.eye(l_mat.shape[-1], dtype=l_mat.dtype)
    n = -l_mat
    p = eye + n
    for _ in range(GDN_INV_STEPS):
        n = n @ n
        p = p @ (eye + n)
    return p


def gdn_chunked(q, k, v, g, beta, s0, with_out):
    c = GDN_CHUNK
    ks, vs = _chunks4(k, c), _chunks4(v, c)
    gs, bs = _chunks3(g, c), _chunks3(beta, c)
    gcum = jnp.cumsum(gs, axis=-1)
    idx = jnp.arange(c)
    lower = idx[:, None] >= idx[None, :]
    strict = idx[:, None] > idx[None, :]
    diff = gcum[..., :, None] - gcum[..., None, :]
    decay = jnp.where(lower, jnp.exp(jnp.where(lower, diff, 0.0)), 0.0)
    kb = ks * bs[..., None]
    l_mat = jnp.where(strict, jnp.einsum('nbhid,nbhjd->nbhij', kb, ks) * decay, 0.0)
    t_mat = unit_lower_inverse(l_mat)
    u_val = t_mat @ (vs * bs[..., None])
    w_dec = t_mat @ (kb * jnp.exp(gcum)[..., None])
    k_tail = ks * jnp.exp(gcum[..., -1:] - gcum)[..., None]
    g_tail = jnp.exp(gcum[..., -1])
    if with_out:
        qs = _chunks4(q, c)
        att = jnp.where(lower, jnp.einsum('nbhid,nbhjd->nbhij', qs, ks) * decay, 0.0)
        q_dec = qs * jnp.exp(gcum)[..., None]
        xs = (u_val, w_dec, k_tail, g_tail, att, q_dec)
    else:
        xs = (u_val, w_dec, k_tail, g_tail)

    def step(s, inp):
        u_i, w_i, kt_i, gt_i = inp[0], inp[1], inp[2], inp[3]
        v_new = u_i - jnp.einsum('bhcd,bhde->bhce', w_i, s)
        s_new = s * gt_i[..., None, None] + jnp.einsum('bhcd,bhce->bhde', kt_i, v_new)
        if not with_out:
            return s_new, None
        att_i, qd_i = inp[4], inp[5]
        o = jnp.einsum('bhcd,bhde->bhce', qd_i, s) + jnp.einsum('bhij,bhje->bhie', att_i, v_new)
        return s_new, o

    s_fin, o = lax.scan(step, s0, xs)
    return (_unchunk(o) if with_out else None), s_fin


def gdn_prep(qkv, alpha, beta, conv_w, a_log, dt_bias):
    f32 = jnp.float32
    bsz, t, _ = qkv.shape
    h = jax.nn.silu(dwconv(qkv, conv_w).astype(f32)).reshape(bsz, t, 3, GDN_HEADS, HEAD_DIM)
    q = l2norm(h[:, :, 0]) * HEAD_DIM ** -0.5
    k = l2norm(h[:, :, 1])
    v = h[:, :, 2]
    g = -jnp.exp(a_log.astype(f32)) * jax.nn.softplus(
        alpha.astype(f32).reshape(bsz, t, 2, GDN_HEADS) + dt_bias.astype(f32))
    b = jax.nn.sigmoid(beta.astype(f32).reshape(bsz, t, 2, GDN_HEADS))
    return q, k, v, g, b


def mixer_gdn(u, uc, conv_w, a_log, dt_bias, norm_g, with_ctx):
    ql, kl, vl, gl, bl = gdn_prep(u['gdn_qkv'], u['gdn_alpha'], u['gdn_beta'], conv_w, a_log, dt_bias)
    qc, kc, vc, gc, bc = gdn_prep(uc['gdn_qkv'], uc['gdn_alpha'], uc['gdn_beta'], conv_w, a_log, dt_bias)
    s0 = jnp.zeros((ql.shape[0], GDN_HEADS, HEAD_DIM, HEAD_DIM), jnp.float32)
    rev = lambda t: t[:, ::-1]
    oc_f, sc_f = gdn_chunked(qc, kc, vc, gc[:, :, 0], bc[:, :, 0], s0, with_ctx)
    ol_f, _ = gdn_chunked(ql, kl, vl, gl[:, :, 0], bl[:, :, 0], sc_f, True)
    oc_b, sc_b = gdn_chunked(rev(qc), rev(kc), rev(vc), rev(gc[:, :, 1]), rev(bc[:, :, 1]), s0, with_ctx)
    ol_b, _ = gdn_chunked(rev(ql), rev(kl), rev(vl), rev(gl[:, :, 1]), rev(bl[:, :, 1]), sc_b, True)

    def finish(o, z):
        bsz, t = z.shape[0], z.shape[1]
        return (rmsnorm(o, norm_g).reshape(bsz, t, GROUP_W) * jax.nn.silu(z.astype(jnp.float32))).astype(z.dtype)

    y_l = finish(ol_f + rev(ol_b), u['gdn_z'])
    y_c = finish(oc_f + rev(oc_b), uc['gdn_z']) if with_ctx else None
    return y_l, y_c


def retention_chunked(q, k, v, log_gamma, s0, with_out):
    c = RET_CHUNK
    idx = jnp.arange(c, dtype=jnp.float32)
    diff = idx[:, None] - idx[None, :]
    lower = diff >= 0
    dmat = jnp.where(lower, jnp.exp(jnp.where(lower, diff, 0.0)[None] * log_gamma[:, None, None]), 0.0)
    q_dec = jnp.exp((idx + 1.0)[None, :] * log_gamma[:, None])
    k_dec = jnp.exp((c - 1.0 - idx)[None, :] * log_gamma[:, None])
    c_dec = jnp.exp(c * log_gamma)
    ks, vs = _chunks4(k, c), _chunks4(v, c)
    xs = (ks, vs, _chunks4(q, c)) if with_out else (ks, vs)

    def step(s, inp):
        k_i, v_i = inp[0], inp[1]
        s_new = s * c_dec[None, :, None, None] + jnp.einsum('bhcd,bhce->bhde', k_i * k_dec[None, :, :, None], v_i)
        if not with_out:
            return s_new, None
        q_i = inp[2]
        att = jnp.einsum('bhid,bhjd->bhij', q_i, k_i) * dmat[None]
        o = jnp.einsum('bhcd,bhde->bhce', q_i * q_dec[None, :, :, None], s) + jnp.einsum('bhij,bhje->bhie', att, v_i)
        return s_new, o

    s_fin, o = lax.scan(step, s0, xs)
    return (_unchunk(o) if with_out else None), s_fin


def mixer_retention(u, uc, decay_logit, ang1d, with_ctx):
    f32 = jnp.float32
    heads = lambda t: t.astype(f32).reshape(t.shape[0], t.shape[1], RET_HEADS, HEAD_DIM)
    kscale = HEAD_DIM ** -0.5
    ql = apply_rope(heads(u['ret_q']), ang1d)
    kl = apply_rope(heads(u['ret_k']), ang1d) * kscale
    vl = heads(u['ret_v'])
    qc = heads(uc['ret_q']) if with_ctx else None
    kc = heads(uc['ret_k']) * kscale
    vc = heads(uc['ret_v'])
    lg = jax.nn.log_sigmoid(decay_logit.astype(f32))
    s0 = jnp.zeros((ql.shape[0], RET_HEADS, HEAD_DIM, HEAD_DIM), f32)
    rev = lambda t: None if t is None else t[:, ::-1]
    oc_f, s_f = retention_chunked(qc, kc, vc, lg[0], s0, with_ctx)
    ol_f, _ = retention_chunked(ql, kl, vl, lg[0], s_f, True)
    oc_b, s_b = retention_chunked(rev(qc), rev(kc), rev(vc), lg[1], s0, with_ctx)
    ol_b, _ = retention_chunked(rev(ql), rev(kl), rev(vl), lg[1], s_b, True)

    def finish(o, z):
        bsz, t = z.shape[0], z.shape[1]
        return (head_groupnorm(o).reshape(bsz, t, GROUP_W) * jax.nn.silu(z.astype(f32))).astype(z.dtype)

    y_l = finish(ol_f + ol_b[:, ::-1], u['ret_z'])
    y_c = finish(oc_f + oc_b[:, ::-1], uc['ret_z']) if with_ctx else None
    return y_l, y_c


def mixer_swa(u, uc, sink, ang2d, with_ctx):
    f32 = jnp.float32
    bsz, t, _ = u['swa_q'].shape
    lc = uc['swa_k'].shape[1]
    blk = SWA_BLOCK
    nb = t // blk
    scale = HEAD_DIM ** -0.5
    q = apply_rope(u['swa_q'].reshape(bsz, t, SWA_HEADS, HEAD_DIM), ang2d) * scale
    k = apply_rope(u['swa_k'].reshape(bsz, t, SWA_KV_HEADS, HEAD_DIM), ang2d)
    v = u['swa_v'].reshape(bsz, t, SWA_KV_HEADS, HEAD_DIM)
    kc = uc['swa_k'].reshape(bsz, lc, SWA_KV_HEADS, HEAD_DIM)
    vc = uc['swa_v'].reshape(bsz, lc, SWA_KV_HEADS, HEAD_DIM)
    sink_g = sink.astype(f32).reshape(SWA_KV_HEADS, SWA_GROUP)

    qb = q.reshape(bsz, nb, blk, SWA_KV_HEADS, SWA_GROUP, HEAD_DIM)
    pad = ((0, 0), (blk, blk), (0, 0), (0, 0))
    kp = jnp.pad(k, pad).reshape(bsz, nb + 2, blk, SWA_KV_HEADS, HEAD_DIM)
    vp = jnp.pad(v, pad).reshape(bsz, nb + 2, blk, SWA_KV_HEADS, HEAD_DIM)
    kw = jnp.concatenate([kp[:, :-2], kp[:, 1:-1], kp[:, 2:]], axis=2)
    vw = jnp.concatenate([vp[:, :-2], vp[:, 1:-1], vp[:, 2:]], axis=2)
    qi = jnp.arange(blk)[:, None]
    kj = jnp.arange(3 * blk)[None, :]
    in_win = jnp.abs(kj - blk - qi) <= WINDOW
    kpos = jnp.arange(nb)[:, None] * blk - blk + jnp.arange(3 * blk)[None, :]
    valid = (kpos >= 0) & (kpos < t)
    mask = in_win[None, :, :] & valid[:, None, :]
    s_loc = jnp.einsum('bnqhgd,bnkhd->bnhgqk', qb, kw).astype(f32)
    s_loc = jnp.where(mask[None, :, None, None], s_loc, NEG_INF)
    s_ctx = jnp.einsum('bnqhgd,bchd->bnhgqc', qb, kc).astype(f32)
    s_snk = jnp.broadcast_to(sink_g[None, None, :, :, None, None], s_loc.shape[:-1] + (1,))
    p = jax.nn.softmax(jnp.concatenate([s_loc, s_ctx, s_snk], axis=-1), axis=-1).astype(v.dtype)
    o = (jnp.einsum('bnhgqk,bnkhd->bnqhgd', p[..., :3 * blk], vw)
         + jnp.einsum('bnhgqc,bchd->bnqhgd', p[..., 3 * blk:3 * blk + lc], vc))
    z_l = u['swa_z']
    y_l = (o.reshape(bsz, t, GROUP_W).astype(f32) * jax.nn.silu(z_l.astype(f32))).astype(z_l.dtype)
    y_c = None
    if with_ctx:
        qcx = uc['swa_q'].reshape(bsz, lc, SWA_KV_HEADS, SWA_GROUP, HEAD_DIM) * scale
        s_c = jnp.einsum('bqhgd,bkhd->bhgqk', qcx, kc).astype(f32)
        s_cs = jnp.broadcast_to(sink_g[None, :, :, None, None], s_c.shape[:-1] + (1,))
        p_c = jax.nn.softmax(jnp.concatenate([s_c, s_cs], axis=-1), axis=-1).astype(vc.dtype)
        o_c = jnp.einsum('bhgqk,bkhd->bqhgd', p_c[..., :lc], vc)
        z_c = uc['swa_z']
        y_c = (o_c.reshape(bsz, lc, GROUP_W).astype(f32) * jax.nn.silu(z_c.astype(f32))).astype(z_c.dtype)
    return y_l, y_c


def setup_inputs(seed: int = 0) -> dict:
    key = jax.random.key(seed)
    ks = jax.random.split(key, 24)
    f32 = jnp.float32
    nrm = lambda k, shape, s: jax.random.normal(k, shape, f32) * s
    x = nrm(ks[0], (BATCH, SEQ, D_MODEL), 1.0)
    c = nrm(ks[1], (BATCH, D_MODEL), 1.0)
    ctx = nrm(ks[2], (BATCH, CTX_LEN, D_MODEL), 1.0)
    c_ctx = nrm(ks[3], (D_MODEL,), 1.0)
    w_mod = nrm(ks[4], (DEPTH, D_MODEL, 3 * D_MODEL), 0.5 * D_MODEL ** -0.5)
    b_mod = nrm(ks[5], (DEPTH, 3 * D_MODEL), 0.02)
    pre_norm_g = 1.0 + nrm(ks[6], (DEPTH, D_MODEL), 0.02)
    post_norm_g = 1.0 + nrm(ks[7], (DEPTH, D_MODEL), 0.02)
    w_in = nrm(ks[8], (DEPTH, D_MODEL, IN_W), D_MODEL ** -0.5)
    w_out = nrm(ks[9], (DEPTH, D_MIX, D_MODEL), D_MIX ** -0.5)
    lru_conv_w = nrm(ks[10], (DEPTH, CONV_W, GROUP_W), CONV_W ** -0.5)
    lru_conv_b = nrm(ks[11], (DEPTH, GROUP_W), 0.02)
    lru_w_r = nrm(ks[12], (DEPTH, 2, LRU_BLOCKS, LRU_BLOCK, LRU_BLOCK), LRU_BLOCK ** -0.5)
    lru_b_r = nrm(ks[13], (DEPTH, 2, GROUP_W), 0.02)
    lru_w_i = nrm(ks[14], (DEPTH, 2, LRU_BLOCKS, LRU_BLOCK, LRU_BLOCK), LRU_BLOCK ** -0.5)
    lru_b_i = nrm(ks[15], (DEPTH, 2, GROUP_W), 0.02)
    a0 = jax.random.uniform(ks[16], (DEPTH, 2, GROUP_W), f32, 0.9, 0.999)
    lru_lambda = jnp.log(a0) - jnp.log1p(-a0)
    gdn_conv_w = nrm(ks[17], (DEPTH, CONV_W, 3 * GROUP_W), CONV_W ** -0.5)
    gdn_a_log = jnp.log(jax.random.uniform(ks[18], (DEPTH, 2, GDN_HEADS), f32, 1.0, 16.0))
    dt = jnp.exp(jax.random.uniform(ks[19], (DEPTH, 2, GDN_HEADS), f32, np.log(1e-3), np.log(1e-1)))
    gdn_dt_bias = dt + jnp.log(-jnp.expm1(-dt))
    gdn_norm_g = 1.0 + nrm(ks[20], (DEPTH, HEAD_DIM), 0.02)
    ret_base = jnp.log(2.0 ** (5.0 + jnp.arange(RET_HEADS, dtype=f32)) - 1.0)
    ret_decay_logit = ret_base + nrm(ks[21], (DEPTH, 2, RET_HEADS), 0.1)
    swa_sink = nrm(ks[22], (DEPTH, SWA_HEADS), 0.5)
    return {'x': x, 'c': c, 'ctx': ctx, 'c_ctx': c_ctx, 'w_mod': w_mod, 'b_mod': b_mod,
            'pre_norm_g': pre_norm_g, 'post_norm_g': post_norm_g, 'w_in': w_in, 'w_out': w_out,
            'lru_conv_w': lru_conv_w, 'lru_conv_b': lru_conv_b, 'lru_w_r': lru_w_r, 'lru_b_r': lru_b_r,
            'lru_w_i': lru_w_i, 'lru_b_i': lru_b_i, 'lru_lambda': lru_lambda,
            'gdn_conv_w': gdn_conv_w, 'gdn_a_log': gdn_a_log, 'gdn_dt_bias': gdn_dt_bias,
            'gdn_norm_g': gdn_norm_g, 'ret_decay_logit': ret_decay_logit, 'swa_sink': swa_sink}


def reference(x, c, ctx, c_ctx, w_mod, b_mod, pre_norm_g, post_norm_g, w_in, w_out,
              lru_conv_w, lru_conv_b, lru_w_r, lru_b_r, lru_w_i, lru_b_i, lru_lambda,
              gdn_conv_w, gdn_a_log, gdn_dt_bias, gdn_norm_g, ret_decay_logit, swa_sink):
    bsz, t, _ = x.shape
    rows = t // GRID_W
    row = jnp.repeat(jnp.arange(rows, dtype=jnp.float32), GRID_W)
    col = jnp.tile(jnp.arange(GRID_W, dtype=jnp.float32), rows)
    ang2d = jnp.concatenate([rope_freqs(row, HEAD_DIM // 2), rope_freqs(col, HEAD_DIM // 2)], axis=-1)
    ang1d = rope_freqs(jnp.arange(t, dtype=jnp.float32), HEAD_DIM)
    s_lat = jax.nn.silu(c)
    s_ctx = jax.nn.silu(c_ctx)
    xc = ctx
    for l in range(DEPTH):
        with_ctx = l < DEPTH - 1
        shift, scale, gate = jnp.split((s_lat @ w_mod[l] + b_mod[l])[:, None, :], 3, axis=-1)
        shift_c, scale_c, gate_c = jnp.split(s_ctx @ w_mod[l] + b_mod[l], 3, axis=-1)
        h = rmsnorm(x, pre_norm_g[l]) * (1 + scale) + shift
        hc = rmsnorm(xc, pre_norm_g[l]) * (1 + scale_c) + shift_c
        u = split_in(h @ w_in[l])
        uc = split_in(hc @ w_in[l])
        ya_l, ya_c = mixer_rglru(u, uc, lru_conv_w[l], lru_conv_b[l], lru_w_r[l], lru_b_r[l],
                                 lru_w_i[l], lru_b_i[l], lru_lambda[l], with_ctx)
        yb_l, yb_c = mixer_gdn(u, uc, gdn_conv_w[l], gdn_a_log[l], gdn_dt_bias[l], gdn_norm_g[l], with_ctx)
        yr_l, yr_c = mixer_retention(u, uc, ret_decay_logit[l], ang1d, with_ctx)
        yd_l, yd_c = mixer_swa(u, uc, swa_sink[l], ang2d, with_ctx)
        y = jnp.concatenate([ya_l, yb_l, yr_l, yd_l], axis=-1) @ w_out[l]
        if with_ctx:
            yc = jnp.concatenate([ya_c, yb_c, yr_c, yd_c], axis=-1) @ w_out[l]
            xc = xc + gate_c * rmsnorm(yc, post_norm_g[l])
        x = x + gate * rmsnorm(y, post_norm_g[l])
    return x
```

```python
import functools
import math

import numpy as np
import jax
import jax.numpy as jnp
from jax import lax
from jax.experimental import pallas as pl
from jax.experimental.pallas import tpu as pltpu

F32 = jnp.float32
BF16 = jnp.bfloat16

HEAD_DIM = 64
GROUP_W = 256
N_HEADS = GROUP_W // HEAD_DIM
NORM_EPS = 1e-6
ROPE_BASE = 10000.0
NEG_INF = -1e30
GRID_W = 64
CONV_W = 4
LRU_C = 8.0
GDN_CHUNK = 64
RET_CHUNK = 256
SWA_KV_HEADS = 2
SWA_KV_W = SWA_KV_HEADS * HEAD_DIM
WINDOW = 128
ROW_BLOCK = 256
SUBLANES = 8
VMEM_LIMIT_BYTES = 56 * 1024 * 1024

W_LRU = 2 * GROUP_W
W_GDN = 4 * GROUP_W
W_RET = 4 * GROUP_W
W_SWA = 2 * GROUP_W + 2 * SWA_KV_W
W_AB = 128
IN_W_PACKED = W_LRU + W_GDN + W_RET + W_SWA + W_AB


def _cparams(sem):
    return pltpu.CompilerParams(dimension_semantics=sem, vmem_limit_bytes=VMEM_LIMIT_BYTES)


def _dot(a, b):
    return jnp.dot(a.astype(BF16), b.astype(BF16), preferred_element_type=F32)


def _dot_nt(a, b):
    return lax.dot_general(a.astype(BF16), b.astype(BF16), (((1,), (1,)), ((), ())),
                           preferred_element_type=F32)


def _dot_tn(a, b):
    return lax.dot_general(a.astype(BF16), b.astype(BF16), (((0,), (0,)), ((), ())),
                           preferred_element_type=F32)


def _split_dot(x, w, parts):
    acc = None
    r = x
    for _ in range(parts):
        hi = r.astype(BF16)
        t = jnp.dot(hi, w, preferred_element_type=F32)
        acc = t if acc is None else acc + t
        r = r - hi.astype(F32)
    return acc


def _split_dot_left(w, x, parts):
    acc = None
    r = x
    for _ in range(parts):
        hi = r.astype(BF16)
        t = jnp.dot(w, hi, preferred_element_type=F32)
        acc = t if acc is None else acc + t
        r = r - hi.astype(F32)
    return acc


def _sigmoid(x):
    return 1.0 / (1.0 + jnp.exp(-x))


def _silu(x):
    return x * _sigmoid(x)


def _softplus(x):
    return jnp.maximum(x, 0.0) + jnp.log1p(jnp.exp(-jnp.abs(x)))


def _head_ones():
    r = lax.broadcasted_iota(jnp.int32, (GROUP_W, GROUP_W), 0) // HEAD_DIM
    c = lax.broadcasted_iota(jnp.int32, (GROUP_W, GROUP_W), 1) // HEAD_DIM
    return (r == c).astype(BF16)


def _dwconv_rows(x_ref, r0, n, seg_lo, seg_hi, c0, c1, w_ref):
    width = c1 - c0
    zeros = jnp.zeros((SUBLANES, width), F32)
    prev = x_ref[0, r0 - SUBLANES:r0, c0:c1] if r0 > seg_lo else zeros
    nxt = x_ref[0, r0 + n:r0 + n + SUBLANES, c0:c1] if r0 + n < seg_hi else zeros
    win = jnp.concatenate([prev, x_ref[0, r0:r0 + n, c0:c1], nxt], axis=0)
    total = n + 2 * SUBLANES
    acc = None
    for k in range(CONV_W):
        shift = (2 - k) % total
        tap = win if shift == 0 else pltpu.roll(win, shift, 0)
        term = tap[SUBLANES:SUBLANES + n] * w_ref[k:k + 1, :]
        acc = term if acc is None else acc + term
    return acc


def _row_blocks(lc, total):
    out = []
    for r0 in range(0, total, ROW_BLOCK):
        out.append((r0, 0, lc) if r0 < lc else (r0, lc, total))
    return out


def _mod_body(s_ref, w_ref, b_ref, o_ref):
    s = _silu(s_ref[...])
    o_ref[0] = _dot(s, w_ref[0]) + b_ref[0]


def _modulation(s_rows, w_mod, b_mod):
    depth, d, d3 = w_mod.shape
    rows = s_rows.shape[0]
    nt = d3 // d
    return pl.pallas_call(
        _mod_body,
        grid=(depth, nt),
        in_specs=[pl.BlockSpec((rows, d), lambda l, j: (0, 0)),
                  pl.BlockSpec((1, d, d), lambda l, j: (l, 0, j)),
                  pl.BlockSpec((1, 1, d), lambda l, j: (l, 0, j))],
        out_specs=pl.BlockSpec((1, rows, d), lambda l, j: (l, 0, j)),
        out_shape=jax.ShapeDtypeStruct((depth, rows, d3), F32),
        compiler_params=_cparams(("arbitrary", "arbitrary")),
        name="modulation",
    )(s_rows, w_mod, b_mod.reshape(depth, 1, d3))


def _inproj_body(x_ref, mod_ref, g_ref, w_ref, o_lru, o_gdn, o_ret, o_swa, o_ab, *, lc, nb):
    b = pl.program_id(0)
    i = pl.program_id(1)
    d = x_ref.shape[2]
    x = x_ref[0]
    ms = jnp.mean(x * x, axis=-1, keepdims=True)
    y = x * lax.rsqrt(ms + NORM_EPS) * g_ref[...]
    row = jnp.where(i * ROW_BLOCK < lc, nb, b)
    m = mod_ref[pl.ds(row, 1), :]
    h = (y * (1.0 + m[:, d:2 * d]) + m[:, 0:d]).astype(BF16)
    off = 0
    for o_ref in (o_lru, o_gdn, o_ret, o_swa, o_ab):
        w = o_ref.shape[2]
        o_ref[0] = jnp.dot(h, w_ref[:, off:off + w], preferred_element_type=F32)
        off += w


def _in_proj(x, mod, g, w, *, lc):
    nb, total, d = x.shape
    nt = total // ROW_BLOCK
    widths = (W_LRU, W_GDN, W_RET, W_SWA, W_AB)
    return pl.pallas_call(
        functools.partial(_inproj_body, lc=lc, nb=nb),
        grid=(nb, nt),
        in_specs=[pl.BlockSpec((1, ROW_BLOCK, d), lambda b, i: (b, i, 0)),
                  pl.BlockSpec(mod.shape, lambda b, i: (0, 0)),
                  pl.BlockSpec((1, d), lambda b, i: (0, 0)),
                  pl.BlockSpec(w.shape, lambda b, i: (0, 0))],
        out_specs=[pl.BlockSpec((1, ROW_BLOCK, wd), lambda b, i: (b, i, 0)) for wd in widths],
        out_shape=[jax.ShapeDtypeStruct((nb, total, wd), F32) for wd in widths],
        compiler_params=_cparams(("arbitrary", "arbitrary")),
        name="in_proj",
    )(x, mod, g.reshape(1, d), w)


def _outproj_body(ya, yb, yc, yd, x_ref, mod_ref, g_ref, w_ref, o_ref, *, lc, nb):
    b = pl.program_id(0)
    i = pl.program_id(1)
    d = x_ref.shape[2]
    acc = None
    for k, y_ref in enumerate((ya, yb, yc, yd)):
        t = jnp.dot(y_ref[0], w_ref[k * GROUP_W:(k + 1) * GROUP_W, :], preferred_element_type=F32)
        acc = t if acc is None else acc + t
    ms = jnp.mean(acc * acc, axis=-1, keepdims=True)
    yn = acc * lax.rsqrt(ms + NORM_EPS) * g_ref[...]
    row = jnp.where(i * ROW_BLOCK < lc, nb, b)
    gate = mod_ref[pl.ds(row, 1), 2 * d:3 * d]
    o_ref[0] = x_ref[0] + gate * yn


def _out_proj(ys, x, mod, g, w, *, lc):
    nb, total, d = x.shape
    nt = total // ROW_BLOCK
    yspec = pl.BlockSpec((1, ROW_BLOCK, GROUP_W), lambda b, i: (b, i, 0))
    return pl.pallas_call(
        functools.partial(_outproj_body, lc=lc, nb=nb),
        grid=(nb, nt),
        in_specs=[yspec, yspec, yspec, yspec,
                  pl.BlockSpec((1, ROW_BLOCK, d), lambda b, i: (b, i, 0)),
                  pl.BlockSpec(mod.shape, lambda b, i: (0, 0)),
                  pl.BlockSpec((1, d), lambda b, i: (0, 0)),
                  pl.BlockSpec(w.shape, lambda b, i: (0, 0))],
        out_specs=pl.BlockSpec((1, ROW_BLOCK, d), lambda b, i: (b, i, 0)),
        out_shape=jax.ShapeDtypeStruct(x.shape, F32),
        compiler_params=_cparams(("arbitrary", "arbitrary")),
        name="out_proj",
    )(*ys, x, mod, g.reshape(1, d), w)


def _lru_scan(a_s, b_s, h_s, tile_lo, n_tiles, carry, *, rev, accumulate):
    row = lax.broadcasted_iota(jnp.int32, (SUBLANES, GROUP_W), 0)

    def body(j, carry):
        t = tile_lo + (n_tiles - 1 - j if rev else j)
        r = pl.multiple_of(t * SUBLANES, SUBLANES)
        a = a_s[pl.ds(r, SUBLANES), :]
        b = b_s[pl.ds(r, SUBLANES), :]
        for s in (1, 2, 4):
            if rev:
                ra = pltpu.roll(a, SUBLANES - s, 0)
                rb = pltpu.roll(b, SUBLANES - s, 0)
                m = row < SUBLANES - s
            else:
                ra = pltpu.roll(a, s, 0)
                rb = pltpu.roll(b, s, 0)
                m = row >= s
            b = a * jnp.where(m, rb, 0.0) + b
            a = a * jnp.where(m, ra, 1.0)
        h = a * carry + b
        if accumulate:
            h_s[pl.ds(r, SUBLANES), :] = h_s[pl.ds(r, SUBLANES), :] + h
        else:
            h_s[pl.ds(r, SUBLANES), :] = h
        return h[0:1, :] if rev else h[SUBLANES - 1:SUBLANES, :]

    return lax.fori_loop(0, n_tiles, body, carry)


def _lru_body(u_ref, cw_ref, cb_ref, wg_ref, bg_ref, lam_ref, o_ref, uc_s, a_s, b_s, h_s, *, lc):
    total = u_ref.shape[1]
    blocks = _row_blocks(lc, total)
    for r0, lo, hi in blocks:
        uc_s[r0:r0 + ROW_BLOCK, :] = (
            _dwconv_rows(u_ref, r0, ROW_BLOCK, lo, hi, 0, GROUP_W, cw_ref) + cb_ref[...])
    zero = jnp.zeros((1, GROUP_W), F32)
    for d in range(2):
        sp = _softplus(-lam_ref[d])
        for r0, _, _ in blocks:
            uc = uc_s[r0:r0 + ROW_BLOCK, :]
            gts = _dot(uc, wg_ref[d]) + bg_ref[d]
            r = _sigmoid(gts[:, 0:GROUP_W])
            ig = _sigmoid(gts[:, GROUP_W:2 * GROUP_W])
            a = jnp.exp(-LRU_C * r * sp)
            a_s[r0:r0 + ROW_BLOCK, :] = a
            b_s[r0:r0 + ROW_BLOCK, :] = jnp.sqrt(1.0 - a * a) * (ig * uc)
        ct, tt = lc // SUBLANES, total // SUBLANES
        if d == 0:
            _lru_scan(a_s, b_s, h_s, 0, tt, zero, rev=False, accumulate=False)
        else:
            carry = _lru_scan(a_s, b_s, h_s, 0, ct, zero, rev=True, accumulate=True)
            _lru_scan(a_s, b_s, h_s, ct, tt - ct, carry, rev=True, accumulate=True)
    for r0, _, _ in blocks:
        z = u_ref[0, r0:r0 + ROW_BLOCK, GROUP_W:2 * GROUP_W]
        o_ref[0, r0:r0 + ROW_BLOCK, :] = (h_s[r0:r0 + ROW_BLOCK, :] * _silu(z)).astype(BF16)


def _block_diag(w):
    n, c, _ = w.shape
    eye = jnp.eye(n, dtype=w.dtype)
    return (eye[:, None, :, None] * w[:, :, None, :]).reshape(n * c, n * c)


def _lru(u, conv_w, conv_b, w_r, b_r, w_i, b_i, lam, *, lc):
    nb, total, _ = u.shape
    wg = jnp.stack([jnp.concatenate([_block_diag(w_r[d]), _block_diag(w_i[d])], axis=1)
                    for d in range(2)]).astype(BF16)
    bg = jnp.concatenate([b_r, b_i], axis=-1).reshape(2, 1, 2 * GROUP_W)
    full = lambda shape: pl.BlockSpec(shape, lambda b: (0,) * len(shape))
    return pl.pallas_call(
        functools.partial(_lru_body, lc=lc),
        grid=(nb,),
        in_specs=[pl.BlockSpec((1, total, W_LRU), lambda b: (b, 0, 0)),
                  full((CONV_W, GROUP_W)), full((1, GROUP_W)),
                  full((2, GROUP_W, 2 * GROUP_W)), full((2, 1, 2 * GROUP_W)),
                  full((2, 1, GROUP_W))],
        out_specs=pl.BlockSpec((1, total, GROUP_W), lambda b: (b, 0, 0)),
        out_shape=jax.ShapeDtypeStruct((nb, total, GROUP_W), BF16),
        scratch_shapes=[pltpu.VMEM((total, GROUP_W), F32) for _ in range(4)],
        compiler_params=_cparams(("arbitrary",)),
        name="lru",
    )(u, conv_w, conv_b.reshape(1, GROUP_W), wg, bg, lam.reshape(2, 1, GROUP_W))


def _rope(x, cos, sin_signed):
    half = HEAD_DIM // 2
    outs = []
    for c0 in range(0, x.shape[1], 128):
        xs = x[:, c0:c0 + 128]
        lane = lax.broadcasted_iota(jnp.int32, xs.shape, 1)
        swapped = jnp.where((lane % HEAD_DIM) < half,
                            pltpu.roll(xs, 128 - half, 1), pltpu.roll(xs, half, 1))
        outs.append(xs * cos[:, c0:c0 + 128] + swapped * sin_signed[:, c0:c0 + 128])
    return outs[0] if len(outs) == 1 else jnp.concatenate(outs, axis=1)


def _ret_body(u_ref, cos_ref, sin_ref, lg_ref, o_ref, qr_s, kr_s, ds_s, st_s, m_s, *, lc):
    total = u_ref.shape[1]
    c = RET_CHUNK
    nch = total // c
    n_ctx = lc // c
    gw = GROUP_W
    ones_bd = _head_ones()
    bd_mask = ones_bd > 0
    lg = -_softplus(-lg_ref[...])
    lgf, lgb = lg[0:1, :], lg[1:2, :]
    pos = lax.broadcasted_iota(jnp.int32, (c, 1), 0).astype(F32)
    qdec_f = jnp.exp((pos + 1.0) * lgf)
    qdec_b = jnp.exp((c - pos) * lgb)
    kdec_f = jnp.exp((c - 1.0 - pos) * lgf)
    kdec_b = jnp.exp(pos * lgb)
    cdec_f = jnp.exp(float(c) * lgf)
    cdec_b = jnp.exp(float(c) * lgb)
    dij = (lax.broadcasted_iota(jnp.int32, (c, c), 0)
           - lax.broadcasted_iota(jnp.int32, (c, c), 1)).astype(F32)
    for h in range(N_HEADS):
        lf = lgf[:, h * HEAD_DIM:h * HEAD_DIM + 1]
        lb = lgb[:, h * HEAD_DIM:h * HEAD_DIM + 1]
        fwd = jnp.exp(jnp.maximum(dij, 0.0) * lf)
        bwd = jnp.exp(jnp.maximum(-dij, 0.0) * lb)
        m_s[h] = jnp.where(dij > 0, fwd, jnp.where(dij < 0, bwd, 2.0))

    for ci in range(nch):
        r0 = ci * c
        q = u_ref[0, r0:r0 + c, 0:gw]
        k = u_ref[0, r0:r0 + c, gw:2 * gw]
        v = u_ref[0, r0:r0 + c, 2 * gw:3 * gw]
        cos = cos_ref[r0:r0 + c, :]
        sin = sin_ref[r0:r0 + c, :]
        qr = _rope(q, cos, sin)
        kr = _rope(k, cos, sin) * (HEAD_DIM ** -0.5)
        qr_s[r0:r0 + c, :] = qr
        kr_s[r0:r0 + c, :] = kr
        ds_s[0, ci] = jnp.where(bd_mask, _dot_tn(kr * kdec_f, v), 0.0)
        ds_s[1, ci] = jnp.where(bd_mask, _dot_tn(kr * kdec_b, v), 0.0)

    s = jnp.zeros((gw, gw), F32)
    for ci in range(nch):
        st_s[0, ci] = s
        s = s * cdec_f + ds_s[0, ci]
    s = jnp.zeros((gw, gw), F32)
    for ci in list(range(n_ctx - 1, -1, -1)) + list(range(nch - 1, n_ctx - 1, -1)):
        st_s[1, ci] = s
        s = s * cdec_b + ds_s[1, ci]

    for ci in range(nch):
        r0 = ci * c
        qr = qr_s[r0:r0 + c, :]
        kr = kr_s[r0:r0 + c, :]
        v = u_ref[0, r0:r0 + c, 2 * gw:3 * gw]
        z = u_ref[0, r0:r0 + c, 3 * gw:4 * gw]
        outs = []
        for h in range(N_HEADS):
            hs = slice(h * HEAD_DIM, (h + 1) * HEAD_DIM)
            att = _dot_nt(qr[:, hs], kr[:, hs]) * m_s[h]
            outs.append(_dot(att, v[:, hs]))
        o = jnp.concatenate(outs, axis=1)
        o = o + _dot(qr * qdec_f, st_s[0, ci]) + _dot(qr * qdec_b, st_s[1, ci])
        mu = _split_dot(o, ones_bd, 2) * (1.0 / HEAD_DIM)
        dlt = o - mu
        var = _split_dot(dlt * dlt, ones_bd, 2) * (1.0 / HEAD_DIM)
        y = dlt * lax.rsqrt(var + NORM_EPS) * _silu(z)
        o_ref[0, r0:r0 + c, :] = y.astype(BF16)


def _ret(u, cos, sin, decay_logit, *, lc):
    nb, total, _ = u.shape
    nch = total // RET_CHUNK
    lg = jnp.repeat(decay_logit, HEAD_DIM, axis=-1)
    full = lambda shape: pl.BlockSpec(shape, lambda b: (0,) * len(shape))
    return pl.pallas_call(
        functools.partial(_ret_body, lc=lc),
        grid=(nb,),
        in_specs=[pl.BlockSpec((1, total, W_RET), lambda b: (b, 0, 0)),
                  full((total, GROUP_W)), full((total, GROUP_W)), full((2, GROUP_W))],
        out_specs=pl.BlockSpec((1, total, GROUP_W), lambda b: (b, 0, 0)),
        out_shape=jax.ShapeDtypeStruct((nb, total, GROUP_W), BF16),
        scratch_shapes=[pltpu.VMEM((total, GROUP_W), F32),
                        pltpu.VMEM((total, GROUP_W), F32),
                        pltpu.VMEM((2, nch, GROUP_W, GROUP_W), F32),
                        pltpu.VMEM((2, nch, GROUP_W, GROUP_W), F32),
                        pltpu.VMEM((N_HEADS, RET_CHUNK, RET_CHUNK), F32)],
        compiler_params=_cparams(("arbitrary",)),
        name="retention",
    )(u, cos, sin, lg)


def _swa_body(sink_ref, u_ref, cos_ref, sin_ref, o_ref, q_s, k_s, v_s, *, lc):
    total = u_ref.shape[1]
    t_lat = total - lc
    gw = GROUP_W
    blk = WINDOW
    nblk = t_lat // blk
    kv0 = lc + blk
    scale = HEAD_DIM ** -0.5
    zpad = jnp.zeros((blk, SWA_KV_W), F32)
    for s_ref in (k_s, v_s):
        s_ref[lc:lc + blk, :] = zpad
        s_ref[kv0 + t_lat:kv0 + t_lat + blk, :] = zpad
    for r0, _, _ in _row_blocks(lc, total):
        cos = cos_ref[r0:r0 + ROW_BLOCK, :]
        sin = sin_ref[r0:r0 + ROW_BLOCK, :]
        q_s[r0:r0 + ROW_BLOCK, :] = _rope(u_ref[0, r0:r0 + ROW_BLOCK, 0:gw], cos, sin) * scale
        kr = _rope(u_ref[0, r0:r0 + ROW_BLOCK, gw:gw + SWA_KV_W], cos[:, 0:SWA_KV_W], sin[:, 0:SWA_KV_W])
        dst = r0 if r0 < lc else r0 + blk
        k_s[dst:dst + ROW_BLOCK, :] = kr
        v_s[dst:dst + ROW_BLOCK, :] = u_ref[0, r0:r0 + ROW_BLOCK, gw + SWA_KV_W:gw + 2 * SWA_KV_W]

    grp = N_HEADS // SWA_KV_HEADS

    def attend(q2, keys, vals, masks, sink_col):
        scores = []
        mx = sink_col
        for kk, mk in zip(keys, masks):
            s = _dot_nt(q2, kk)
            if mk is not None:
                s = jnp.where(mk, s, NEG_INF)
            scores.append(s)
            mx = jnp.maximum(mx, jnp.max(s, axis=-1, keepdims=True))
        den = jnp.exp(sink_col - mx)
        acc = None
        for s, vv in zip(scores, vals):
            e = jnp.exp(s - mx)
            den = den + jnp.sum(e, axis=-1, keepdims=True)
            t = _dot(e, vv)
            acc = t if acc is None else acc + t
        return acc / den

    def sink_column(hk, rows_per_head):
        row = lax.broadcasted_iota(jnp.int32, (grp * rows_per_head, 1), 0)
        col = jnp.full((grp * rows_per_head, 1), sink_ref[hk * grp], F32)
        for g in range(1, grp):
            col = jnp.where(row >= g * rows_per_head, sink_ref[hk * grp + g], col)
        return col

    if True:
        outs = [None] * N_HEADS
        for hk in range(SWA_KV_HEADS):
            ks = slice(hk * HEAD_DIM, (hk + 1) * HEAD_DIM)
            q2 = jnp.concatenate(
                [q_s[0:lc, (hk * grp + g) * HEAD_DIM:(hk * grp + g + 1) * HEAD_DIM] for g in range(grp)],
                axis=0)
            o2 = attend(q2, [k_s[0:lc, ks]], [v_s[0:lc, ks]], [None], sink_column(hk, lc))
            for g in range(grp):
                outs[hk * grp + g] = o2[g * lc:(g + 1) * lc, :]
        o = jnp.concatenate(outs, axis=1)
        z = u_ref[0, 0:lc, gw + 2 * SWA_KV_W:2 * gw + 2 * SWA_KV_W]
        o_ref[0, 0:lc, :] = (o * _silu(z)).astype(BF16)

    qi = lax.broadcasted_iota(jnp.int32, (grp * blk, 3 * blk), 0) % blk
    kj = lax.broadcasted_iota(jnp.int32, (grp * blk, 3 * blk), 1)
    in_win = jnp.abs(kj - blk - qi) <= WINDOW

    def block_body(n, carry):
        kpos = n * blk - blk + kj
        mask = in_win & (kpos >= 0) & (kpos < t_lat)
        qrow = pl.multiple_of(lc + n * blk, blk)
        wrow = pl.multiple_of(lc + n * blk, blk)
        outs = [None] * N_HEADS
        for hk in range(SWA_KV_HEADS):
            ks = slice(hk * HEAD_DIM, (hk + 1) * HEAD_DIM)
            q2 = jnp.concatenate(
                [q_s[pl.ds(qrow, blk), (hk * grp + g) * HEAD_DIM:(hk * grp + g + 1) * HEAD_DIM]
                 for g in range(grp)], axis=0)
            o2 = attend(q2,
                        [k_s[pl.ds(wrow, 3 * blk), ks], k_s[0:lc, ks]],
                        [v_s[pl.ds(wrow, 3 * blk), ks], v_s[0:lc, ks]],
                        [mask, None], sink_column(hk, blk))
            for g in range(grp):
                outs[hk * grp + g] = o2[g * blk:(g + 1) * blk, :]
        o = jnp.concatenate(outs, axis=1)
        z = u_ref[0, pl.ds(qrow, blk), gw + 2 * SWA_KV_W:2 * gw + 2 * SWA_KV_W]
        o_ref[0, pl.ds(qrow, blk), :] = (o * _silu(z)).astype(BF16)
        return carry

    lax.fori_loop(0, nblk, block_body, 0)


def _swa(u, cos, sin, sink, *, lc):
    nb, total, _ = u.shape
    t_lat = total - lc
    full = lambda shape: pl.BlockSpec(shape, lambda b, s: (0,) * len(shape))
    grid_spec = pltpu.PrefetchScalarGridSpec(
        num_scalar_prefetch=1,
        grid=(nb,),
        in_specs=[pl.BlockSpec((1, total, W_SWA), lambda b, s: (b, 0, 0)),
                  full((total, GROUP_W)), full((total, GROUP_W))],
        out_specs=pl.BlockSpec((1, total, GROUP_W), lambda b, s: (b, 0, 0)),
        scratch_shapes=[pltpu.VMEM((total, GROUP_W), F32),
                        pltpu.VMEM((total + 2 * WINDOW, SWA_KV_W), F32),
                        pltpu.VMEM((total + 2 * WINDOW, SWA_KV_W), F32)],
    )
    return pl.pallas_call(
        functools.partial(_swa_body, lc=lc),
        grid_spec=grid_spec,
        out_shape=jax.ShapeDtypeStruct((nb, total, GROUP_W), BF16),
        compiler_params=_cparams(("arbitrary",)),
        name="swa",
    )(sink, u, cos, sin)


def _gdn_body(u_ref, ab_ref, cw_ref, avec_ref, dtb_ref, ng_ref, o_ref,
              q_s, k_s, v_s, cs_s, gt_s, of_s, st_s, *, lc):
    total = u_ref.shape[1]
    gw = GROUP_W
    c = GDN_CHUNK
    nch = total // c
    n_ctx = lc // c
    per_blk = ROW_BLOCK // c
    ones_bd = _head_ones()

    ri = lax.broadcasted_iota(jnp.int32, (ROW_BLOCK, ROW_BLOCK), 0)
    ci_ = lax.broadcasted_iota(jnp.int32, (ROW_BLOCK, ROW_BLOCK), 1)
    same_chunk = (ri // c) == (ci_ // c)
    tri_lo = (same_chunk & (ri >= ci_)).astype(BF16)
    tri_up = (same_chunk & (ri <= ci_)).astype(BF16)
    lane = lax.broadcasted_iota(jnp.int32, (ROW_BLOCK, W_AB), 1)

    for bi, (r0, lo, hi) in enumerate(_row_blocks(lc, total)):
        h = _silu(_dwconv_rows(u_ref, r0, ROW_BLOCK, lo, hi, 0, 3 * gw, cw_ref))
        hq, hk, hv = h[:, 0:gw], h[:, gw:2 * gw], h[:, 2 * gw:3 * gw]
        ssq = _split_dot(hq * hq, ones_bd, 2)
        q_s[r0:r0 + ROW_BLOCK, :] = hq * lax.rsqrt(ssq + NORM_EPS) * (HEAD_DIM ** -0.5)
        ssk = _split_dot(hk * hk, ones_bd, 2)
        k_s[r0:r0 + ROW_BLOCK, :] = hk * lax.rsqrt(ssk + NORM_EPS)
        v_s[r0:r0 + ROW_BLOCK, :] = hv
        ab = ab_ref[0, r0:r0 + ROW_BLOCK, :]
        g = -avec_ref[...] * _softplus(ab + dtb_ref[...])
        g = jnp.where(lane < 2 * N_HEADS, g, 0.0)
        beta = _sigmoid(ab)
        cs_lo = _split_dot_left(tri_lo, g, 3)
        cs_up = _split_dot_left(tri_up, g, 3)
        cs = jnp.where(lane < N_HEADS, cs_lo, jnp.where(lane < 2 * N_HEADS, cs_up, beta))
        cs_s[r0:r0 + ROW_BLOCK, :] = cs
        cst = cs.T
        for cc in range(per_blk):
            row0 = (bi * per_blk + cc) * SUBLANES
            gt_s[row0:row0 + SUBLANES, :] = cst[0:SUBLANES, cc * c:(cc + 1) * c]

    ii = lax.broadcasted_iota(jnp.int32, (c, c), 0)
    jj = lax.broadcasted_iota(jnp.int32, (c, c), 1)
    eye = (ii == jj).astype(F32)
    n_sq = c.bit_length() - 2

    def chunk_step(ch, rev):
        r = pl.multiple_of(ch * c, c)
        kc = k_s[pl.ds(r, c), :]
        qc = q_s[pl.ds(r, c), :]
        vc = v_s[pl.ds(r, c), :]
        csc = cs_s[pl.ds(r, c), :]
        gt = gt_s[pl.ds(pl.multiple_of(ch * SUBLANES, SUBLANES), SUBLANES), :]
        incl = (ii <= jj) if rev else (ii >= jj)
        strict = (ii < jj) if rev else (ii > jj)
        last = 0 if rev else c - 1
        outs = []
        for h in range(N_HEADS):
            hd = h + (N_HEADS if rev else 0)
            hs = slice(h * HEAD_DIM, (h + 1) * HEAD_DIM)
            col = csc[:, hd:hd + 1]
            rowv = gt[hd:hd + 1, :]
            bcol = csc[:, 2 * N_HEADS + hd:2 * N_HEADS + hd + 1]
            decay = jnp.where(incl, jnp.exp(jnp.where(incl, col - rowv, 0.0)), 0.0)
            kh, qh, vh = kc[:, hs], qc[:, hs], vc[:, hs]
            kk = _dot_nt(kh * bcol, kh)
            qk = _dot_nt(qh, kh)
            n = -jnp.where(strict, kk * decay, 0.0)
            att = qk * decay
            a_mat = eye - n
            p = eye + n
            for _ in range(n_sq):
                p = p + _dot(p, eye - _dot(a_mat, p))
            eg = jnp.exp(col)
            glast = csc[last:last + 1, hd:hd + 1]
            rhs = jnp.concatenate([vh * bcol, kh * (bcol * eg)], axis=1)
            uw = _dot(p, rhs)
            u_val, w_dec = uw[:, 0:HEAD_DIM], uw[:, HEAD_DIM:2 * HEAD_DIM]
            s = st_s[h]
            wq = _dot(jnp.concatenate([w_dec, qh * eg], axis=0), s)
            v_new = u_val - wq[0:c, :]
            outs.append(wq[c:2 * c, :] + _dot(att, v_new))
            k_tail = kh * jnp.exp(glast - col)
            st_s[h] = s * jnp.exp(glast) + _dot_tn(k_tail, v_new)
        return r, jnp.concatenate(outs, axis=1)

    def fwd_body(i, carry):
        r, o = chunk_step(i, False)
        of_s[pl.ds(r, c), :] = o
        return carry

    def bwd_body(i, carry):
        ch = jnp.where(i < n_ctx, n_ctx - 1 - i, nch - 1 + n_ctx - i)
        r, o = chunk_step(ch, True)
        of_s[pl.ds(r, c), :] = of_s[pl.ds(r, c), :] + o
        return carry

    zeros_state = jnp.zeros((N_HEADS, HEAD_DIM, HEAD_DIM), F32)
    st_s[...] = zeros_state
    lax.fori_loop(0, nch, fwd_body, 0)
    st_s[...] = zeros_state
    lax.fori_loop(0, nch, bwd_body, 0)

    for r0, _, _ in _row_blocks(lc, total):
        o = of_s[r0:r0 + ROW_BLOCK, :]
        ms = _split_dot(o * o, ones_bd, 2) * (1.0 / HEAD_DIM)
        z = u_ref[0, r0:r0 + ROW_BLOCK, 3 * gw:4 * gw]
        y = o * lax.rsqrt(ms + NORM_EPS) * ng_ref[...] * _silu(z)
        o_ref[0, r0:r0 + ROW_BLOCK, :] = y.astype(BF16)


def _gdn(u, ab, conv_w, a_log, dt_bias, norm_g, *, lc):
    nb, total, _ = u.shape
    nch = total // GDN_CHUNK
    pad = W_AB - 2 * N_HEADS
    avec = jnp.pad(jnp.exp(a_log.reshape(1, 2 * N_HEADS)), ((0, 0), (0, pad)))
    dtb = jnp.pad(dt_bias.reshape(1, 2 * N_HEADS), ((0, 0), (0, pad)))
    ng = jnp.tile(norm_g.reshape(1, HEAD_DIM), (1, N_HEADS))
    full = lambda shape: pl.BlockSpec(shape, lambda b: (0,) * len(shape))
    return pl.pallas_call(
        functools.partial(_gdn_body, lc=lc),
        grid=(nb,),
        in_specs=[pl.BlockSpec((1, total, W_GDN), lambda b: (b, 0, 0)),
                  pl.BlockSpec((1, total, W_AB), lambda b: (b, 0, 0)),
                  full((CONV_W, 3 * GROUP_W)), full((1, W_AB)), full((1, W_AB)),
                  full((1, GROUP_W))],
        out_specs=pl.BlockSpec((1, total, GROUP_W), lambda b: (b, 0, 0)),
        out_shape=jax.ShapeDtypeStruct((nb, total, GROUP_W), BF16),
        scratch_shapes=[pltpu.VMEM((total, GROUP_W), F32),
                        pltpu.VMEM((total, GROUP_W), F32),
                        pltpu.VMEM((total, GROUP_W), F32),
                        pltpu.VMEM((total, W_AB), F32),
                        pltpu.VMEM((nch * SUBLANES, GDN_CHUNK), F32),
                        pltpu.VMEM((total, GROUP_W), F32),
                        pltpu.VMEM((N_HEADS, HEAD_DIM, HEAD_DIM), F32)],
        compiler_params=_cparams(("arbitrary",)),
        name="gdn",
    )(u, ab, conv_w, avec, dtb, ng)


def _pack_w_in(w_in):
    gw = GROUP_W
    o_gdn = 2 * gw
    o_ab = o_gdn + 4 * gw
    o_ret = o_ab + 4 * N_HEADS
    o_swa = o_ret + 4 * gw
    end = o_swa + W_SWA
    assert end == w_in.shape[-1]
    pad = jnp.zeros(w_in.shape[:-1] + (W_AB - 4 * N_HEADS,), w_in.dtype)
    return jnp.concatenate([w_in[..., 0:o_gdn], w_in[..., o_gdn:o_ab], w_in[..., o_ret:o_swa],
                            w_in[..., o_swa:end], w_in[..., o_ab:o_ret], pad], axis=-1).astype(BF16)


def _rope_tables(ang, lc):
    cos = jnp.cos(ang)
    sin = jnp.sin(ang)
    cos_h = jnp.concatenate([cos, cos], axis=-1)
    sin_h = jnp.concatenate([-sin, sin], axis=-1)
    cos_t = jnp.tile(cos_h, (1, N_HEADS))
    sin_t = jnp.tile(sin_h, (1, N_HEADS))
    cos_t = jnp.concatenate([jnp.ones((lc, GROUP_W), F32), cos_t], axis=0)
    sin_t = jnp.concatenate([jnp.zeros((lc, GROUP_W), F32), sin_t], axis=0)
    return cos_t, sin_t


def _rope_freqs(pos, n):
    inv = ROPE_BASE ** (-jnp.arange(0, n, 2, dtype=F32) / n)
    return pos[:, None] * inv[None, :]


def kernel(x, c, ctx, c_ctx, w_mod, b_mod, pre_norm_g, post_norm_g, w_in, w_out, lru_conv_w, lru_conv_b, lru_w_r, lru_b_r, lru_w_i, lru_b_i, lru_lambda, gdn_conv_w, gdn_a_log, gdn_dt_bias, gdn_norm_g, ret_decay_logit, swa_sink):
    nb, t, d = x.shape
    lc = ctx.shape[1]
    depth = w_mod.shape[0]
    assert t % ROW_BLOCK == 0 and lc % ROW_BLOCK == 0 and d == 4 * GROUP_W

    rows = t // GRID_W
    row = jnp.repeat(jnp.arange(rows, dtype=F32), GRID_W)
    col = jnp.tile(jnp.arange(GRID_W, dtype=F32), rows)
    ang2d = jnp.concatenate([_rope_freqs(row, HEAD_DIM // 2), _rope_freqs(col, HEAD_DIM // 2)], axis=-1)
    ang1d = _rope_freqs(jnp.arange(t, dtype=F32), HEAD_DIM)
    cos1, sin1 = _rope_tables(ang1d, lc)
    cos2, sin2 = _rope_tables(ang2d, lc)

    mod_rows = -(-(nb + 1) // SUBLANES) * SUBLANES
    s_rows = jnp.concatenate([c, c_ctx[None, :], jnp.zeros((mod_rows - nb - 1, d), F32)], axis=0)
    mod = _modulation(s_rows, w_mod, b_mod)

    w_in_p = _pack_w_in(w_in)
    w_out_b = w_out.astype(BF16)
    xs = jnp.concatenate([ctx, x], axis=1)
    for l in range(depth):
        u_lru, u_gdn, u_ret, u_swa, u_ab = _in_proj(xs, mod[l], pre_norm_g[l], w_in_p[l], lc=lc)
        ya = _lru(u_lru, lru_conv_w[l], lru_conv_b[l], lru_w_r[l], lru_b_r[l], lru_w_i[l], lru_b_i[l],
                  lru_lambda[l], lc=lc)
        yb = _gdn(u_gdn, u_ab, gdn_conv_w[l], gdn_a_log[l], gdn_dt_bias[l], gdn_norm_g[l], lc=lc)
        yc = _ret(u_ret, cos1, sin1, ret_decay_logit[l], lc=lc)
        yd = _swa(u_swa, cos2, sin2, swa_sink[l], lc=lc)
        xs = _out_proj((ya, yb, yc, yd), xs, mod[l], post_norm_g[l], w_out_b[l], lc=lc)
    return xs[:, lc:, :]
```

```python
import functools

import jax
import jax.numpy as jnp
from jax import lax
from jax.experimental import pallas as pl
from jax.experimental.pallas import tpu as pltpu

F32 = jnp.float32
BF16 = jnp.bfloat16

HEAD_DIM = 64
GROUP_W = 256
N_HEADS = GROUP_W // HEAD_DIM
NORM_EPS = 1e-6
ROPE_BASE = 10000.0
NEG_INF = -1e30
GRID_W = 64
CONV_W = 4
LRU_C = 8.0
GDN_CHUNK = 64
GDN_B1_CHUNKS = 2
RET_CHUNK = 256
SWA_KV_HEADS = 2
SWA_KV_W = SWA_KV_HEADS * HEAD_DIM
WINDOW = 128
ROW_BLOCK = 256
SUBLANES = 8
VMEM_LIMIT_BYTES = 56 * 1024 * 1024

W_LRU = 2 * GROUP_W
W_GDN = 4 * GROUP_W
W_RET = 4 * GROUP_W
W_SWA = 2 * GROUP_W + 2 * SWA_KV_W
W_AB = 128
IN_W_PACKED = W_LRU + W_GDN + W_RET + W_SWA + W_AB


def _cparams(sem):
    return pltpu.CompilerParams(dimension_semantics=sem, vmem_limit_bytes=VMEM_LIMIT_BYTES)


def _dot(a, b):
    return jnp.dot(a.astype(BF16), b.astype(BF16), preferred_element_type=F32)


def _dot_nt(a, b):
    return lax.dot_general(a.astype(BF16), b.astype(BF16), (((1,), (1,)), ((), ())),
                           preferred_element_type=F32)


def _dot_tn(a, b):
    return lax.dot_general(a.astype(BF16), b.astype(BF16), (((0,), (0,)), ((), ())),
                           preferred_element_type=F32)


def _split_dot(x, w, parts):
    acc = None
    r = x
    for _ in range(parts):
        hi = r.astype(BF16)
        t = jnp.dot(hi, w, preferred_element_type=F32)
        acc = t if acc is None else acc + t
        r = r - hi.astype(F32)
    return acc


def _split_dot_left(w, x, parts):
    acc = None
    r = x
    for _ in range(parts):
        hi = r.astype(BF16)
        t = jnp.dot(w, hi, preferred_element_type=F32)
        acc = t if acc is None else acc + t
        r = r - hi.astype(F32)
    return acc


def _sigmoid(x):
    return 1.0 / (1.0 + jnp.exp(-x))


def _silu(x):
    return x * _sigmoid(x)


def _softplus(x):
    return jnp.maximum(x, 0.0) + jnp.log1p(jnp.exp(-jnp.abs(x)))


def _head_ones():
    r = lax.broadcasted_iota(jnp.int32, (GROUP_W, GROUP_W), 0) // HEAD_DIM
    c = lax.broadcasted_iota(jnp.int32, (GROUP_W, GROUP_W), 1) // HEAD_DIM
    return (r == c).astype(BF16)


def _dwconv_rows(x_ref, r0, n, seg_lo, seg_hi, c0, c1, w_ref):
    width = c1 - c0
    zeros = jnp.zeros((SUBLANES, width), F32)
    prev = x_ref[0, r0 - SUBLANES:r0, c0:c1] if r0 > seg_lo else zeros
    nxt = x_ref[0, r0 + n:r0 + n + SUBLANES, c0:c1] if r0 + n < seg_hi else zeros
    win = jnp.concatenate([prev, x_ref[0, r0:r0 + n, c0:c1], nxt], axis=0)
    total = n + 2 * SUBLANES
    acc = None
    for k in range(CONV_W):
        shift = (2 - k) % total
        tap = win if shift == 0 else pltpu.roll(win, shift, 0)
        term = tap[SUBLANES:SUBLANES + n] * w_ref[k:k + 1, :]
        acc = term if acc is None else acc + term
    return acc


def _row_blocks(lc, total):
    out = []
    for r0 in range(0, total, ROW_BLOCK):
        out.append((r0, 0, lc) if r0 < lc else (r0, lc, total))
    return out


def _mod_body(s_ref, w_ref, b_ref, o_ref):
    s = _silu(s_ref[...])
    o_ref[0] = _dot(s, w_ref[0]) + b_ref[0]


def _modulation(s_rows, w_mod, b_mod):
    depth, d, d3 = w_mod.shape
    rows = s_rows.shape[0]
    nt = d3 // d
    return pl.pallas_call(
        _mod_body,
        grid=(depth, nt),
        in_specs=[pl.BlockSpec((rows, d), lambda l, j: (0, 0)),
                  pl.BlockSpec((1, d, d), lambda l, j: (l, 0, j)),
                  pl.BlockSpec((1, 1, d), lambda l, j: (l, 0, j))],
        out_specs=pl.BlockSpec((1, rows, d), lambda l, j: (l, 0, j)),
        out_shape=jax.ShapeDtypeStruct((depth, rows, d3), F32),
        compiler_params=_cparams(("arbitrary", "arbitrary")),
        name="modulation",
    )(s_rows, w_mod, b_mod.reshape(depth, 1, d3))


def _inproj_body(x_ref, mod_ref, g_ref, w_ref, o_lru, o_gdn, o_ret, o_swa, o_ab, *, lc, nb):
    b = pl.program_id(0)
    i = pl.program_id(1)
    d = x_ref.shape[2]
    x = x_ref[0]
    ms = jnp.mean(x * x, axis=-1, keepdims=True)
    y = x * lax.rsqrt(ms + NORM_EPS) * g_ref[...]
    row = jnp.where(i * ROW_BLOCK < lc, nb, b)
    m = mod_ref[pl.ds(row, 1), :]
    h = (y * (1.0 + m[:, d:2 * d]) + m[:, 0:d]).astype(BF16)
    off = 0
    for o_ref in (o_lru, o_gdn, o_ret, o_swa, o_ab):
        w = o_ref.shape[2]
        o_ref[0] = jnp.dot(h, w_ref[:, off:off + w], preferred_element_type=F32)
        off += w


def _in_proj(x, mod, g, w, *, lc):
    nb, total, d = x.shape
    nt = total // ROW_BLOCK
    widths = (W_LRU, W_GDN, W_RET, W_SWA, W_AB)
    return pl.pallas_call(
        functools.partial(_inproj_body, lc=lc, nb=nb),
        grid=(nb, nt),
        in_specs=[pl.BlockSpec((1, ROW_BLOCK, d), lambda b, i: (b, i, 0)),
                  pl.BlockSpec(mod.shape, lambda b, i: (0, 0)),
                  pl.BlockSpec((1, d), lambda b, i: (0, 0)),
                  pl.BlockSpec(w.shape, lambda b, i: (0, 0))],
        out_specs=[pl.BlockSpec((1, ROW_BLOCK, wd), lambda b, i: (b, i, 0)) for wd in widths],
        out_shape=[jax.ShapeDtypeStruct((nb, total, wd), F32) for wd in widths],
        compiler_params=_cparams(("arbitrary", "arbitrary")),
        name="in_proj",
    )(x, mod, g.reshape(1, d), w)


def _outproj_body(ya, yb, yc, yd, x_ref, mod_ref, g_ref, w_ref, o_ref, *, lc, nb):
    b = pl.program_id(0)
    i = pl.program_id(1)
    d = x_ref.shape[2]
    acc = None
    for k, y_ref in enumerate((ya, yb, yc, yd)):
        t = jnp.dot(y_ref[0], w_ref[k * GROUP_W:(k + 1) * GROUP_W, :], preferred_element_type=F32)
        acc = t if acc is None else acc + t
    ms = jnp.mean(acc * acc, axis=-1, keepdims=True)
    yn = acc * lax.rsqrt(ms + NORM_EPS) * g_ref[...]
    row = jnp.where(i * ROW_BLOCK < lc, nb, b)
    gate = mod_ref[pl.ds(row, 1), 2 * d:3 * d]
    o_ref[0] = x_ref[0] + gate * yn


def _out_proj(ys, x, mod, g, w, *, lc):
    nb, total, d = x.shape
    nt = total // ROW_BLOCK
    yspec = pl.BlockSpec((1, ROW_BLOCK, GROUP_W), lambda b, i: (b, i, 0))
    return pl.pallas_call(
        functools.partial(_outproj_body, lc=lc, nb=nb),
        grid=(nb, nt),
        in_specs=[yspec, yspec, yspec, yspec,
                  pl.BlockSpec((1, ROW_BLOCK, d), lambda b, i: (b, i, 0)),
                  pl.BlockSpec(mod.shape, lambda b, i: (0, 0)),
                  pl.BlockSpec((1, d), lambda b, i: (0, 0)),
                  pl.BlockSpec(w.shape, lambda b, i: (0, 0))],
        out_specs=pl.BlockSpec((1, ROW_BLOCK, d), lambda b, i: (b, i, 0)),
        out_shape=jax.ShapeDtypeStruct(x.shape, F32),
        compiler_params=_cparams(("arbitrary", "arbitrary")),
        name="out_proj",
    )(*ys, x, mod, g.reshape(1, d), w)


def _lru_scan(a_s, b_s, h_s, tile_lo, n_tiles, carry, *, rev, accumulate):
    row = lax.broadcasted_iota(jnp.int32, (SUBLANES, GROUP_W), 0)

    def body(j, carry):
        t = tile_lo + (n_tiles - 1 - j if rev else j)
        r = pl.multiple_of(t * SUBLANES, SUBLANES)
        a = a_s[pl.ds(r, SUBLANES), :]
        b = b_s[pl.ds(r, SUBLANES), :]
        for s in (1, 2, 4):
            if rev:
                ra = pltpu.roll(a, SUBLANES - s, 0)
                rb = pltpu.roll(b, SUBLANES - s, 0)
                m = row < SUBLANES - s
            else:
                ra = pltpu.roll(a, s, 0)
                rb = pltpu.roll(b, s, 0)
                m = row >= s
            b = a * jnp.where(m, rb, 0.0) + b
            a = a * jnp.where(m, ra, 1.0)
        h = a * carry + b
        if accumulate:
            h_s[pl.ds(r, SUBLANES), :] = h_s[pl.ds(r, SUBLANES), :] + h
        else:
            h_s[pl.ds(r, SUBLANES), :] = h
        return h[0:1, :] if rev else h[SUBLANES - 1:SUBLANES, :]

    return lax.fori_loop(0, n_tiles, body, carry)


def _lru_body(u_ref, cw_ref, cb_ref, wg_ref, bg_ref, lam_ref, o_ref, uc_s, a_s, b_s, h_s, *, lc):
    total = u_ref.shape[1]
    blocks = _row_blocks(lc, total)
    for r0, lo, hi in blocks:
        uc_s[r0:r0 + ROW_BLOCK, :] = (
            _dwconv_rows(u_ref, r0, ROW_BLOCK, lo, hi, 0, GROUP_W, cw_ref) + cb_ref[...])
    zero = jnp.zeros((1, GROUP_W), F32)
    for d in range(2):
        sp = _softplus(-lam_ref[d])
        for r0, _, _ in blocks:
            uc = uc_s[r0:r0 + ROW_BLOCK, :]
            gts = _dot(uc, wg_ref[d]) + bg_ref[d]
            r = _sigmoid(gts[:, 0:GROUP_W])
            ig = _sigmoid(gts[:, GROUP_W:2 * GROUP_W])
            a = jnp.exp(-LRU_C * r * sp)
            a_s[r0:r0 + ROW_BLOCK, :] = a
            b_s[r0:r0 + ROW_BLOCK, :] = jnp.sqrt(1.0 - a * a) * (ig * uc)
        ct, tt = lc // SUBLANES, total // SUBLANES
        if d == 0:
            _lru_scan(a_s, b_s, h_s, 0, tt, zero, rev=False, accumulate=False)
        else:
            carry = _lru_scan(a_s, b_s, h_s, 0, ct, zero, rev=True, accumulate=True)
            _lru_scan(a_s, b_s, h_s, ct, tt - ct, carry, rev=True, accumulate=True)
    for r0, _, _ in blocks:
        z = u_ref[0, r0:r0 + ROW_BLOCK, GROUP_W:2 * GROUP_W]
        o_ref[0, r0:r0 + ROW_BLOCK, :] = (h_s[r0:r0 + ROW_BLOCK, :] * _silu(z)).astype(BF16)


def _block_diag(w):
    n, c, _ = w.shape
    eye = jnp.eye(n, dtype=w.dtype)
    return (eye[:, None, :, None] * w[:, :, None, :]).reshape(n * c, n * c)


def _lru(u, conv_w, conv_b, w_r, b_r, w_i, b_i, lam, *, lc):
    nb, total, _ = u.shape
    wg = jnp.stack([jnp.concatenate([_block_diag(w_r[d]), _block_diag(w_i[d])], axis=1)
                    for d in range(2)]).astype(BF16)
    bg = jnp.concatenate([b_r, b_i], axis=-1).reshape(2, 1, 2 * GROUP_W)
    full = lambda shape: pl.BlockSpec(shape, lambda b: (0,) * len(shape))
    return pl.pallas_call(
        functools.partial(_lru_body, lc=lc),
        grid=(nb,),
        in_specs=[pl.BlockSpec((1, total, W_LRU), lambda b: (b, 0, 0)),
                  full((CONV_W, GROUP_W)), full((1, GROUP_W)),
                  full((2, GROUP_W, 2 * GROUP_W)), full((2, 1, 2 * GROUP_W)),
                  full((2, 1, GROUP_W))],
        out_specs=pl.BlockSpec((1, total, GROUP_W), lambda b: (b, 0, 0)),
        out_shape=jax.ShapeDtypeStruct((nb, total, GROUP_W), BF16),
        scratch_shapes=[pltpu.VMEM((total, GROUP_W), F32) for _ in range(4)],
        compiler_params=_cparams(("arbitrary",)),
        name="lru",
    )(u, conv_w, conv_b.reshape(1, GROUP_W), wg, bg, lam.reshape(2, 1, GROUP_W))


def _rope(x, cos, sin_signed):
    half = HEAD_DIM // 2
    outs = []
    for c0 in range(0, x.shape[1], 128):
        xs = x[:, c0:c0 + 128]
        lane = lax.broadcasted_iota(jnp.int32, xs.shape, 1)
        swapped = jnp.where((lane % HEAD_DIM) < half,
                            pltpu.roll(xs, 128 - half, 1), pltpu.roll(xs, half, 1))
        outs.append(xs * cos[:, c0:c0 + 128] + swapped * sin_signed[:, c0:c0 + 128])
    return outs[0] if len(outs) == 1 else jnp.concatenate(outs, axis=1)


def _ret_body(u_ref, cos_ref, sin_ref, lg_ref, o_ref, qr_s, kr_s, ds_s, st_s, m_s, *, lc):
    total = u_ref.shape[1]
    c = RET_CHUNK
    nch = total // c
    n_ctx = lc // c
    gw = GROUP_W
    ones_bd = _head_ones()
    bd_mask = ones_bd > 0
    lg = -_softplus(-lg_ref[...])
    lgf, lgb = lg[0:1, :], lg[1:2, :]
    pos = lax.broadcasted_iota(jnp.int32, (c, 1), 0).astype(F32)
    qdec_f = jnp.exp((pos + 1.0) * lgf)
    qdec_b = jnp.exp((c - pos) * lgb)
    kdec_f = jnp.exp((c - 1.0 - pos) * lgf)
    kdec_b = jnp.exp(pos * lgb)
    cdec_f = jnp.exp(float(c) * lgf)
    cdec_b = jnp.exp(float(c) * lgb)
    dij = (lax.broadcasted_iota(jnp.int32, (c, c), 0)
           - lax.broadcasted_iota(jnp.int32, (c, c), 1)).astype(F32)
    for h in range(N_HEADS):
        lf = lgf[:, h * HEAD_DIM:h * HEAD_DIM + 1]
        lb = lgb[:, h * HEAD_DIM:h * HEAD_DIM + 1]
        fwd = jnp.exp(jnp.maximum(dij, 0.0) * lf)
        bwd = jnp.exp(jnp.maximum(-dij, 0.0) * lb)
        m_s[h] = jnp.where(dij > 0, fwd, jnp.where(dij < 0, bwd, 2.0))

    for ci in range(nch):
        r0 = ci * c
        q = u_ref[0, r0:r0 + c, 0:gw]
        k = u_ref[0, r0:r0 + c, gw:2 * gw]
        v = u_ref[0, r0:r0 + c, 2 * gw:3 * gw]
        cos = cos_ref[r0:r0 + c, :]
        sin = sin_ref[r0:r0 + c, :]
        qr = _rope(q, cos, sin)
        kr = _rope(k, cos, sin) * (HEAD_DIM ** -0.5)
        qr_s[r0:r0 + c, :] = qr
        kr_s[r0:r0 + c, :] = kr
        ds_s[0, ci] = jnp.where(bd_mask, _dot_tn(kr * kdec_f, v), 0.0)
        ds_s[1, ci] = jnp.where(bd_mask, _dot_tn(kr * kdec_b, v), 0.0)

    s = jnp.zeros((gw, gw), F32)
    for ci in range(nch):
        st_s[0, ci] = s
        s = s * cdec_f + ds_s[0, ci]
    s = jnp.zeros((gw, gw), F32)
    for ci in list(range(n_ctx - 1, -1, -1)) + list(range(nch - 1, n_ctx - 1, -1)):
        st_s[1, ci] = s
        s = s * cdec_b + ds_s[1, ci]

    for ci in range(nch):
        r0 = ci * c
        qr = qr_s[r0:r0 + c, :]
        kr = kr_s[r0:r0 + c, :]
        v = u_ref[0, r0:r0 + c, 2 * gw:3 * gw]
        z = u_ref[0, r0:r0 + c, 3 * gw:4 * gw]
        outs = []
        for h in range(N_HEADS):
            hs = slice(h * HEAD_DIM, (h + 1) * HEAD_DIM)
            att = _dot_nt(qr[:, hs], kr[:, hs]) * m_s[h]
            outs.append(_dot(att, v[:, hs]))
        o = jnp.concatenate(outs, axis=1)
        o = o + _dot(qr * qdec_f, st_s[0, ci]) + _dot(qr * qdec_b, st_s[1, ci])
        mu = _split_dot(o, ones_bd, 2) * (1.0 / HEAD_DIM)
        dlt = o - mu
        var = _split_dot(dlt * dlt, ones_bd, 2) * (1.0 / HEAD_DIM)
        y = dlt * lax.rsqrt(var + NORM_EPS) * _silu(z)
        o_ref[0, r0:r0 + c, :] = y.astype(BF16)


def _ret(u, cos, sin, decay_logit, *, lc):
    nb, total, _ = u.shape
    nch = total // RET_CHUNK
    lg = jnp.repeat(decay_logit, HEAD_DIM, axis=-1)
    full = lambda shape: pl.BlockSpec(shape, lambda b: (0,) * len(shape))
    return pl.pallas_call(
        functools.partial(_ret_body, lc=lc),
        grid=(nb,),
        in_specs=[pl.BlockSpec((1, total, W_RET), lambda b: (b, 0, 0)),
                  full((total, GROUP_W)), full((total, GROUP_W)), full((2, GROUP_W))],
        out_specs=pl.BlockSpec((1, total, GROUP_W), lambda b: (b, 0, 0)),
        out_shape=jax.ShapeDtypeStruct((nb, total, GROUP_W), BF16),
        scratch_shapes=[pltpu.VMEM((total, GROUP_W), F32),
                        pltpu.VMEM((total, GROUP_W), F32),
                        pltpu.VMEM((2, nch, GROUP_W, GROUP_W), F32),
                        pltpu.VMEM((2, nch, GROUP_W, GROUP_W), F32),
                        pltpu.VMEM((N_HEADS, RET_CHUNK, RET_CHUNK), F32)],
        compiler_params=_cparams(("arbitrary",)),
        name="retention",
    )(u, cos, sin, lg)


def _swa_body(sink_ref, u_ref, cos_ref, sin_ref, o_ref, q_s, k_s, v_s, *, lc):
    total = u_ref.shape[1]
    t_lat = total - lc
    gw = GROUP_W
    blk = WINDOW
    nblk = t_lat // blk
    kv0 = lc + blk
    scale = HEAD_DIM ** -0.5
    zpad = jnp.zeros((blk, SWA_KV_W), F32)
    for s_ref in (k_s, v_s):
        s_ref[lc:lc + blk, :] = zpad
        s_ref[kv0 + t_lat:kv0 + t_lat + blk, :] = zpad
    for r0, _, _ in _row_blocks(lc, total):
        cos = cos_ref[r0:r0 + ROW_BLOCK, :]
        sin = sin_ref[r0:r0 + ROW_BLOCK, :]
        q_s[r0:r0 + ROW_BLOCK, :] = _rope(u_ref[0, r0:r0 + ROW_BLOCK, 0:gw], cos, sin) * scale
        kr = _rope(u_ref[0, r0:r0 + ROW_BLOCK, gw:gw + SWA_KV_W], cos[:, 0:SWA_KV_W], sin[:, 0:SWA_KV_W])
        dst = r0 if r0 < lc else r0 + blk
        k_s[dst:dst + ROW_BLOCK, :] = kr
        v_s[dst:dst + ROW_BLOCK, :] = u_ref[0, r0:r0 + ROW_BLOCK, gw + SWA_KV_W:gw + 2 * SWA_KV_W]

    grp = N_HEADS // SWA_KV_HEADS

    def attend(q2, keys, vals, masks, sink_col):
        scores = []
        mx = sink_col
        for kk, mk in zip(keys, masks):
            s = _dot_nt(q2, kk)
            if mk is not None:
                s = jnp.where(mk, s, NEG_INF)
            scores.append(s)
            mx = jnp.maximum(mx, jnp.max(s, axis=-1, keepdims=True))
        den = jnp.exp(sink_col - mx)
        acc = None
        for s, vv in zip(scores, vals):
            e = jnp.exp(s - mx)
            den = den + jnp.sum(e, axis=-1, keepdims=True)
            t = _dot(e, vv)
            acc = t if acc is None else acc + t
        return acc / den

    def sink_column(hk, rows_per_head):
        row = lax.broadcasted_iota(jnp.int32, (grp * rows_per_head, 1), 0)
        col = jnp.full((grp * rows_per_head, 1), sink_ref[hk * grp], F32)
        for g in range(1, grp):
            col = jnp.where(row >= g * rows_per_head, sink_ref[hk * grp + g], col)
        return col

    if True:
        outs = [None] * N_HEADS
        for hk in range(SWA_KV_HEADS):
            ks = slice(hk * HEAD_DIM, (hk + 1) * HEAD_DIM)
            q2 = jnp.concatenate(
                [q_s[0:lc, (hk * grp + g) * HEAD_DIM:(hk * grp + g + 1) * HEAD_DIM] for g in range(grp)],
                axis=0)
            o2 = attend(q2, [k_s[0:lc, ks]], [v_s[0:lc, ks]], [None], sink_column(hk, lc))
            for g in range(grp):
                outs[hk * grp + g] = o2[g * lc:(g + 1) * lc, :]
        o = jnp.concatenate(outs, axis=1)
        z = u_ref[0, 0:lc, gw + 2 * SWA_KV_W:2 * gw + 2 * SWA_KV_W]
        o_ref[0, 0:lc, :] = (o * _silu(z)).astype(BF16)

    qi = lax.broadcasted_iota(jnp.int32, (grp * blk, 3 * blk), 0) % blk
    kj = lax.broadcasted_iota(jnp.int32, (grp * blk, 3 * blk), 1)
    in_win = jnp.abs(kj - blk - qi) <= WINDOW

    def block_body(n, carry):
        kpos = n * blk - blk + kj
        mask = in_win & (kpos >= 0) & (kpos < t_lat)
        qrow = pl.multiple_of(lc + n * blk, blk)
        wrow = pl.multiple_of(lc + n * blk, blk)
        outs = [None] * N_HEADS
        for hk in range(SWA_KV_HEADS):
            ks = slice(hk * HEAD_DIM, (hk + 1) * HEAD_DIM)
            q2 = jnp.concatenate(
                [q_s[pl.ds(qrow, blk), (hk * grp + g) * HEAD_DIM:(hk * grp + g + 1) * HEAD_DIM]
                 for g in range(grp)], axis=0)
            o2 = attend(q2,
                        [k_s[pl.ds(wrow, 3 * blk), ks], k_s[0:lc, ks]],
                        [v_s[pl.ds(wrow, 3 * blk), ks], v_s[0:lc, ks]],
                        [mask, None], sink_column(hk, blk))
            for g in range(grp):
                outs[hk * grp + g] = o2[g * blk:(g + 1) * blk, :]
        o = jnp.concatenate(outs, axis=1)
        z = u_ref[0, pl.ds(qrow, blk), gw + 2 * SWA_KV_W:2 * gw + 2 * SWA_KV_W]
        o_ref[0, pl.ds(qrow, blk), :] = (o * _silu(z)).astype(BF16)
        return carry

    lax.fori_loop(0, nblk, block_body, 0)


def _swa(u, cos, sin, sink, *, lc):
    nb, total, _ = u.shape
    t_lat = total - lc
    full = lambda shape: pl.BlockSpec(shape, lambda b, s: (0,) * len(shape))
    grid_spec = pltpu.PrefetchScalarGridSpec(
        num_scalar_prefetch=1,
        grid=(nb,),
        in_specs=[pl.BlockSpec((1, total, W_SWA), lambda b, s: (b, 0, 0)),
                  full((total, GROUP_W)), full((total, GROUP_W))],
        out_specs=pl.BlockSpec((1, total, GROUP_W), lambda b, s: (b, 0, 0)),
        scratch_shapes=[pltpu.VMEM((total, GROUP_W), F32),
                        pltpu.VMEM((total + 2 * WINDOW, SWA_KV_W), F32),
                        pltpu.VMEM((total + 2 * WINDOW, SWA_KV_W), F32)],
    )
    return pl.pallas_call(
        functools.partial(_swa_body, lc=lc),
        grid_spec=grid_spec,
        out_shape=jax.ShapeDtypeStruct((nb, total, GROUP_W), BF16),
        compiler_params=_cparams(("arbitrary",)),
        name="swa",
    )(sink, u, cos, sin)


def _gdn_body(u_ref, ab_ref, cw_ref, avec_ref, dtb_ref, ng_ref, o_ref,
              q_s, k_s, v_s, cs_s, gt_s, uv_s, wq_s, att_s, ktt_s, of_s, ob_s, st_s, *, lc):
    total = u_ref.shape[1]
    gw = GROUP_W
    c = GDN_CHUNK
    nch = total // c
    n_ctx = lc // c
    per_blk = ROW_BLOCK // c
    ones_bd = _head_ones()

    ri = lax.broadcasted_iota(jnp.int32, (ROW_BLOCK, ROW_BLOCK), 0)
    ci_ = lax.broadcasted_iota(jnp.int32, (ROW_BLOCK, ROW_BLOCK), 1)
    same_chunk = (ri // c) == (ci_ // c)
    tri_lo = (same_chunk & (ri >= ci_)).astype(BF16)
    tri_up = (same_chunk & (ri <= ci_)).astype(BF16)
    lane = lax.broadcasted_iota(jnp.int32, (ROW_BLOCK, W_AB), 1)

    for bi, (r0, lo, hi) in enumerate(_row_blocks(lc, total)):
        h = _silu(_dwconv_rows(u_ref, r0, ROW_BLOCK, lo, hi, 0, 3 * gw, cw_ref))
        hq, hk, hv = h[:, 0:gw], h[:, gw:2 * gw], h[:, 2 * gw:3 * gw]
        ssq = _split_dot(hq * hq, ones_bd, 2)
        q_s[r0:r0 + ROW_BLOCK, :] = hq * lax.rsqrt(ssq + NORM_EPS) * (HEAD_DIM ** -0.5)
        ssk = _split_dot(hk * hk, ones_bd, 2)
        k_s[r0:r0 + ROW_BLOCK, :] = hk * lax.rsqrt(ssk + NORM_EPS)
        v_s[r0:r0 + ROW_BLOCK, :] = hv
        ab = ab_ref[0, r0:r0 + ROW_BLOCK, :]
        g = -avec_ref[...] * _softplus(ab + dtb_ref[...])
        g = jnp.where(lane < 2 * N_HEADS, g, 0.0)
        beta = _sigmoid(ab)
        cs_lo = _split_dot_left(tri_lo, g, 3)
        cs_up = _split_dot_left(tri_up, g, 3)
        cs = jnp.where(lane < N_HEADS, cs_lo, jnp.where(lane < 2 * N_HEADS, cs_up, beta))
        cs_s[r0:r0 + ROW_BLOCK, :] = cs
        cst = cs.T
        for cc in range(per_blk):
            row0 = (bi * per_blk + cc) * SUBLANES
            gt_s[row0:row0 + SUBLANES, :] = cst[0:SUBLANES, cc * c:(cc + 1) * c]

    ii = lax.broadcasted_iota(jnp.int32, (c, c), 0)
    jj = lax.broadcasted_iota(jnp.int32, (c, c), 1)
    eye = (ii == jj).astype(F32)
    n_sq = c.bit_length() - 2
    incl = ((ii >= jj), (ii <= jj))
    strict = ((ii > jj), (ii < jj))
    last = (c - 1, 0)
    heads = [slice(h * HEAD_DIM, (h + 1) * HEAD_DIM) for h in range(N_HEADS)]

    def b1_body(i, carry):
        probs = []
        for cc in range(GDN_B1_CHUNKS):
            ch = i * GDN_B1_CHUNKS + cc
            r = pl.multiple_of(ch * c, c)
            kc = k_s[pl.ds(r, c), :]
            qc = q_s[pl.ds(r, c), :]
            vc = v_s[pl.ds(r, c), :]
            csc = cs_s[pl.ds(r, c), :]
            gt = gt_s[pl.ds(pl.multiple_of(ch * SUBLANES, SUBLANES), SUBLANES), :]
            for h in range(N_HEADS):
                kh, qh, vh = kc[:, heads[h]], qc[:, heads[h]], vc[:, heads[h]]
                kk = _dot_nt(kh, kh)
                qk = _dot_nt(qh, kh)
                for d in range(2):
                    hd = d * N_HEADS + h
                    col = csc[:, hd:hd + 1]
                    rowv = gt[hd:hd + 1, :]
                    bcol = csc[:, 2 * N_HEADS + hd:2 * N_HEADS + hd + 1]
                    decay = jnp.where(incl[d], jnp.exp(jnp.where(incl[d], col - rowv, 0.0)), 0.0)
                    n = -jnp.where(strict[d], kk * decay, 0.0) * bcol
                    eg = jnp.exp(col)
                    glast = csc[last[d]:last[d] + 1, hd:hd + 1]
                    probs.append(dict(
                        cc=cc, d=d, h=h, r=r, ch=ch,
                        a=(eye - n).astype(BF16), p=eye + n,
                        rhs=jnp.concatenate([vh * bcol, kh * (bcol * eg)], axis=1).astype(BF16),
                        att=qk * decay, qe=qh * eg, kt=kh * jnp.exp(glast - col)))
        for _ in range(n_sq):
            res = [eye - jnp.dot(pr['a'], pr['p'].astype(BF16), preferred_element_type=F32)
                   for pr in probs]
            for pr, rr in zip(probs, res):
                pr['p'] = pr['p'] + _dot(pr['p'], rr)
        for pr in probs:
            pr['uw'] = _dot(pr['p'], pr['rhs'])
        for cc in range(GDN_B1_CHUNKS):
            for d in range(2):
                sel = [pr for pr in probs if pr['cc'] == cc and pr['d'] == d]
                r, ch = sel[0]['r'], sel[0]['ch']
                cat = lambda f: jnp.concatenate([f(pr) for pr in sel], axis=1)
                uv_s[d, pl.ds(r, c), :] = cat(lambda pr: pr['uw'][:, 0:HEAD_DIM])
                wq = jnp.concatenate([cat(lambda pr: pr['uw'][:, HEAD_DIM:2 * HEAD_DIM]),
                                      cat(lambda pr: pr['qe'])], axis=0)
                wq_s[d, pl.ds(pl.multiple_of(ch * 2 * c, 2 * c), 2 * c), :] = wq.astype(BF16)
                att_s[d, pl.ds(r, c), :] = cat(lambda pr: pr['att']).astype(BF16)
                ktt = cat(lambda pr: pr['kt']).T
                ktt_s[d, pl.ds(pl.multiple_of(ch * gw, gw), gw), :] = ktt.astype(BF16)
        return carry

    lax.fori_loop(0, nch // GDN_B1_CHUNKS, b1_body, 0)

    def b2_body(i, carry):
        chs = (i, jnp.where(i < n_ctx, n_ctx - 1 - i, nch - 1 + n_ctx - i))
        probs = []
        for d in range(2):
            ch = chs[d]
            r = pl.multiple_of(ch * c, c)
            wqm = wq_s[d, pl.ds(pl.multiple_of(ch * 2 * c, 2 * c), 2 * c), :]
            uvm = uv_s[d, pl.ds(r, c), :]
            attm = att_s[d, pl.ds(r, c), :]
            grow = cs_s[pl.ds(r + last[d], 1), :]
            for h in range(N_HEADS):
                hd = d * N_HEADS + h
                kt = ktt_s[d, pl.ds(pl.multiple_of(ch * gw + h * HEAD_DIM, HEAD_DIM), HEAD_DIM), :]
                probs.append(dict(d=d, h=h, r=r, s=st_s[hd], wqm=wqm[:, heads[h]], uv=uvm[:, heads[h]],
                                  att=attm[:, heads[h]], kt=kt, gl=jnp.exp(grow[:, hd:hd + 1])))
        for pr in probs:
            pr['wq'] = jnp.dot(pr['wqm'], pr['s'].astype(BF16), preferred_element_type=F32)
        for pr in probs:
            pr['vn'] = (pr['uv'] - pr['wq'][0:c, :]).astype(BF16)
        for pr in probs:
            pr['o'] = pr['wq'][c:2 * c, :] + jnp.dot(pr['att'], pr['vn'], preferred_element_type=F32)
            st_s[pr['d'] * N_HEADS + pr['h']] = (
                pr['s'] * pr['gl'] + jnp.dot(pr['kt'], pr['vn'], preferred_element_type=F32))
        for d, o_s in ((0, of_s), (1, ob_s)):
            sel = [pr for pr in probs if pr['d'] == d]
            o_s[pl.ds(sel[0]['r'], c), :] = jnp.concatenate([pr['o'] for pr in sel], axis=1)
        return carry

    st_s[...] = jnp.zeros(st_s.shape, F32)
    lax.fori_loop(0, nch, b2_body, 0)

    for r0, _, _ in _row_blocks(lc, total):
        o = of_s[r0:r0 + ROW_BLOCK, :] + ob_s[r0:r0 + ROW_BLOCK, :]
        ms = _split_dot(o * o, ones_bd, 2) * (1.0 / HEAD_DIM)
        z = u_ref[0, r0:r0 + ROW_BLOCK, 3 * gw:4 * gw]
        y = o * lax.rsqrt(ms + NORM_EPS) * ng_ref[...] * _silu(z)
        o_ref[0, r0:r0 + ROW_BLOCK, :] = y.astype(BF16)


def _gdn(u, ab, conv_w, a_log, dt_bias, norm_g, *, lc):
    nb, total, _ = u.shape
    nch = total // GDN_CHUNK
    pad = W_AB - 2 * N_HEADS
    avec = jnp.pad(jnp.exp(a_log.reshape(1, 2 * N_HEADS)), ((0, 0), (0, pad)))
    dtb = jnp.pad(dt_bias.reshape(1, 2 * N_HEADS), ((0, 0), (0, pad)))
    ng = jnp.tile(norm_g.reshape(1, HEAD_DIM), (1, N_HEADS))
    full = lambda shape: pl.BlockSpec(shape, lambda b: (0,) * len(shape))
    return pl.pallas_call(
        functools.partial(_gdn_body, lc=lc),
        grid=(nb,),
        in_specs=[pl.BlockSpec((1, total, W_GDN), lambda b: (b, 0, 0), pipeline_mode=pl.Buffered(1)),
                  pl.BlockSpec((1, total, W_AB), lambda b: (b, 0, 0), pipeline_mode=pl.Buffered(1)),
                  full((CONV_W, 3 * GROUP_W)), full((1, W_AB)), full((1, W_AB)),
                  full((1, GROUP_W))],
        out_specs=pl.BlockSpec((1, total, GROUP_W), lambda b: (b, 0, 0)),
        out_shape=jax.ShapeDtypeStruct((nb, total, GROUP_W), BF16),
        scratch_shapes=[pltpu.VMEM((total, GROUP_W), F32),
                        pltpu.VMEM((total, GROUP_W), F32),
                        pltpu.VMEM((total, GROUP_W), F32),
                        pltpu.VMEM((total, W_AB), F32),
                        pltpu.VMEM((nch * SUBLANES, GDN_CHUNK), F32),
                        pltpu.VMEM((2, total, GROUP_W), F32),
                        pltpu.VMEM((2, 2 * total, GROUP_W), BF16),
                        pltpu.VMEM((2, total, GROUP_W), BF16),
                        pltpu.VMEM((2, nch * GROUP_W, GDN_CHUNK), BF16),
                        pltpu.VMEM((total, GROUP_W), F32),
                        pltpu.VMEM((total, GROUP_W), F32),
                        pltpu.VMEM((2 * N_HEADS, HEAD_DIM, HEAD_DIM), F32)],
        compiler_params=_cparams(("arbitrary",)),
        name="gdn",
    )(u, ab, conv_w, avec, dtb, ng)


def _pack_w_in(w_in):
    gw = GROUP_W
    o_gdn = 2 * gw
    o_ab = o_gdn + 4 * gw
    o_ret = o_ab + 4 * N_HEADS
    o_swa = o_ret + 4 * gw
    end = o_swa + W_SWA
    assert end == w_in.shape[-1]
    pad = jnp.zeros(w_in.shape[:-1] + (W_AB - 4 * N_HEADS,), w_in.dtype)
    return jnp.concatenate([w_in[..., 0:o_gdn], w_in[..., o_gdn:o_ab], w_in[..., o_ret:o_swa],
                            w_in[..., o_swa:end], w_in[..., o_ab:o_ret], pad], axis=-1).astype(BF16)


def _rope_tables(ang, lc):
    cos = jnp.cos(ang)
    sin = jnp.sin(ang)
    cos_h = jnp.concatenate([cos, cos], axis=-1)
    sin_h = jnp.concatenate([-sin, sin], axis=-1)
    cos_t = jnp.tile(cos_h, (1, N_HEADS))
    sin_t = jnp.tile(sin_h, (1, N_HEADS))
    cos_t = jnp.concatenate([jnp.ones((lc, GROUP_W), F32), cos_t], axis=0)
    sin_t = jnp.concatenate([jnp.zeros((lc, GROUP_W), F32), sin_t], axis=0)
    return cos_t, sin_t


def _rope_freqs(pos, n):
    inv = ROPE_BASE ** (-jnp.arange(0, n, 2, dtype=F32) / n)
    return pos[:, None] * inv[None, :]


def kernel(x, c, ctx, c_ctx, w_mod, b_mod, pre_norm_g, post_norm_g, w_in, w_out, lru_conv_w, lru_conv_b, lru_w_r, lru_b_r, lru_w_i, lru_b_i, lru_lambda, gdn_conv_w, gdn_a_log, gdn_dt_bias, gdn_norm_g, ret_decay_logit, swa_sink):
    nb, t, d = x.shape
    lc = ctx.shape[1]
    depth = w_mod.shape[0]
    assert t % ROW_BLOCK == 0 and lc % ROW_BLOCK == 0 and d == 4 * GROUP_W

    rows = t // GRID_W
    row = jnp.repeat(jnp.arange(rows, dtype=F32), GRID_W)
    col = jnp.tile(jnp.arange(GRID_W, dtype=F32), rows)
    ang2d = jnp.concatenate([_rope_freqs(row, HEAD_DIM // 2), _rope_freqs(col, HEAD_DIM // 2)], axis=-1)
    ang1d = _rope_freqs(jnp.arange(t, dtype=F32), HEAD_DIM)
    cos1, sin1 = _rope_tables(ang1d, lc)
    cos2, sin2 = _rope_tables(ang2d, lc)

    mod_rows = -(-(nb + 1) // SUBLANES) * SUBLANES
    s_rows = jnp.concatenate([c, c_ctx[None, :], jnp.zeros((mod_rows - nb - 1, d), F32)], axis=0)
    mod = _modulation(s_rows, w_mod, b_mod)

    w_in_p = _pack_w_in(w_in)
    w_out_b = w_out.astype(BF16)
    xs = jnp.concatenate([ctx, x], axis=1)
    for l in range(depth):
        u_lru, u_gdn, u_ret, u_swa, u_ab = _in_proj(xs, mod[l], pre_norm_g[l], w_in_p[l], lc=lc)
        ya = _lru(u_lru, lru_conv_w[l], lru_conv_b[l], lru_w_r[l], lru_b_r[l], lru_w_i[l], lru_b_i[l],
                  lru_lambda[l], lc=lc)
        yb = _gdn(u_gdn, u_ab, gdn_conv_w[l], gdn_a_log[l], gdn_dt_bias[l], gdn_norm_g[l], lc=lc)
        yc = _ret(u_ret, cos1, sin1, ret_decay_logit[l], lc=lc)
        yd = _swa(u_swa, cos2, sin2, swa_sink[l], lc=lc)
        xs = _out_proj((ya, yb, yc, yd), xs, mod[l], post_norm_g[l], w_out_b[l], lc=lc)
    return xs[:, lc:, :]
```

```python
import functools

import jax
import jax.numpy as jnp
from jax import lax
from jax.experimental import pallas as pl
from jax.experimental.pallas import tpu as pltpu

F32 = jnp.float32
BF16 = jnp.bfloat16

HEAD_DIM = 64
GROUP_W = 256
N_HEADS = GROUP_W // HEAD_DIM
NORM_EPS = 1e-6
ROPE_BASE = 10000.0
NEG_INF = -1e30
GRID_W = 64
CONV_W = 4
LRU_C = 8.0
LRU_SCAN_UNROLL = 4
GDN_CHUNK = 64
GDN_B1_CHUNKS = 4
RET_CHUNK = 256
SWA_KV_HEADS = 2
SWA_KV_W = SWA_KV_HEADS * HEAD_DIM
WINDOW = 128
ROW_BLOCK = 256
PROJ_ROWS_MAX = 576
SUBLANES = 8
VMEM_LIMIT_BYTES = 56 * 1024 * 1024

W_LRU = 2 * GROUP_W
W_GDN = 4 * GROUP_W
W_RET = 4 * GROUP_W
W_SWA = 2 * GROUP_W + 2 * SWA_KV_W
W_AB = 128
IN_W_PACKED = W_LRU + W_GDN + W_RET + W_SWA + W_AB


def _cparams(sem):
    return pltpu.CompilerParams(dimension_semantics=sem, vmem_limit_bytes=VMEM_LIMIT_BYTES)


def _dot(a, b):
    return jnp.dot(a.astype(BF16), b.astype(BF16), preferred_element_type=F32)


def _dot_nt(a, b):
    return lax.dot_general(a.astype(BF16), b.astype(BF16), (((1,), (1,)), ((), ())),
                           preferred_element_type=F32)


def _dot_tn(a, b):
    return lax.dot_general(a.astype(BF16), b.astype(BF16), (((0,), (0,)), ((), ())),
                           preferred_element_type=F32)


def _split_dot(x, w, parts):
    acc = None
    r = x
    for _ in range(parts):
        hi = r.astype(BF16)
        t = jnp.dot(hi, w, preferred_element_type=F32)
        acc = t if acc is None else acc + t
        r = r - hi.astype(F32)
    return acc


def _split_dot_left(w, x, parts):
    acc = None
    r = x
    for _ in range(parts):
        hi = r.astype(BF16)
        t = jnp.dot(w, hi, preferred_element_type=F32)
        acc = t if acc is None else acc + t
        r = r - hi.astype(F32)
    return acc


def _sigmoid(x):
    return 1.0 / (1.0 + jnp.exp(-x))


def _silu(x):
    return x * _sigmoid(x)


def _softplus(x):
    return jnp.maximum(x, 0.0) + jnp.log1p(jnp.exp(-jnp.abs(x)))


def _head_ones():
    r = lax.broadcasted_iota(jnp.int32, (GROUP_W, GROUP_W), 0) // HEAD_DIM
    c = lax.broadcasted_iota(jnp.int32, (GROUP_W, GROUP_W), 1) // HEAD_DIM
    return (r == c).astype(BF16)


def _dwconv_rows(x_ref, r0, n, seg_lo, seg_hi, c0, c1, w_ref):
    width = c1 - c0
    zeros = jnp.zeros((SUBLANES, width), F32)
    prev = x_ref[0, r0 - SUBLANES:r0, c0:c1] if r0 > seg_lo else zeros
    nxt = x_ref[0, r0 + n:r0 + n + SUBLANES, c0:c1] if r0 + n < seg_hi else zeros
    win = jnp.concatenate([prev, x_ref[0, r0:r0 + n, c0:c1], nxt], axis=0)
    total = n + 2 * SUBLANES
    acc = None
    for k in range(CONV_W):
        shift = (2 - k) % total
        tap = win if shift == 0 else pltpu.roll(win, shift, 0)
        term = tap[SUBLANES:SUBLANES + n] * w_ref[k:k + 1, :]
        acc = term if acc is None else acc + term
    return acc


def _row_blocks(lc, total):
    out = []
    for r0 in range(0, total, ROW_BLOCK):
        out.append((r0, 0, lc) if r0 < lc else (r0, lc, total))
    return out


def _mod_body(s_ref, w_ref, b_ref, o_ref):
    s = _silu(s_ref[...])
    o_ref[0] = _dot(s, w_ref[0]) + b_ref[0]


def _modulation(s_rows, w_mod, b_mod):
    depth, d, d3 = w_mod.shape
    rows = s_rows.shape[0]
    nt = d3 // d
    return pl.pallas_call(
        _mod_body,
        grid=(depth, nt),
        in_specs=[pl.BlockSpec((rows, d), lambda l, j: (0, 0)),
                  pl.BlockSpec((1, d, d), lambda l, j: (l, 0, j)),
                  pl.BlockSpec((1, 1, d), lambda l, j: (l, 0, j))],
        out_specs=pl.BlockSpec((1, rows, d), lambda l, j: (l, 0, j)),
        out_shape=jax.ShapeDtypeStruct((depth, rows, d3), F32),
        compiler_params=_cparams(("arbitrary", "arbitrary")),
        name="modulation",
    )(s_rows, w_mod, b_mod.reshape(depth, 1, d3))


def _proj_rows(total):
    for tm in range(PROJ_ROWS_MAX, SUBLANES - 1, -SUBLANES):
        if total % tm == 0:
            return tm
    raise ValueError(total)


def _mod_rows(mod_ref, b, i, tm, lc, nb, c0, c1):
    lat = mod_ref[pl.ds(b, 1), c0:c1]
    if lc % tm == 0:
        ctx = mod_ref[nb:nb + 1, c0:c1]
        return jnp.where(i * tm < lc, ctx, lat)
    ctx = mod_ref[nb:nb + 1, c0:c1]
    row = i * tm + lax.broadcasted_iota(jnp.int32, (tm, 1), 0)
    return jnp.where(row < lc, ctx, lat)


def _inproj_body(x_ref, mod_ref, g_ref, w_ref, o_lru, o_gdn, o_ret, o_swa, o_ab, *, lc, nb):
    b = pl.program_id(0)
    i = pl.program_id(1)
    d = x_ref.shape[2]
    x = x_ref[0]
    ms = jnp.mean(x * x, axis=-1, keepdims=True)
    y = x * lax.rsqrt(ms + NORM_EPS) * g_ref[...]
    m = _mod_rows(mod_ref, b, i, x.shape[0], lc, nb, 0, 2 * d)
    h = (y * (1.0 + m[:, d:2 * d]) + m[:, 0:d]).astype(BF16)
    off = 0
    for o_ref in (o_lru, o_gdn, o_ret, o_swa, o_ab):
        w = o_ref.shape[2]
        o_ref[0] = jnp.dot(h, w_ref[:, off:off + w], preferred_element_type=F32)
        off += w


def _in_proj(x, mod, g, w, *, lc):
    nb, total, d = x.shape
    tm = _proj_rows(total)
    nt = total // tm
    widths = (W_LRU, W_GDN, W_RET, W_SWA, W_AB)
    return pl.pallas_call(
        functools.partial(_inproj_body, lc=lc, nb=nb),
        grid=(nb, nt),
        in_specs=[pl.BlockSpec((1, tm, d), lambda b, i: (b, i, 0)),
                  pl.BlockSpec(mod.shape, lambda b, i: (0, 0)),
                  pl.BlockSpec((1, d), lambda b, i: (0, 0)),
                  pl.BlockSpec(w.shape, lambda b, i: (0, 0))],
        out_specs=[pl.BlockSpec((1, tm, wd), lambda b, i: (b, i, 0)) for wd in widths],
        out_shape=[jax.ShapeDtypeStruct((nb, total, wd), F32) for wd in widths],
        compiler_params=_cparams(("arbitrary", "arbitrary")),
        name="in_proj",
    )(x, mod, g.reshape(1, d), w)


def _outproj_body(ya, yb, yc, yd, x_ref, mod_ref, g_ref, w_ref, o_ref, *, lc, nb):
    b = pl.program_id(0)
    i = pl.program_id(1)
    d = x_ref.shape[2]
    acc = None
    for k, y_ref in enumerate((ya, yb, yc, yd)):
        t = jnp.dot(y_ref[0], w_ref[k * GROUP_W:(k + 1) * GROUP_W, :], preferred_element_type=F32)
        acc = t if acc is None else acc + t
    ms = jnp.mean(acc * acc, axis=-1, keepdims=True)
    yn = acc * lax.rsqrt(ms + NORM_EPS) * g_ref[...]
    gate = _mod_rows(mod_ref, b, i, acc.shape[0], lc, nb, 2 * d, 3 * d)
    o_ref[0] = x_ref[0] + gate * yn


def _out_proj(ys, x, mod, g, w, *, lc):
    nb, total, d = x.shape
    tm = _proj_rows(total)
    nt = total // tm
    yspec = pl.BlockSpec((1, tm, GROUP_W), lambda b, i: (b, i, 0))
    return pl.pallas_call(
        functools.partial(_outproj_body, lc=lc, nb=nb),
        grid=(nb, nt),
        in_specs=[yspec, yspec, yspec, yspec,
                  pl.BlockSpec((1, tm, d), lambda b, i: (b, i, 0)),
                  pl.BlockSpec(mod.shape, lambda b, i: (0, 0)),
                  pl.BlockSpec((1, d), lambda b, i: (0, 0)),
                  pl.BlockSpec(w.shape, lambda b, i: (0, 0))],
        out_specs=pl.BlockSpec((1, tm, d), lambda b, i: (b, i, 0)),
        out_shape=jax.ShapeDtypeStruct(x.shape, F32),
        compiler_params=_cparams(("arbitrary", "arbitrary")),
        name="out_proj",
    )(*ys, x, mod, g.reshape(1, d), w)


def _lru_scan(a_s, b_s, h_s, tile_lo, n_tiles, carry, *, rev, accumulate):
    row = lax.broadcasted_iota(jnp.int32, (SUBLANES, GROUP_W), 0)

    def body(j, carry):
        t = tile_lo + (n_tiles - 1 - j if rev else j)
        r = pl.multiple_of(t * SUBLANES, SUBLANES)
        a = a_s[pl.ds(r, SUBLANES), :]
        b = b_s[pl.ds(r, SUBLANES), :]
        for s in (1, 2, 4):
            if rev:
                ra = pltpu.roll(a, SUBLANES - s, 0)
                rb = pltpu.roll(b, SUBLANES - s, 0)
                m = row < SUBLANES - s
            else:
                ra = pltpu.roll(a, s, 0)
                rb = pltpu.roll(b, s, 0)
                m = row >= s
            b = a * jnp.where(m, rb, 0.0) + b
            a = a * jnp.where(m, ra, 1.0)
        h = a * carry + b
        if accumulate:
            h_s[pl.ds(r, SUBLANES), :] = h_s[pl.ds(r, SUBLANES), :] + h
        else:
            h_s[pl.ds(r, SUBLANES), :] = h
        return h[0:1, :] if rev else h[SUBLANES - 1:SUBLANES, :]

    return lax.fori_loop(0, n_tiles, body, carry, unroll=LRU_SCAN_UNROLL)


def _lru_body(u_ref, cw_ref, cb_ref, wg_ref, bg_ref, lam_ref, o_ref, uc_s, a_s, b_s, h_s, *, lc):
    total = u_ref.shape[1]
    blocks = _row_blocks(lc, total)
    for r0, lo, hi in blocks:
        uc_s[r0:r0 + ROW_BLOCK, :] = (
            _dwconv_rows(u_ref, r0, ROW_BLOCK, lo, hi, 0, GROUP_W, cw_ref) + cb_ref[...])
    zero = jnp.zeros((1, GROUP_W), F32)
    for d in range(2):
        sp = _softplus(-lam_ref[d])
        for r0, _, _ in blocks:
            uc = uc_s[r0:r0 + ROW_BLOCK, :]
            gts = _dot(uc, wg_ref[d]) + bg_ref[d]
            r = _sigmoid(gts[:, 0:GROUP_W])
            ig = _sigmoid(gts[:, GROUP_W:2 * GROUP_W])
            a = jnp.exp(-LRU_C * r * sp)
            a_s[r0:r0 + ROW_BLOCK, :] = a
            b_s[r0:r0 + ROW_BLOCK, :] = jnp.sqrt(1.0 - a * a) * (ig * uc)
        ct, tt = lc // SUBLANES, total // SUBLANES
        if d == 0:
            _lru_scan(a_s, b_s, h_s, 0, tt, zero, rev=False, accumulate=False)
        else:
            carry = _lru_scan(a_s, b_s, h_s, 0, ct, zero, rev=True, accumulate=True)
            _lru_scan(a_s, b_s, h_s, ct, tt - ct, carry, rev=True, accumulate=True)
    for r0, _, _ in blocks:
        z = u_ref[0, r0:r0 + ROW_BLOCK, GROUP_W:2 * GROUP_W]
        o_ref[0, r0:r0 + ROW_BLOCK, :] = (h_s[r0:r0 + ROW_BLOCK, :] * _silu(z)).astype(BF16)


def _block_diag(w):
    n, c, _ = w.shape
    eye = jnp.eye(n, dtype=w.dtype)
    return (eye[:, None, :, None] * w[:, :, None, :]).reshape(n * c, n * c)


def _lru(u, conv_w, conv_b, w_r, b_r, w_i, b_i, lam, *, lc):
    nb, total, _ = u.shape
    wg = jnp.stack([jnp.concatenate([_block_diag(w_r[d]), _block_diag(w_i[d])], axis=1)
                    for d in range(2)]).astype(BF16)
    bg = jnp.concatenate([b_r, b_i], axis=-1).reshape(2, 1, 2 * GROUP_W)
    full = lambda shape: pl.BlockSpec(shape, lambda b: (0,) * len(shape))
    return pl.pallas_call(
        functools.partial(_lru_body, lc=lc),
        grid=(nb,),
        in_specs=[pl.BlockSpec((1, total, W_LRU), lambda b: (b, 0, 0)),
                  full((CONV_W, GROUP_W)), full((1, GROUP_W)),
                  full((2, GROUP_W, 2 * GROUP_W)), full((2, 1, 2 * GROUP_W)),
                  full((2, 1, GROUP_W))],
        out_specs=pl.BlockSpec((1, total, GROUP_W), lambda b: (b, 0, 0)),
        out_shape=jax.ShapeDtypeStruct((nb, total, GROUP_W), BF16),
        scratch_shapes=[pltpu.VMEM((total, GROUP_W), F32) for _ in range(4)],
        compiler_params=_cparams(("arbitrary",)),
        name="lru",
    )(u, conv_w, conv_b.reshape(1, GROUP_W), wg, bg, lam.reshape(2, 1, GROUP_W))


def _rope(x, cos, sin_signed):
    half = HEAD_DIM // 2
    outs = []
    for c0 in range(0, x.shape[1], 128):
        xs = x[:, c0:c0 + 128]
        lane = lax.broadcasted_iota(jnp.int32, xs.shape, 1)
        swapped = jnp.where((lane % HEAD_DIM) < half,
                            pltpu.roll(xs, 128 - half, 1), pltpu.roll(xs, half, 1))
        outs.append(xs * cos[:, c0:c0 + 128] + swapped * sin_signed[:, c0:c0 + 128])
    return outs[0] if len(outs) == 1 else jnp.concatenate(outs, axis=1)


def _ret_body(u_ref, cos_ref, sin_ref, lg_ref, o_ref, qr_s, kr_s, ds_s, st_s, m_s, *, lc):
    total = u_ref.shape[1]
    c = RET_CHUNK
    nch = total // c
    n_ctx = lc // c
    gw = GROUP_W
    ones_bd = _head_ones()
    bd_mask = ones_bd > 0
    lg = -_softplus(-lg_ref[...])
    lgf, lgb = lg[0:1, :], lg[1:2, :]
    pos = lax.broadcasted_iota(jnp.int32, (c, 1), 0).astype(F32)
    qdec_f = jnp.exp((pos + 1.0) * lgf)
    qdec_b = jnp.exp((c - pos) * lgb)
    kdec_f = jnp.exp((c - 1.0 - pos) * lgf)
    kdec_b = jnp.exp(pos * lgb)
    cdec_f = jnp.exp(float(c) * lgf)
    cdec_b = jnp.exp(float(c) * lgb)
    dij = (lax.broadcasted_iota(jnp.int32, (c, c), 0)
           - lax.broadcasted_iota(jnp.int32, (c, c), 1)).astype(F32)
    for h in range(N_HEADS):
        lf = lgf[:, h * HEAD_DIM:h * HEAD_DIM + 1]
        lb = lgb[:, h * HEAD_DIM:h * HEAD_DIM + 1]
        fwd = jnp.exp(jnp.maximum(dij, 0.0) * lf)
        bwd = jnp.exp(jnp.maximum(-dij, 0.0) * lb)
        m_s[h] = jnp.where(dij > 0, fwd, jnp.where(dij < 0, bwd, 2.0))

    for ci in range(nch):
        r0 = ci * c
        q = u_ref[0, r0:r0 + c, 0:gw]
        k = u_ref[0, r0:r0 + c, gw:2 * gw]
        v = u_ref[0, r0:r0 + c, 2 * gw:3 * gw]
        cos = cos_ref[r0:r0 + c, :]
        sin = sin_ref[r0:r0 + c, :]
        qr = _rope(q, cos, sin)
        kr = _rope(k, cos, sin) * (HEAD_DIM ** -0.5)
        qr_s[r0:r0 + c, :] = qr
        kr_s[r0:r0 + c, :] = kr
        ds_s[0, ci] = jnp.where(bd_mask, _dot_tn(kr * kdec_f, v), 0.0)
        ds_s[1, ci] = jnp.where(bd_mask, _dot_tn(kr * kdec_b, v), 0.0)

    s = jnp.zeros((gw, gw), F32)
    for ci in range(nch):
        st_s[0, ci] = s
        s = s * cdec_f + ds_s[0, ci]
    s = jnp.zeros((gw, gw), F32)
    for ci in list(range(n_ctx - 1, -1, -1)) + list(range(nch - 1, n_ctx - 1, -1)):
        st_s[1, ci] = s
        s = s * cdec_b + ds_s[1, ci]

    for ci in range(nch):
        r0 = ci * c
        qr = qr_s[r0:r0 + c, :]
        kr = kr_s[r0:r0 + c, :]
        v = u_ref[0, r0:r0 + c, 2 * gw:3 * gw]
        z = u_ref[0, r0:r0 + c, 3 * gw:4 * gw]
        outs = []
        for h in range(N_HEADS):
            hs = slice(h * HEAD_DIM, (h + 1) * HEAD_DIM)
            att = _dot_nt(qr[:, hs], kr[:, hs]) * m_s[h]
            outs.append(_dot(att, v[:, hs]))
        o = jnp.concatenate(outs, axis=1)
        o = o + _dot(qr * qdec_f, st_s[0, ci]) + _dot(qr * qdec_b, st_s[1, ci])
        mu = _split_dot(o, ones_bd, 2) * (1.0 / HEAD_DIM)
        dlt = o - mu
        var = _split_dot(dlt * dlt, ones_bd, 2) * (1.0 / HEAD_DIM)
        y = dlt * lax.rsqrt(var + NORM_EPS) * _silu(z)
        o_ref[0, r0:r0 + c, :] = y.astype(BF16)


def _ret(u, cos, sin, decay_logit, *, lc):
    nb, total, _ = u.shape
    nch = total // RET_CHUNK
    lg = jnp.repeat(decay_logit, HEAD_DIM, axis=-1)
    full = lambda shape: pl.BlockSpec(shape, lambda b: (0,) * len(shape))
    return pl.pallas_call(
        functools.partial(_ret_body, lc=lc),
        grid=(nb,),
        in_specs=[pl.BlockSpec((1, total, W_RET), lambda b: (b, 0, 0)),
                  full((total, GROUP_W)), full((total, GROUP_W)), full((2, GROUP_W))],
        out_specs=pl.BlockSpec((1, total, GROUP_W), lambda b: (b, 0, 0)),
        out_shape=jax.ShapeDtypeStruct((nb, total, GROUP_W), BF16),
        scratch_shapes=[pltpu.VMEM((total, GROUP_W), F32),
                        pltpu.VMEM((total, GROUP_W), F32),
                        pltpu.VMEM((2, nch, GROUP_W, GROUP_W), F32),
                        pltpu.VMEM((2, nch, GROUP_W, GROUP_W), F32),
                        pltpu.VMEM((N_HEADS, RET_CHUNK, RET_CHUNK), F32)],
        compiler_params=_cparams(("arbitrary",)),
        name="retention",
    )(u, cos, sin, lg)


def _swa_body(sink_ref, u_ref, cos_ref, sin_ref, o_ref, q_s, k_s, v_s, *, lc):
    total = u_ref.shape[1]
    t_lat = total - lc
    gw = GROUP_W
    blk = WINDOW
    nblk = t_lat // blk
    kv0 = lc + blk
    scale = HEAD_DIM ** -0.5
    zpad = jnp.zeros((blk, SWA_KV_W), F32)
    for s_ref in (k_s, v_s):
        s_ref[lc:lc + blk, :] = zpad
        s_ref[kv0 + t_lat:kv0 + t_lat + blk, :] = zpad
    for r0, _, _ in _row_blocks(lc, total):
        cos = cos_ref[r0:r0 + ROW_BLOCK, :]
        sin = sin_ref[r0:r0 + ROW_BLOCK, :]
        q_s[r0:r0 + ROW_BLOCK, :] = _rope(u_ref[0, r0:r0 + ROW_BLOCK, 0:gw], cos, sin) * scale
        kr = _rope(u_ref[0, r0:r0 + ROW_BLOCK, gw:gw + SWA_KV_W], cos[:, 0:SWA_KV_W], sin[:, 0:SWA_KV_W])
        dst = r0 if r0 < lc else r0 + blk
        k_s[dst:dst + ROW_BLOCK, :] = kr
        v_s[dst:dst + ROW_BLOCK, :] = u_ref[0, r0:r0 + ROW_BLOCK, gw + SWA_KV_W:gw + 2 * SWA_KV_W]

    grp = N_HEADS // SWA_KV_HEADS

    def attend(items):
        scores = [[_dot_nt(q2, kk) for kk in keys] for q2, keys, _, _, _ in items]
        exps, dens = [], []
        for (_, _, _, masks, sink_col), sc in zip(items, scores):
            sc = [s if mk is None else jnp.where(mk, s, NEG_INF) for s, mk in zip(sc, masks)]
            mx = sink_col
            for s in sc:
                mx = jnp.maximum(mx, jnp.max(s, axis=-1, keepdims=True))
            es = [jnp.exp(s - mx) for s in sc]
            den = jnp.exp(sink_col - mx)
            for e in es:
                den = den + jnp.sum(e, axis=-1, keepdims=True)
            exps.append([e.astype(BF16) for e in es])
            dens.append(den)
        outs = []
        for (_, _, vals, _, _), es, den in zip(items, exps, dens):
            acc = None
            for e, vv in zip(es, vals):
                t = jnp.dot(e, vv.astype(BF16), preferred_element_type=F32)
                acc = t if acc is None else acc + t
            outs.append(acc / den)
        return outs

    def sink_column(hk, rows_per_head):
        row = lax.broadcasted_iota(jnp.int32, (grp * rows_per_head, 1), 0)
        col = jnp.full((grp * rows_per_head, 1), sink_ref[hk * grp], F32)
        for g in range(1, grp):
            col = jnp.where(row >= g * rows_per_head, sink_ref[hk * grp + g], col)
        return col

    kvs = [slice(hk * HEAD_DIM, (hk + 1) * HEAD_DIM) for hk in range(SWA_KV_HEADS)]

    def stack_q(hk, rows):
        return jnp.concatenate(
            [q_s[rows, (hk * grp + g) * HEAD_DIM:(hk * grp + g + 1) * HEAD_DIM] for g in range(grp)], axis=0)

    def unstack_o(o2s, n):
        return jnp.concatenate([o2s[hk][g * n:(g + 1) * n, :]
                                for hk in range(SWA_KV_HEADS) for g in range(grp)], axis=1)

    o2s = attend([(stack_q(hk, slice(0, lc)), [k_s[0:lc, kvs[hk]]], [v_s[0:lc, kvs[hk]]], [None],
                   sink_column(hk, lc)) for hk in range(SWA_KV_HEADS)])
    z = u_ref[0, 0:lc, gw + 2 * SWA_KV_W:2 * gw + 2 * SWA_KV_W]
    o_ref[0, 0:lc, :] = (unstack_o(o2s, lc) * _silu(z)).astype(BF16)

    qi = lax.broadcasted_iota(jnp.int32, (grp * blk, 3 * blk), 0) % blk
    kj = lax.broadcasted_iota(jnp.int32, (grp * blk, 3 * blk), 1)
    in_win = jnp.abs(kj - blk - qi) <= WINDOW

    def block_body(n, carry):
        kpos = n * blk - blk + kj
        mask = in_win & (kpos >= 0) & (kpos < t_lat)
        qrow = pl.multiple_of(lc + n * blk, blk)
        wrow = pl.multiple_of(lc + n * blk, blk)
        o2s = attend([(stack_q(hk, pl.ds(qrow, blk)),
                       [k_s[pl.ds(wrow, 3 * blk), kvs[hk]], k_s[0:lc, kvs[hk]]],
                       [v_s[pl.ds(wrow, 3 * blk), kvs[hk]], v_s[0:lc, kvs[hk]]],
                       [mask, None], sink_column(hk, blk)) for hk in range(SWA_KV_HEADS)])
        z = u_ref[0, pl.ds(qrow, blk), gw + 2 * SWA_KV_W:2 * gw + 2 * SWA_KV_W]
        o_ref[0, pl.ds(qrow, blk), :] = (unstack_o(o2s, blk) * _silu(z)).astype(BF16)
        return carry

    lax.fori_loop(0, nblk, block_body, 0)


def _swa(u, cos, sin, sink, *, lc):
    nb, total, _ = u.shape
    t_lat = total - lc
    full = lambda shape: pl.BlockSpec(shape, lambda b, s: (0,) * len(shape))
    grid_spec = pltpu.PrefetchScalarGridSpec(
        num_scalar_prefetch=1,
        grid=(nb,),
        in_specs=[pl.BlockSpec((1, total, W_SWA), lambda b, s: (b, 0, 0)),
                  full((total, GROUP_W)), full((total, GROUP_W))],
        out_specs=pl.BlockSpec((1, total, GROUP_W), lambda b, s: (b, 0, 0)),
        scratch_shapes=[pltpu.VMEM((total, GROUP_W), F32),
                        pltpu.VMEM((total + 2 * WINDOW, SWA_KV_W), F32),
                        pltpu.VMEM((total + 2 * WINDOW, SWA_KV_W), F32)],
    )
    return pl.pallas_call(
        functools.partial(_swa_body, lc=lc),
        grid_spec=grid_spec,
        out_shape=jax.ShapeDtypeStruct((nb, total, GROUP_W), BF16),
        compiler_params=_cparams(("arbitrary",)),
        name="swa",
    )(sink, u, cos, sin)


def _gdn_body(u_ref, ab_ref, cw_ref, avec_ref, dtb_ref, ng_ref, o_ref,
              q_s, k_s, v_s, cs_s, gt_s, of_s, ob_s, st_s, *ring_refs, lc):
    ring = (ring_refs[0:4], ring_refs[4:8])
    total = u_ref.shape[1]
    gw = GROUP_W
    c = GDN_CHUNK
    nch = total // c
    n_ctx = lc // c
    per_blk = ROW_BLOCK // c
    ones_bd = _head_ones()

    ri = lax.broadcasted_iota(jnp.int32, (ROW_BLOCK, ROW_BLOCK), 0)
    ci_ = lax.broadcasted_iota(jnp.int32, (ROW_BLOCK, ROW_BLOCK), 1)
    same_chunk = (ri // c) == (ci_ // c)
    tri_lo = (same_chunk & (ri >= ci_)).astype(BF16)
    tri_up = (same_chunk & (ri <= ci_)).astype(BF16)
    lane = lax.broadcasted_iota(jnp.int32, (ROW_BLOCK, W_AB), 1)

    for bi, (r0, lo, hi) in enumerate(_row_blocks(lc, total)):
        h = _silu(_dwconv_rows(u_ref, r0, ROW_BLOCK, lo, hi, 0, 3 * gw, cw_ref))
        hq, hk, hv = h[:, 0:gw], h[:, gw:2 * gw], h[:, 2 * gw:3 * gw]
        ssq = _split_dot(hq * hq, ones_bd, 2)
        q_s[r0:r0 + ROW_BLOCK, :] = hq * lax.rsqrt(ssq + NORM_EPS) * (HEAD_DIM ** -0.5)
        ssk = _split_dot(hk * hk, ones_bd, 2)
        k_s[r0:r0 + ROW_BLOCK, :] = hk * lax.rsqrt(ssk + NORM_EPS)
        v_s[r0:r0 + ROW_BLOCK, :] = hv
        ab = ab_ref[0, r0:r0 + ROW_BLOCK, :]
        g = -avec_ref[...] * _softplus(ab + dtb_ref[...])
        g = jnp.where(lane < 2 * N_HEADS, g, 0.0)
        beta = _sigmoid(ab)
        cs_lo = _split_dot_left(tri_lo, g, 3)
        cs_up = _split_dot_left(tri_up, g, 3)
        cs = jnp.where(lane < N_HEADS, cs_lo, jnp.where(lane < 2 * N_HEADS, cs_up, beta))
        cs_s[r0:r0 + ROW_BLOCK, :] = cs
        cst = cs.T
        for cc in range(per_blk):
            row0 = (bi * per_blk + cc) * SUBLANES
            gt_s[row0:row0 + SUBLANES, :] = cst[0:SUBLANES, cc * c:(cc + 1) * c]

    ii = lax.broadcasted_iota(jnp.int32, (c, c), 0)
    jj = lax.broadcasted_iota(jnp.int32, (c, c), 1)
    eye = (ii == jj).astype(F32)
    n_sq = c.bit_length() - 2
    incl = ((ii >= jj), (ii <= jj))
    strict = ((ii > jj), (ii < jj))
    last = (c - 1, 0)
    heads = [slice(h * HEAD_DIM, (h + 1) * HEAD_DIM) for h in range(N_HEADS)]
    e_src = lax.broadcasted_iota(jnp.int32, (3 * W_AB, 2 * N_HEADS * 128), 0) % W_AB
    e_grp = lax.broadcasted_iota(jnp.int32, (3 * W_AB, 2 * N_HEADS * 128), 1) // 128
    expand3 = [(e_src == jnp.where(e_grp < N_HEADS, d * N_HEADS + e_grp,
                                   (2 + d) * N_HEADS + e_grp - N_HEADS)).astype(BF16)
               for d in range(2)]

    blk_rows = GDN_B1_CHUNKS * c
    n_steps = total // blk_rows
    assert lc == blk_rows and n_steps >= 2

    def block_row0(j, d):
        if d == 0:
            return j * blk_rows
        if isinstance(j, int):
            return 0 if j == 0 else total + lc - (j + 1) * blk_rows
        return pl.multiple_of(jnp.where(j == 0, 0, total + lc - (j + 1) * blk_rows), blk_rows)

    def cat_heads(items):
        return jnp.concatenate(items, axis=1)

    def b1_gen(j, slot):
        uvr_s, wqr_s, attr_s, kttr_s = ring[slot]
        probs = []
        for d in range(2):
            r0 = block_row0(j, d)
            parts = []
            rem = cs_s[pl.ds(r0, blk_rows), :]
            for _ in range(3):
                hi = rem.astype(BF16)
                parts.append(hi)
                rem = rem - hi.astype(F32)
            xp = jnp.dot(jnp.concatenate(parts, axis=1), expand3[d], preferred_element_type=F32)
            for cc in range(GDN_B1_CHUNKS):
                r = r0 + cc * c
                kc = k_s[pl.ds(r, c), :]
                qc = q_s[pl.ds(r, c), :]
                vc = v_s[pl.ds(r, c), :]
                gt = gt_s[pl.ds(pl.multiple_of((r0 // c + cc) * SUBLANES, SUBLANES), SUBLANES), :]
                sel = []
                for h in range(N_HEADS):
                    kh, qh, vh = kc[:, heads[h]], qc[:, heads[h]], vc[:, heads[h]]
                    kk = _dot_nt(kh, kh)
                    qk = _dot_nt(qh, kh)
                    col = xp[cc * c:(cc + 1) * c, h * 128:h * 128 + HEAD_DIM]
                    bcol = xp[cc * c:(cc + 1) * c, (N_HEADS + h) * 128:(N_HEADS + h) * 128 + HEAD_DIM]
                    rowv = gt[d * N_HEADS + h:d * N_HEADS + h + 1, :]
                    decay = jnp.where(incl[d], jnp.exp(jnp.where(incl[d], col - rowv, 0.0)), 0.0)
                    n = -jnp.where(strict[d], kk * decay, 0.0) * bcol
                    eg = jnp.exp(col)
                    glast = col[last[d]:last[d] + 1, :]
                    pr = dict(cc=cc, d=d, a=(eye - n).astype(BF16), p=eye + n,
                              rhs=jnp.concatenate([vh * bcol, kh * (bcol * eg)], axis=1).astype(BF16),
                              att=qk * decay, qe=qh * eg, kt=kh * jnp.exp(glast - col))
                    probs.append(pr)
                    sel.append(pr)
                wqr_s[d,(2 * cc + 1) * c:(2 * cc + 2) * c, :] = (
                    cat_heads([pr['qe'] for pr in sel]).astype(BF16))
                attr_s[d,cc * c:(cc + 1) * c, :] = cat_heads([pr['att'] for pr in sel]).astype(BF16)
                ktt = cat_heads([pr['kt'] for pr in sel]).T
                kttr_s[d,cc * gw:(cc + 1) * gw, :] = ktt.astype(BF16)
        yield
        for _ in range(n_sq):
            res = [eye - jnp.dot(pr['a'], pr['p'].astype(BF16), preferred_element_type=F32)
                   for pr in probs]
            yield
            for pr, rr in zip(probs, res):
                pr['p'] = pr['p'] + _dot(pr['p'], rr)
            yield
        for pr in probs:
            pr['uw'] = _dot(pr['p'], pr['rhs'])
        yield
        for d in range(2):
            for cc in range(GDN_B1_CHUNKS):
                sel = [pr for pr in probs if pr['cc'] == cc and pr['d'] == d]
                uvr_s[d,cc * c:(cc + 1) * c, :] = cat_heads([pr['uw'][:, 0:HEAD_DIM] for pr in sel])
                wqr_s[d,2 * cc * c:(2 * cc + 1) * c, :] = (
                    cat_heads([pr['uw'][:, HEAD_DIM:2 * HEAD_DIM] for pr in sel]).astype(BF16))

    def b2_gen(j, slot):
        uvr_s, wqr_s, attr_s, kttr_s = ring[slot]
        r0s = [block_row0(j, d) for d in range(2)]
        states = [st_s[hd] for hd in range(2 * N_HEADS)]
        for step in range(GDN_B1_CHUNKS):
            probs = []
            for d in range(2):
                cc = step if d == 0 else GDN_B1_CHUNKS - 1 - step
                r = r0s[d] + cc * c
                wqm = wqr_s[d, 2 * cc * c:(2 * cc + 2) * c, :]
                uvm = uvr_s[d, cc * c:(cc + 1) * c, :]
                attm = attr_s[d, cc * c:(cc + 1) * c, :]
                grow = cs_s[pl.ds(r + last[d], 1), :]
                for h in range(N_HEADS):
                    hd = d * N_HEADS + h
                    kt = kttr_s[d, cc * gw + h * HEAD_DIM:cc * gw + (h + 1) * HEAD_DIM, :]
                    probs.append(dict(d=d, hd=hd, r=r, wqm=wqm[:, heads[h]],
                                      uv=uvm[:, heads[h]], att=attm[:, heads[h]], kt=kt,
                                      gl=jnp.exp(grow[:, hd:hd + 1])))
            for pr in probs:
                pr['wq'] = jnp.dot(pr['wqm'], states[pr['hd']].astype(BF16), preferred_element_type=F32)
            yield
            for pr in probs:
                vn = (pr['uv'] - pr['wq'][0:c, :]).astype(BF16)
                pr['o'] = pr['wq'][c:2 * c, :] + jnp.dot(pr['att'], vn, preferred_element_type=F32)
                states[pr['hd']] = (states[pr['hd']] * pr['gl']
                                    + jnp.dot(pr['kt'], vn, preferred_element_type=F32))
            for d, o_s in ((0, of_s), (1, ob_s)):
                sel = [pr for pr in probs if pr['d'] == d]
                o_s[pl.ds(sel[0]['r'], c), :] = cat_heads([pr['o'] for pr in sel])
            yield
        for hd in range(2 * N_HEADS):
            st_s[hd] = states[hd]

    def run_interleaved(gens):
        live = list(gens)
        while live:
            still = []
            for g in live:
                try:
                    next(g)
                    still.append(g)
                except StopIteration:
                    pass
            live = still

    st_s[...] = jnp.zeros(st_s.shape, F32)
    run_interleaved([b1_gen(0, 0)])

    def pipe_body(t, carry):
        j = 1 + 2 * t
        run_interleaved([b1_gen(j, 1), b2_gen(j - 1, 0)])
        run_interleaved([b1_gen(j + 1, 0), b2_gen(j, 1)])
        return carry

    lax.fori_loop(0, (n_steps - 1) // 2, pipe_body, 0)
    if (n_steps - 1) % 2:
        run_interleaved([b1_gen(n_steps - 1, (n_steps - 1) % 2), b2_gen(n_steps - 2, n_steps % 2)])
    run_interleaved([b2_gen(n_steps - 1, (n_steps - 1) % 2)])

    for r0, _, _ in _row_blocks(lc, total):
        o = of_s[r0:r0 + ROW_BLOCK, :] + ob_s[r0:r0 + ROW_BLOCK, :]
        ms = _split_dot(o * o, ones_bd, 2) * (1.0 / HEAD_DIM)
        z = u_ref[0, r0:r0 + ROW_BLOCK, 3 * gw:4 * gw]
        y = o * lax.rsqrt(ms + NORM_EPS) * ng_ref[...] * _silu(z)
        o_ref[0, r0:r0 + ROW_BLOCK, :] = y.astype(BF16)


def _gdn(u, ab, conv_w, a_log, dt_bias, norm_g, *, lc):
    nb, total, _ = u.shape
    nch = total // GDN_CHUNK
    blk = GDN_B1_CHUNKS * GDN_CHUNK
    pad = W_AB - 2 * N_HEADS
    avec = jnp.pad(jnp.exp(a_log.reshape(1, 2 * N_HEADS)), ((0, 0), (0, pad)))
    dtb = jnp.pad(dt_bias.reshape(1, 2 * N_HEADS), ((0, 0), (0, pad)))
    ng = jnp.tile(norm_g.reshape(1, HEAD_DIM), (1, N_HEADS))
    full = lambda shape: pl.BlockSpec(shape, lambda b: (0,) * len(shape))
    return pl.pallas_call(
        functools.partial(_gdn_body, lc=lc),
        grid=(nb,),
        in_specs=[pl.BlockSpec((1, total, W_GDN), lambda b: (b, 0, 0), pipeline_mode=pl.Buffered(1)),
                  pl.BlockSpec((1, total, W_AB), lambda b: (b, 0, 0), pipeline_mode=pl.Buffered(1)),
                  full((CONV_W, 3 * GROUP_W)), full((1, W_AB)), full((1, W_AB)),
                  full((1, GROUP_W))],
        out_specs=pl.BlockSpec((1, total, GROUP_W), lambda b: (b, 0, 0)),
        out_shape=jax.ShapeDtypeStruct((nb, total, GROUP_W), BF16),
        scratch_shapes=[pltpu.VMEM((total, GROUP_W), F32),
                        pltpu.VMEM((total, GROUP_W), F32),
                        pltpu.VMEM((total, GROUP_W), F32),
                        pltpu.VMEM((total, W_AB), F32),
                        pltpu.VMEM((nch * SUBLANES, GDN_CHUNK), F32),
                        pltpu.VMEM((total, GROUP_W), F32),
                        pltpu.VMEM((total, GROUP_W), F32),
                        pltpu.VMEM((2 * N_HEADS, HEAD_DIM, HEAD_DIM), F32)] + 2 * [
                        pltpu.VMEM((2, blk, GROUP_W), F32),
                        pltpu.VMEM((2, 2 * blk, GROUP_W), BF16),
                        pltpu.VMEM((2, blk, GROUP_W), BF16),
                        pltpu.VMEM((2, GDN_B1_CHUNKS * GROUP_W, GDN_CHUNK), BF16)],
        compiler_params=_cparams(("arbitrary",)),
        name="gdn",
    )(u, ab, conv_w, avec, dtb, ng)


def _pack_w_in(w_in):
    gw = GROUP_W
    o_gdn = 2 * gw
    o_ab = o_gdn + 4 * gw
    o_ret = o_ab + 4 * N_HEADS
    o_swa = o_ret + 4 * gw
    end = o_swa + W_SWA
    assert end == w_in.shape[-1]
    pad = jnp.zeros(w_in.shape[:-1] + (W_AB - 4 * N_HEADS,), w_in.dtype)
    return jnp.concatenate([w_in[..., 0:o_gdn], w_in[..., o_gdn:o_ab], w_in[..., o_ret:o_swa],
                            w_in[..., o_swa:end], w_in[..., o_ab:o_ret], pad], axis=-1).astype(BF16)


def _rope_tables(ang, lc):
    cos = jnp.cos(ang)
    sin = jnp.sin(ang)
    cos_h = jnp.concatenate([cos, cos], axis=-1)
    sin_h = jnp.concatenate([-sin, sin], axis=-1)
    cos_t = jnp.tile(cos_h, (1, N_HEADS))
    sin_t = jnp.tile(sin_h, (1, N_HEADS))
    cos_t = jnp.concatenate([jnp.ones((lc, GROUP_W), F32), cos_t], axis=0)
    sin_t = jnp.concatenate([jnp.zeros((lc, GROUP_W), F32), sin_t], axis=0)
    return cos_t, sin_t


def _rope_freqs(pos, n):
    inv = ROPE_BASE ** (-jnp.arange(0, n, 2, dtype=F32) / n)
    return pos[:, None] * inv[None, :]


def kernel(x, c, ctx, c_ctx, w_mod, b_mod, pre_norm_g, post_norm_g, w_in, w_out, lru_conv_w, lru_conv_b, lru_w_r, lru_b_r, lru_w_i, lru_b_i, lru_lambda, gdn_conv_w, gdn_a_log, gdn_dt_bias, gdn_norm_g, ret_decay_logit, swa_sink):
    nb, t, d = x.shape
    lc = ctx.shape[1]
    depth = w_mod.shape[0]
    assert t % ROW_BLOCK == 0 and lc % ROW_BLOCK == 0 and d == 4 * GROUP_W

    rows = t // GRID_W
    row = jnp.repeat(jnp.arange(rows, dtype=F32), GRID_W)
    col = jnp.tile(jnp.arange(GRID_W, dtype=F32), rows)
    ang2d = jnp.concatenate([_rope_freqs(row, HEAD_DIM // 2), _rope_freqs(col, HEAD_DIM // 2)], axis=-1)
    ang1d = _rope_freqs(jnp.arange(t, dtype=F32), HEAD_DIM)
    cos1, sin1 = _rope_tables(ang1d, lc)
    cos2, sin2 = _rope_tables(ang2d, lc)

    mod_rows = -(-(nb + 1) // SUBLANES) * SUBLANES
    s_rows = jnp.concatenate([c, c_ctx[None, :], jnp.zeros((mod_rows - nb - 1, d), F32)], axis=0)
    mod = _modulation(s_rows, w_mod, b_mod)

    w_in_p = _pack_w_in(w_in)
    w_out_b = w_out.astype(BF16)
    xs = jnp.concatenate([ctx, x], axis=1)
    for l in range(depth):
        u_lru, u_gdn, u_ret, u_swa, u_ab = _in_proj(xs, mod[l], pre_norm_g[l], w_in_p[l], lc=lc)
        ya = _lru(u_lru, lru_conv_w[l], lru_conv_b[l], lru_w_r[l], lru_b_r[l], lru_w_i[l], lru_b_i[l],
                  lru_lambda[l], lc=lc)
        yb = _gdn(u_gdn, u_ab, gdn_conv_w[l], gdn_a_log[l], gdn_dt_bias[l], gdn_norm_g[l], lc=lc)
        yc = _ret(u_ret, cos1, sin1, ret_decay_logit[l], lc=lc)
        yd = _swa(u_swa, cos2, sin2, swa_sink[l], lc=lc)
        xs = _out_proj((ya, yb, yc, yd), xs, mod[l], post_norm_g[l], w_out_b[l], lc=lc)
    return xs[:, lc:, :]
```

```python
import functools

import jax
import jax.numpy as jnp
from jax import lax
from jax.experimental import pallas as pl
from jax.experimental.pallas import tpu as pltpu

F32 = jnp.float32
BF16 = jnp.bfloat16

HEAD_DIM = 64
GROUP_W = 256
N_HEADS = GROUP_W // HEAD_DIM
NORM_EPS = 1e-6
ROPE_BASE = 10000.0
NEG_INF = -1e30
GRID_W = 64
CONV_W = 4
LRU_C = 8.0
LRU_SCAN_UNROLL = 4
GDN_CHUNK = 64
GDN_B1_CHUNKS = 4
RET_CHUNK = 256
SWA_KV_HEADS = 2
SWA_KV_W = SWA_KV_HEADS * HEAD_DIM
WINDOW = 128
ROW_BLOCK = 256
PROJ_ROWS_MAX = 576
SUBLANES = 8
VMEM_LIMIT_BYTES = 56 * 1024 * 1024

W_LRU = 2 * GROUP_W
W_GDN = 4 * GROUP_W
W_RET = 4 * GROUP_W
W_SWA = 2 * GROUP_W + 2 * SWA_KV_W
W_AB = 128
IN_W_PACKED = W_LRU + W_GDN + W_RET + W_SWA + W_AB


def _cparams(sem, flags=None):
    return pltpu.CompilerParams(dimension_semantics=sem, vmem_limit_bytes=VMEM_LIMIT_BYTES, flags=flags)


def _dot(a, b):
    return jnp.dot(a.astype(BF16), b.astype(BF16), preferred_element_type=F32)


def _dot_nt(a, b):
    return lax.dot_general(a.astype(BF16), b.astype(BF16), (((1,), (1,)), ((), ())),
                           preferred_element_type=F32)


def _dot_tn(a, b):
    return lax.dot_general(a.astype(BF16), b.astype(BF16), (((0,), (0,)), ((), ())),
                           preferred_element_type=F32)


def _split_dot(x, w, parts):
    acc = None
    r = x
    for _ in range(parts):
        hi = r.astype(BF16)
        t = jnp.dot(hi, w, preferred_element_type=F32)
        acc = t if acc is None else acc + t
        r = r - hi.astype(F32)
    return acc


def _split_dot_left(w, x, parts):
    acc = None
    r = x
    for _ in range(parts):
        hi = r.astype(BF16)
        t = jnp.dot(w, hi, preferred_element_type=F32)
        acc = t if acc is None else acc + t
        r = r - hi.astype(F32)
    return acc


def _sigmoid(x):
    return 0.5 * jnp.tanh(0.5 * x) + 0.5


def _silu(x):
    return x * _sigmoid(x)


def _softplus(x):
    return jnp.maximum(x, 0.0) + jnp.log1p(jnp.exp(-jnp.abs(x)))


def _head_ones():
    r = lax.broadcasted_iota(jnp.int32, (GROUP_W, GROUP_W), 0) // HEAD_DIM
    c = lax.broadcasted_iota(jnp.int32, (GROUP_W, GROUP_W), 1) // HEAD_DIM
    return (r == c).astype(BF16)


def _dwconv_rows(x_ref, r0, n, seg_lo, seg_hi, c0, c1, w_ref):
    width = c1 - c0
    if r0 - 2 >= seg_lo and r0 + n + 1 <= seg_hi:
        acc = None
        for k in range(CONV_W):
            term = x_ref[0, r0 + k - 2:r0 + k - 2 + n, c0:c1] * w_ref[k:k + 1, :]
            acc = term if acc is None else acc + term
        return acc
    zeros = jnp.zeros((SUBLANES, width), F32)
    prev = x_ref[0, r0 - SUBLANES:r0, c0:c1] if r0 > seg_lo else zeros
    nxt = x_ref[0, r0 + n:r0 + n + SUBLANES, c0:c1] if r0 + n < seg_hi else zeros
    win = jnp.concatenate([prev, x_ref[0, r0:r0 + n, c0:c1], nxt], axis=0)
    total = n + 2 * SUBLANES
    acc = None
    for k in range(CONV_W):
        shift = (2 - k) % total
        tap = win if shift == 0 else pltpu.roll(win, shift, 0)
        term = tap[SUBLANES:SUBLANES + n] * w_ref[k:k + 1, :]
        acc = term if acc is None else acc + term
    return acc


def _row_blocks(lc, total):
    out = []
    for r0 in range(0, total, ROW_BLOCK):
        out.append((r0, 0, lc) if r0 < lc else (r0, lc, total))
    return out


def _mod_body(s_ref, w_ref, b_ref, o_ref):
    s = _silu(s_ref[...])
    o_ref[0] = _dot(s, w_ref[0]) + b_ref[0]


def _modulation(s_rows, w_mod, b_mod):
    depth, d, d3 = w_mod.shape
    rows = s_rows.shape[0]
    nt = d3 // d
    return pl.pallas_call(
        _mod_body,
        grid=(depth, nt),
        in_specs=[pl.BlockSpec((rows, d), lambda l, j: (0, 0)),
                  pl.BlockSpec((1, d, d), lambda l, j: (l, 0, j)),
                  pl.BlockSpec((1, 1, d), lambda l, j: (l, 0, j))],
        out_specs=pl.BlockSpec((1, rows, d), lambda l, j: (l, 0, j)),
        out_shape=jax.ShapeDtypeStruct((depth, rows, d3), F32),
        compiler_params=_cparams(("arbitrary", "arbitrary")),
        name="modulation",
    )(s_rows, w_mod, b_mod.reshape(depth, 1, d3))


def _proj_rows(total):
    for tm in range(PROJ_ROWS_MAX, SUBLANES - 1, -SUBLANES):
        if total % tm == 0:
            return tm
    raise ValueError(total)


def _mod_rows(mod_ref, b, i, tm, lc, nb, c0, c1):
    lat = mod_ref[pl.ds(b, 1), c0:c1]
    if lc % tm == 0:
        ctx = mod_ref[nb:nb + 1, c0:c1]
        return jnp.where(i * tm < lc, ctx, lat)
    ctx = mod_ref[nb:nb + 1, c0:c1]
    row = i * tm + lax.broadcasted_iota(jnp.int32, (tm, 1), 0)
    return jnp.where(row < lc, ctx, lat)


def _inproj_body(x_ref, mod_ref, g_ref, w_ref, o_lru, o_gdn, o_ret, o_swa, o_ab, *, lc, nb):
    b = pl.program_id(0)
    i = pl.program_id(1)
    d = x_ref.shape[2]
    x = x_ref[0]
    ms = jnp.mean(x * x, axis=-1, keepdims=True)
    y = x * lax.rsqrt(ms + NORM_EPS) * g_ref[...]
    m = _mod_rows(mod_ref, b, i, x.shape[0], lc, nb, 0, 2 * d)
    h = (y * (1.0 + m[:, d:2 * d]) + m[:, 0:d]).astype(BF16)
    off = 0
    for o_ref in (o_lru, o_gdn, o_ret, o_swa, o_ab):
        w = o_ref.shape[2]
        o_ref[0] = jnp.dot(h, w_ref[:, off:off + w], preferred_element_type=F32)
        off += w


def _in_proj(x, mod, g, w, l, *, lc):
    nb, total, d = x.shape
    tm = _proj_rows(total)
    nt = total // tm
    widths = (W_LRU, W_GDN, W_RET, W_SWA, W_AB)
    return pl.pallas_call(
        functools.partial(_inproj_body, lc=lc, nb=nb),
        grid=(nb, nt),
        in_specs=[pl.BlockSpec((1, tm, d), lambda b, i: (b, i, 0)),
                  _layer(mod, l), _layer(g, l), _layer(w, l)],
        out_specs=[pl.BlockSpec((1, tm, wd), lambda b, i: (b, i, 0)) for wd in widths],
        out_shape=[jax.ShapeDtypeStruct((nb, total, wd), F32) for wd in widths],
        compiler_params=_cparams(("arbitrary", "arbitrary")),
        name="in_proj",
    )(x, mod, g, w)


def _outproj_body(ya, yb, yc, yd, x_ref, mod_ref, g_ref, w_ref, o_ref, *, lc, nb):
    b = pl.program_id(0)
    i = pl.program_id(1)
    d = x_ref.shape[2]
    acc = None
    for k, y_ref in enumerate((ya, yb, yc, yd)):
        t = jnp.dot(y_ref[0], w_ref[k * GROUP_W:(k + 1) * GROUP_W, :], preferred_element_type=F32)
        acc = t if acc is None else acc + t
    ms = jnp.mean(acc * acc, axis=-1, keepdims=True)
    yn = acc * lax.rsqrt(ms + NORM_EPS) * g_ref[...]
    gate = _mod_rows(mod_ref, b, i, acc.shape[0], lc, nb, 2 * d, 3 * d)
    o_ref[0] = x_ref[0] + gate * yn


def _out_proj(ys, x, mod, g, w, l, *, lc):
    nb, total, d = x.shape
    tm = _proj_rows(total)
    nt = total // tm
    yspec = pl.BlockSpec((1, tm, GROUP_W), lambda b, i: (b, i, 0))
    return pl.pallas_call(
        functools.partial(_outproj_body, lc=lc, nb=nb),
        grid=(nb, nt),
        in_specs=[yspec, yspec, yspec, yspec,
                  pl.BlockSpec((1, tm, d), lambda b, i: (b, i, 0)),
                  _layer(mod, l), _layer(g, l), _layer(w, l)],
        out_specs=pl.BlockSpec((1, tm, d), lambda b, i: (b, i, 0)),
        out_shape=jax.ShapeDtypeStruct(x.shape, F32),
        compiler_params=_cparams(("arbitrary", "arbitrary")),
        name="out_proj",
    )(*ys, x, mod, g, w)


def _lru_scan(a_s, b_s, h_s, tile_lo, n_tiles, carry, *, rev, accumulate):
    row = lax.broadcasted_iota(jnp.int32, (SUBLANES, GROUP_W), 0)

    def body(j, carry):
        t = tile_lo + (n_tiles - 1 - j if rev else j)
        r = pl.multiple_of(t * SUBLANES, SUBLANES)
        a = a_s[pl.ds(r, SUBLANES), :]
        b = b_s[pl.ds(r, SUBLANES), :]
        for s in (1, 2, 4):
            if rev:
                ra = pltpu.roll(a, SUBLANES - s, 0)
                rb = pltpu.roll(b, SUBLANES - s, 0)
                m = row < SUBLANES - s
            else:
                ra = pltpu.roll(a, s, 0)
                rb = pltpu.roll(b, s, 0)
                m = row >= s
            b = a * jnp.where(m, rb, 0.0) + b
            a = a * jnp.where(m, ra, 1.0)
        h = a * carry + b
        if accumulate:
            h_s[pl.ds(r, SUBLANES), :] = h_s[pl.ds(r, SUBLANES), :] + h
        else:
            h_s[pl.ds(r, SUBLANES), :] = h
        return h[0:1, :] if rev else h[SUBLANES - 1:SUBLANES, :]

    return lax.fori_loop(0, n_tiles, body, carry, unroll=LRU_SCAN_UNROLL)


def _lru_body(u_ref, cw_ref, cb_ref, wg_ref, bg_ref, lam_ref, o_ref, uc_s, a_s, b_s, h_s, *, lc):
    total = u_ref.shape[1]
    blocks = _row_blocks(lc, total)
    for r0, lo, hi in blocks:
        uc_s[r0:r0 + ROW_BLOCK, :] = (
            _dwconv_rows(u_ref, r0, ROW_BLOCK, lo, hi, 0, GROUP_W, cw_ref) + cb_ref[...])
    zero = jnp.zeros((1, GROUP_W), F32)
    for d in range(2):
        sp = _softplus(-lam_ref[d])
        for r0, _, _ in blocks:
            uc = uc_s[r0:r0 + ROW_BLOCK, :]
            gts = _dot(uc, wg_ref[d]) + bg_ref[d]
            r = _sigmoid(gts[:, 0:GROUP_W])
            ig = _sigmoid(gts[:, GROUP_W:2 * GROUP_W])
            a = jnp.exp(-LRU_C * r * sp)
            a_s[r0:r0 + ROW_BLOCK, :] = a
            b_s[r0:r0 + ROW_BLOCK, :] = jnp.sqrt(1.0 - a * a) * (ig * uc)
        ct, tt = lc // SUBLANES, total // SUBLANES
        if d == 0:
            _lru_scan(a_s, b_s, h_s, 0, tt, zero, rev=False, accumulate=False)
        else:
            carry = _lru_scan(a_s, b_s, h_s, 0, ct, zero, rev=True, accumulate=True)
            _lru_scan(a_s, b_s, h_s, ct, tt - ct, carry, rev=True, accumulate=True)
    for r0, _, _ in blocks:
        z = u_ref[0, r0:r0 + ROW_BLOCK, GROUP_W:2 * GROUP_W]
        o_ref[0, r0:r0 + ROW_BLOCK, :] = (h_s[r0:r0 + ROW_BLOCK, :] * _silu(z)).astype(BF16)


def _block_diag(w):
    n, c = w.shape[-3], w.shape[-1]
    eye = jnp.eye(n, dtype=w.dtype)
    return (eye[:, None, :, None] * w[..., :, :, None, :]).reshape(w.shape[:-3] + (n * c, n * c))


def _layer(arr, l):
    shape = arr.shape[1:]
    return pl.BlockSpec((None,) + shape, lambda *_: (l,) + (0,) * len(shape))


def _lru_params(conv_b, w_r, b_r, w_i, b_i, lam):
    depth = conv_b.shape[0]
    wg = jnp.concatenate([_block_diag(w_r), _block_diag(w_i)], axis=-1).astype(BF16)
    bg = jnp.concatenate([b_r, b_i], axis=-1).reshape(depth, 2, 1, 2 * GROUP_W)
    return conv_b.reshape(depth, 1, GROUP_W), wg, bg, lam.reshape(depth, 2, 1, GROUP_W)


def _lru(u, conv_w, conv_b, wg, bg, lam, l, *, lc):
    nb, total, _ = u.shape
    return pl.pallas_call(
        functools.partial(_lru_body, lc=lc),
        grid=(nb,),
        in_specs=[pl.BlockSpec((1, total, W_LRU), lambda b: (b, 0, 0)),
                  _layer(conv_w, l), _layer(conv_b, l), _layer(wg, l), _layer(bg, l), _layer(lam, l)],
        out_specs=pl.BlockSpec((1, total, GROUP_W), lambda b: (b, 0, 0)),
        out_shape=jax.ShapeDtypeStruct((nb, total, GROUP_W), BF16),
        scratch_shapes=[pltpu.VMEM((total, GROUP_W), F32) for _ in range(4)],
        compiler_params=_cparams(("arbitrary",)),
        name="lru",
    )(u, conv_w, conv_b, wg, bg, lam)


def _rope(x, cos, sin_signed):
    half = HEAD_DIM // 2
    outs = []
    for c0 in range(0, x.shape[1], 128):
        xs = x[:, c0:c0 + 128]
        lane = lax.broadcasted_iota(jnp.int32, xs.shape, 1)
        swapped = jnp.where((lane % HEAD_DIM) < half,
                            pltpu.roll(xs, 128 - half, 1), pltpu.roll(xs, half, 1))
        outs.append(xs * cos[:, c0:c0 + 128] + swapped * sin_signed[:, c0:c0 + 128])
    return outs[0] if len(outs) == 1 else jnp.concatenate(outs, axis=1)


def _ret_body(u_ref, cos_ref, sin_ref, lg_ref, o_ref, qr_s, kr_s, ds_s, st_s, m_s, *, lc):
    total = u_ref.shape[1]
    c = RET_CHUNK
    nch = total // c
    n_ctx = lc // c
    gw = GROUP_W
    ones_bd = _head_ones()
    bd_mask = ones_bd > 0
    lg = -_softplus(-lg_ref[...])
    lgf, lgb = lg[0:1, :], lg[1:2, :]
    pos = lax.broadcasted_iota(jnp.int32, (c, 1), 0).astype(F32)
    qdec_f = jnp.exp((pos + 1.0) * lgf)
    qdec_b = jnp.exp((c - pos) * lgb)
    kdec_f = jnp.exp((c - 1.0 - pos) * lgf)
    kdec_b = jnp.exp(pos * lgb)
    cdec_f = jnp.exp(float(c) * lgf)
    cdec_b = jnp.exp(float(c) * lgb)
    dij = (lax.broadcasted_iota(jnp.int32, (c, c), 0)
           - lax.broadcasted_iota(jnp.int32, (c, c), 1)).astype(F32)
    for h in range(N_HEADS):
        lf = lgf[:, h * HEAD_DIM:h * HEAD_DIM + 1]
        lb = lgb[:, h * HEAD_DIM:h * HEAD_DIM + 1]
        fwd = jnp.exp(jnp.maximum(dij, 0.0) * lf)
        bwd = jnp.exp(jnp.maximum(-dij, 0.0) * lb)
        m_s[h] = jnp.where(dij > 0, fwd, jnp.where(dij < 0, bwd, 2.0))

    for ci in range(nch):
        r0 = ci * c
        q = u_ref[0, r0:r0 + c, 0:gw]
        k = u_ref[0, r0:r0 + c, gw:2 * gw]
        v = u_ref[0, r0:r0 + c, 2 * gw:3 * gw]
        cos = cos_ref[r0:r0 + c, :]
        sin = sin_ref[r0:r0 + c, :]
        qr = _rope(q, cos, sin)
        kr = _rope(k, cos, sin) * (HEAD_DIM ** -0.5)
        qr_s[r0:r0 + c, :] = qr
        kr_s[r0:r0 + c, :] = kr
        ds_s[0, ci] = jnp.where(bd_mask, _dot_tn(kr * kdec_f, v), 0.0)
        ds_s[1, ci] = jnp.where(bd_mask, _dot_tn(kr * kdec_b, v), 0.0)

    s = jnp.zeros((gw, gw), F32)
    for ci in range(nch):
        st_s[0, ci] = s
        s = s * cdec_f + ds_s[0, ci]
    s = jnp.zeros((gw, gw), F32)
    for ci in list(range(n_ctx - 1, -1, -1)) + list(range(nch - 1, n_ctx - 1, -1)):
        st_s[1, ci] = s
        s = s * cdec_b + ds_s[1, ci]

    for ci in range(nch):
        r0 = ci * c
        qr = qr_s[r0:r0 + c, :]
        kr = kr_s[r0:r0 + c, :]
        v = u_ref[0, r0:r0 + c, 2 * gw:3 * gw]
        z = u_ref[0, r0:r0 + c, 3 * gw:4 * gw]
        hss = [slice(h * HEAD_DIM, (h + 1) * HEAD_DIM) for h in range(N_HEADS)]
        inter = _dot(qr * qdec_f, st_s[0, ci]) + _dot(qr * qdec_b, st_s[1, ci])
        atts = [_dot_nt(qr[:, hs], kr[:, hs]) for hs in hss]
        atts = [(a * m_s[h]).astype(BF16) for h, a in enumerate(atts)]
        o = inter + jnp.concatenate(
            [jnp.dot(a, v[:, hs].astype(BF16), preferred_element_type=F32) for a, hs in zip(atts, hss)],
            axis=1)
        mu = _split_dot(o, ones_bd, 2) * (1.0 / HEAD_DIM)
        dlt = o - mu
        var = _split_dot(dlt * dlt, ones_bd, 2) * (1.0 / HEAD_DIM)
        y = dlt * lax.rsqrt(var + NORM_EPS) * _silu(z)
        o_ref[0, r0:r0 + c, :] = y.astype(BF16)


def _ret(u, cos, sin, lg, l, *, lc):
    nb, total, _ = u.shape
    nch = total // RET_CHUNK
    full = lambda shape: pl.BlockSpec(shape, lambda b: (0,) * len(shape))
    return pl.pallas_call(
        functools.partial(_ret_body, lc=lc),
        grid=(nb,),
        in_specs=[pl.BlockSpec((1, total, W_RET), lambda b: (b, 0, 0)),
                  full((total, GROUP_W)), full((total, GROUP_W)), _layer(lg, l)],
        out_specs=pl.BlockSpec((1, total, GROUP_W), lambda b: (b, 0, 0)),
        out_shape=jax.ShapeDtypeStruct((nb, total, GROUP_W), BF16),
        scratch_shapes=[pltpu.VMEM((total, GROUP_W), F32),
                        pltpu.VMEM((total, GROUP_W), F32),
                        pltpu.VMEM((2, nch, GROUP_W, GROUP_W), F32),
                        pltpu.VMEM((2, nch, GROUP_W, GROUP_W), F32),
                        pltpu.VMEM((N_HEADS, RET_CHUNK, RET_CHUNK), F32)],
        compiler_params=_cparams(("arbitrary",)),
        name="retention",
    )(u, cos, sin, lg)


def _swa_body(sink_ref, u_ref, cos_ref, sin_ref, o_ref, q_s, k_s, v_s, *, lc, sink0):
    total = u_ref.shape[1]
    t_lat = total - lc
    gw = GROUP_W
    blk = WINDOW
    nblk = t_lat // blk
    kv0 = lc + blk
    scale = HEAD_DIM ** -0.5
    zpad = jnp.zeros((blk, SWA_KV_W), F32)
    for s_ref in (k_s, v_s):
        s_ref[lc:lc + blk, :] = zpad
        s_ref[kv0 + t_lat:kv0 + t_lat + blk, :] = zpad
    for r0, _, _ in _row_blocks(lc, total):
        cos = cos_ref[r0:r0 + ROW_BLOCK, :]
        sin = sin_ref[r0:r0 + ROW_BLOCK, :]
        q_s[r0:r0 + ROW_BLOCK, :] = _rope(u_ref[0, r0:r0 + ROW_BLOCK, 0:gw], cos, sin) * scale
        kr = _rope(u_ref[0, r0:r0 + ROW_BLOCK, gw:gw + SWA_KV_W], cos[:, 0:SWA_KV_W], sin[:, 0:SWA_KV_W])
        dst = r0 if r0 < lc else r0 + blk
        k_s[dst:dst + ROW_BLOCK, :] = kr
        v_s[dst:dst + ROW_BLOCK, :] = u_ref[0, r0:r0 + ROW_BLOCK, gw + SWA_KV_W:gw + 2 * SWA_KV_W]

    grp = N_HEADS // SWA_KV_HEADS

    def attend(items):
        scores = [[_dot_nt(q2, kk) for kk in keys] for q2, keys, _, _, _ in items]
        exps, dens = [], []
        for (_, _, _, masks, sink_col), sc in zip(items, scores):
            sc = [s if mk is None else jnp.where(mk, s, NEG_INF) for s, mk in zip(sc, masks)]
            mx = sink_col
            for s in sc:
                mx = jnp.maximum(mx, jnp.max(s, axis=-1, keepdims=True))
            es = [jnp.exp(s - mx) for s in sc]
            den = jnp.exp(sink_col - mx)
            for e in es:
                den = den + jnp.sum(e, axis=-1, keepdims=True)
            exps.append([e.astype(BF16) for e in es])
            dens.append(den)
        outs = []
        for (_, _, vals, _, _), es, den in zip(items, exps, dens):
            acc = None
            for e, vv in zip(es, vals):
                t = jnp.dot(e, vv.astype(BF16), preferred_element_type=F32)
                acc = t if acc is None else acc + t
            outs.append(acc / den)
        return outs

    def sink_column(hk, rows_per_head):
        row = lax.broadcasted_iota(jnp.int32, (grp * rows_per_head, 1), 0)
        col = jnp.full((grp * rows_per_head, 1), sink_ref[sink0 + hk * grp], F32)
        for g in range(1, grp):
            col = jnp.where(row >= g * rows_per_head, sink_ref[sink0 + hk * grp + g], col)
        return col

    kvs = [slice(hk * HEAD_DIM, (hk + 1) * HEAD_DIM) for hk in range(SWA_KV_HEADS)]

    def stack_q(hk, rows):
        return jnp.concatenate(
            [q_s[rows, (hk * grp + g) * HEAD_DIM:(hk * grp + g + 1) * HEAD_DIM] for g in range(grp)], axis=0)

    def unstack_o(o2s, n):
        return jnp.concatenate([o2s[hk][g * n:(g + 1) * n, :]
                                for hk in range(SWA_KV_HEADS) for g in range(grp)], axis=1)

    o2s = attend([(stack_q(hk, slice(0, lc)), [k_s[0:lc, kvs[hk]]], [v_s[0:lc, kvs[hk]]], [None],
                   sink_column(hk, lc)) for hk in range(SWA_KV_HEADS)])
    z = u_ref[0, 0:lc, gw + 2 * SWA_KV_W:2 * gw + 2 * SWA_KV_W]
    o_ref[0, 0:lc, :] = (unstack_o(o2s, lc) * _silu(z)).astype(BF16)

    qi = lax.broadcasted_iota(jnp.int32, (grp * blk, 3 * blk), 0) % blk
    kj = lax.broadcasted_iota(jnp.int32, (grp * blk, 3 * blk), 1)
    in_win = jnp.abs(kj - blk - qi) <= WINDOW

    def block_body(n, carry):
        kpos = n * blk - blk + kj
        mask = in_win & (kpos >= 0) & (kpos < t_lat)
        qrow = pl.multiple_of(lc + n * blk, blk)
        wrow = pl.multiple_of(lc + n * blk, blk)
        o2s = attend([(stack_q(hk, pl.ds(qrow, blk)),
                       [k_s[pl.ds(wrow, 3 * blk), kvs[hk]], k_s[0:lc, kvs[hk]]],
                       [v_s[pl.ds(wrow, 3 * blk), kvs[hk]], v_s[0:lc, kvs[hk]]],
                       [mask, None], sink_column(hk, blk)) for hk in range(SWA_KV_HEADS)])
        z = u_ref[0, pl.ds(qrow, blk), gw + 2 * SWA_KV_W:2 * gw + 2 * SWA_KV_W]
        o_ref[0, pl.ds(qrow, blk), :] = (unstack_o(o2s, blk) * _silu(z)).astype(BF16)
        return carry

    lax.fori_loop(0, nblk, block_body, 0)


def _swa(u, cos, sin, sink, l, *, lc):
    nb, total, _ = u.shape
    t_lat = total - lc
    full = lambda shape: pl.BlockSpec(shape, lambda b, s: (0,) * len(shape))
    grid_spec = pltpu.PrefetchScalarGridSpec(
        num_scalar_prefetch=1,
        grid=(nb,),
        in_specs=[pl.BlockSpec((1, total, W_SWA), lambda b, s: (b, 0, 0)),
                  full((total, GROUP_W)), full((total, GROUP_W))],
        out_specs=pl.BlockSpec((1, total, GROUP_W), lambda b, s: (b, 0, 0)),
        scratch_shapes=[pltpu.VMEM((total, GROUP_W), F32),
                        pltpu.VMEM((total + 2 * WINDOW, SWA_KV_W), F32),
                        pltpu.VMEM((total + 2 * WINDOW, SWA_KV_W), F32)],
    )
    return pl.pallas_call(
        functools.partial(_swa_body, lc=lc, sink0=l * N_HEADS),
        grid_spec=grid_spec,
        out_shape=jax.ShapeDtypeStruct((nb, total, GROUP_W), BF16),
        compiler_params=_cparams(("arbitrary",)),
        name="swa",
    )(sink, u, cos, sin)


def _gdn_body(u_ref, ab_ref, cw_ref, avec_ref, dtb_ref, ng_ref, o_ref,
              q_s, k_s, v_s, cs_s, gt_s, of_s, ob_s, st_s, *ring_refs, lc):
    ring = (ring_refs[0:4], ring_refs[4:8])
    total = u_ref.shape[1]
    gw = GROUP_W
    c = GDN_CHUNK
    nch = total // c
    n_ctx = lc // c
    per_blk = ROW_BLOCK // c
    ones_bd = _head_ones()

    ri = lax.broadcasted_iota(jnp.int32, (ROW_BLOCK, ROW_BLOCK), 0)
    ci_ = lax.broadcasted_iota(jnp.int32, (ROW_BLOCK, ROW_BLOCK), 1)
    same_chunk = (ri // c) == (ci_ // c)
    tri_lo = (same_chunk & (ri >= ci_)).astype(BF16)
    tri_up = (same_chunk & (ri <= ci_)).astype(BF16)
    lane = lax.broadcasted_iota(jnp.int32, (ROW_BLOCK, W_AB), 1)

    for bi, (r0, lo, hi) in enumerate(_row_blocks(lc, total)):
        h = _silu(_dwconv_rows(u_ref, r0, ROW_BLOCK, lo, hi, 0, 3 * gw, cw_ref))
        hq, hk, hv = h[:, 0:gw], h[:, gw:2 * gw], h[:, 2 * gw:3 * gw]
        ssq = _split_dot(hq * hq, ones_bd, 2)
        q_s[r0:r0 + ROW_BLOCK, :] = hq * lax.rsqrt(ssq + NORM_EPS) * (HEAD_DIM ** -0.5)
        ssk = _split_dot(hk * hk, ones_bd, 2)
        k_s[r0:r0 + ROW_BLOCK, :] = hk * lax.rsqrt(ssk + NORM_EPS)
        v_s[r0:r0 + ROW_BLOCK, :] = hv
        ab = ab_ref[0, r0:r0 + ROW_BLOCK, :]
        g = -avec_ref[...] * _softplus(ab + dtb_ref[...])
        g = jnp.where(lane < 2 * N_HEADS, g, 0.0)
        beta = _sigmoid(ab)
        cs_lo = _split_dot_left(tri_lo, g, 3)
        cs_up = _split_dot_left(tri_up, g, 3)
        cs = jnp.where(lane < N_HEADS, cs_lo, jnp.where(lane < 2 * N_HEADS, cs_up, beta))
        cs_s[r0:r0 + ROW_BLOCK, :] = cs
        cst = cs.T
        for cc in range(per_blk):
            row0 = (bi * per_blk + cc) * SUBLANES
            gt_s[row0:row0 + SUBLANES, :] = cst[0:SUBLANES, cc * c:(cc + 1) * c]

    ii = lax.broadcasted_iota(jnp.int32, (c, c), 0)
    jj = lax.broadcasted_iota(jnp.int32, (c, c), 1)
    eye = (ii == jj).astype(F32)
    n_sq = c.bit_length() - 2
    incl = ((ii >= jj), (ii <= jj))
    strict = ((ii > jj), (ii < jj))
    last = (c - 1, 0)
    heads = [slice(h * HEAD_DIM, (h + 1) * HEAD_DIM) for h in range(N_HEADS)]
    e_src = lax.broadcasted_iota(jnp.int32, (3 * W_AB, 2 * N_HEADS * 128), 0) % W_AB
    e_grp = lax.broadcasted_iota(jnp.int32, (3 * W_AB, 2 * N_HEADS * 128), 1) // 128
    expand3 = [(e_src == jnp.where(e_grp < N_HEADS, d * N_HEADS + e_grp,
                                   (2 + d) * N_HEADS + e_grp - N_HEADS)).astype(BF16)
               for d in range(2)]

    blk_rows = GDN_B1_CHUNKS * c
    n_steps = total // blk_rows
    assert lc % blk_rows == 0 and n_steps >= 2

    def block_row0(j, d):
        if d == 0:
            return j * blk_rows
        if isinstance(j, int):
            return lc - (j + 1) * blk_rows if (j + 1) * blk_rows <= lc else total + lc - (j + 1) * blk_rows
        return pl.multiple_of(jnp.where((j + 1) * blk_rows <= lc, lc - (j + 1) * blk_rows,
                                        total + lc - (j + 1) * blk_rows), blk_rows)

    def cat_heads(items):
        return jnp.concatenate(items, axis=1)

    def b1_gen(j, slot):
        uvr_s, wqr_s, attr_s, kttr_s = ring[slot]
        probs = []
        for d in range(2):
            r0 = block_row0(j, d)
            parts = []
            rem = cs_s[pl.ds(r0, blk_rows), :]
            for _ in range(3):
                hi = rem.astype(BF16)
                parts.append(hi)
                rem = rem - hi.astype(F32)
            xp = jnp.dot(jnp.concatenate(parts, axis=1), expand3[d], preferred_element_type=F32)
            for cc in range(GDN_B1_CHUNKS):
                r = r0 + cc * c
                kc = k_s[pl.ds(r, c), :]
                qc = q_s[pl.ds(r, c), :]
                vc = v_s[pl.ds(r, c), :]
                gt = gt_s[pl.ds(pl.multiple_of((r0 // c + cc) * SUBLANES, SUBLANES), SUBLANES), :]
                sel = []
                for h in range(N_HEADS):
                    kh, qh, vh = kc[:, heads[h]], qc[:, heads[h]], vc[:, heads[h]]
                    kk = _dot_nt(kh, kh)
                    qk = _dot_nt(qh, kh)
                    col = xp[cc * c:(cc + 1) * c, h * 128:h * 128 + HEAD_DIM]
                    bcol = xp[cc * c:(cc + 1) * c, (N_HEADS + h) * 128:(N_HEADS + h) * 128 + HEAD_DIM]
                    rowv = gt[d * N_HEADS + h:d * N_HEADS + h + 1, :]
                    decay = jnp.where(incl[d], jnp.exp(jnp.where(incl[d], col - rowv, 0.0)), 0.0)
                    n = -jnp.where(strict[d], kk * decay, 0.0) * bcol
                    eg = jnp.exp(col)
                    glast = col[last[d]:last[d] + 1, :]
                    pr = dict(cc=cc, d=d, a=(eye - n).astype(BF16), p=eye + n,
                              rhs=jnp.concatenate([vh * bcol, kh * (bcol * eg)], axis=1).astype(BF16),
                              att=qk * decay, qe=qh * eg, kt=kh * jnp.exp(glast - col))
                    probs.append(pr)
                    sel.append(pr)
                wqr_s[d,(2 * cc + 1) * c:(2 * cc + 2) * c, :] = (
                    cat_heads([pr['qe'] for pr in sel]).astype(BF16))
                attr_s[d,cc * c:(cc + 1) * c, :] = cat_heads([pr['att'] for pr in sel]).astype(BF16)
                ktt = cat_heads([pr['kt'] for pr in sel]).T
                kttr_s[d,cc * gw:(cc + 1) * gw, :] = ktt.astype(BF16)
        yield
        for _ in range(n_sq):
            res = [eye - jnp.dot(pr['a'], pr['p'].astype(BF16), preferred_element_type=F32)
                   for pr in probs]
            yield
            for pr, rr in zip(probs, res):
                pr['p'] = pr['p'] + _dot(pr['p'], rr)
            yield
        for pr in probs:
            pr['uw'] = _dot(pr['p'], pr['rhs'])
        yield
        for d in range(2):
            for cc in range(GDN_B1_CHUNKS):
                sel = [pr for pr in probs if pr['cc'] == cc and pr['d'] == d]
                uvr_s[d,cc * c:(cc + 1) * c, :] = cat_heads([pr['uw'][:, 0:HEAD_DIM] for pr in sel])
                wqr_s[d,2 * cc * c:(2 * cc + 1) * c, :] = (
                    cat_heads([pr['uw'][:, HEAD_DIM:2 * HEAD_DIM] for pr in sel]).astype(BF16))

    def b2_gen(j, slot):
        uvr_s, wqr_s, attr_s, kttr_s = ring[slot]
        r0s = [block_row0(j, d) for d in range(2)]
        states = [st_s[hd] for hd in range(2 * N_HEADS)]
        for step in range(GDN_B1_CHUNKS):
            probs = []
            for d in range(2):
                cc = step if d == 0 else GDN_B1_CHUNKS - 1 - step
                r = r0s[d] + cc * c
                wqm = wqr_s[d, 2 * cc * c:(2 * cc + 2) * c, :]
                uvm = uvr_s[d, cc * c:(cc + 1) * c, :]
                attm = attr_s[d, cc * c:(cc + 1) * c, :]
                grow = cs_s[pl.ds(r + last[d], 1), :]
                for h in range(N_HEADS):
                    hd = d * N_HEADS + h
                    kt = kttr_s[d, cc * gw + h * HEAD_DIM:cc * gw + (h + 1) * HEAD_DIM, :]
                    probs.append(dict(d=d, hd=hd, r=r, wqm=wqm[:, heads[h]],
                                      uv=uvm[:, heads[h]], att=attm[:, heads[h]], kt=kt,
                                      gl=jnp.exp(grow[:, hd:hd + 1])))
            for pr in probs:
                pr['wq'] = jnp.dot(pr['wqm'], states[pr['hd']].astype(BF16), preferred_element_type=F32)
            yield
            for pr in probs:
                vn = (pr['uv'] - pr['wq'][0:c, :]).astype(BF16)
                pr['o'] = pr['wq'][c:2 * c, :] + jnp.dot(pr['att'], vn, preferred_element_type=F32)
                states[pr['hd']] = (states[pr['hd']] * pr['gl']
                                    + jnp.dot(pr['kt'], vn, preferred_element_type=F32))
            for d, o_s in ((0, of_s), (1, ob_s)):
                sel = [pr for pr in probs if pr['d'] == d]
                o_s[pl.ds(sel[0]['r'], c), :] = cat_heads([pr['o'] for pr in sel])
            yield
        for hd in range(2 * N_HEADS):
            st_s[hd] = states[hd]

    def run_interleaved(gens):
        live = list(gens)
        while live:
            still = []
            for g in live:
                try:
                    next(g)
                    still.append(g)
                except StopIteration:
                    pass
            live = still

    st_s[...] = jnp.zeros(st_s.shape, F32)
    run_interleaved([b1_gen(0, 0)])

    def pipe_body(t, carry):
        j = 1 + 2 * t
        run_interleaved([b1_gen(j, 1), b2_gen(j - 1, 0)])
        run_interleaved([b1_gen(j + 1, 0), b2_gen(j, 1)])
        return carry

    lax.fori_loop(0, (n_steps - 1) // 2, pipe_body, 0)
    if (n_steps - 1) % 2:
        run_interleaved([b1_gen(n_steps - 1, (n_steps - 1) % 2), b2_gen(n_steps - 2, n_steps % 2)])
    run_interleaved([b2_gen(n_steps - 1, (n_steps - 1) % 2)])

    for r0, _, _ in _row_blocks(lc, total):
        o = of_s[r0:r0 + ROW_BLOCK, :] + ob_s[r0:r0 + ROW_BLOCK, :]
        ms = _split_dot(o * o, ones_bd, 2) * (1.0 / HEAD_DIM)
        z = u_ref[0, r0:r0 + ROW_BLOCK, 3 * gw:4 * gw]
        y = o * lax.rsqrt(ms + NORM_EPS) * ng_ref[...] * _silu(z)
        o_ref[0, r0:r0 + ROW_BLOCK, :] = y.astype(BF16)


def _gdn_params(a_log, dt_bias, norm_g):
    depth = a_log.shape[0]
    pad = ((0, 0), (0, 0), (0, W_AB - 2 * N_HEADS))
    avec = jnp.pad(jnp.exp(a_log.reshape(depth, 1, 2 * N_HEADS)), pad)
    dtb = jnp.pad(dt_bias.reshape(depth, 1, 2 * N_HEADS), pad)
    ng = jnp.tile(norm_g.reshape(depth, 1, HEAD_DIM), (1, 1, N_HEADS))
    return avec, dtb, ng


def _gdn(u, ab, conv_w, avec, dtb, ng, l, *, lc):
    nb, total, _ = u.shape
    nch = total // GDN_CHUNK
    blk = GDN_B1_CHUNKS * GDN_CHUNK
    return pl.pallas_call(
        functools.partial(_gdn_body, lc=lc),
        grid=(nb,),
        in_specs=[pl.BlockSpec((1, total, W_GDN), lambda b: (b, 0, 0)),
                  pl.BlockSpec((1, total, W_AB), lambda b: (b, 0, 0)),
                  _layer(conv_w, l), _layer(avec, l), _layer(dtb, l), _layer(ng, l)],
        out_specs=pl.BlockSpec((1, total, GROUP_W), lambda b: (b, 0, 0)),
        out_shape=jax.ShapeDtypeStruct((nb, total, GROUP_W), BF16),
        scratch_shapes=[pltpu.VMEM((total, GROUP_W), F32),
                        pltpu.VMEM((total, GROUP_W), F32),
                        pltpu.VMEM((total, GROUP_W), F32),
                        pltpu.VMEM((total, W_AB), F32),
                        pltpu.VMEM((nch * SUBLANES, GDN_CHUNK), F32),
                        pltpu.VMEM((total, GROUP_W), F32),
                        pltpu.VMEM((total, GROUP_W), F32),
                        pltpu.VMEM((2 * N_HEADS, HEAD_DIM, HEAD_DIM), F32)] + 2 * [
                        pltpu.VMEM((2, blk, GROUP_W), F32),
                        pltpu.VMEM((2, 2 * blk, GROUP_W), BF16),
                        pltpu.VMEM((2, blk, GROUP_W), BF16),
                        pltpu.VMEM((2, GDN_B1_CHUNKS * GROUP_W, GDN_CHUNK), BF16)],
        compiler_params=_cparams(("arbitrary",)),
        name="gdn",
    )(u, ab, conv_w, avec, dtb, ng)


def _pack_w_in(w_in):
    gw = GROUP_W
    o_gdn = 2 * gw
    o_ab = o_gdn + 4 * gw
    o_ret = o_ab + 4 * N_HEADS
    o_swa = o_ret + 4 * gw
    end = o_swa + W_SWA
    assert end == w_in.shape[-1]
    wb = w_in.astype(BF16)
    pad = jnp.zeros(wb.shape[:-1] + (W_AB - 4 * N_HEADS,), BF16)
    return jnp.concatenate([wb[..., 0:o_ab], wb[..., o_ret:end], wb[..., o_ab:o_ret], pad], axis=-1)


def _rope_tables(ang, lc):
    cos = jnp.cos(ang)
    sin = jnp.sin(ang)
    cos_h = jnp.concatenate([cos, cos], axis=-1)
    sin_h = jnp.concatenate([-sin, sin], axis=-1)
    cos_t = jnp.tile(cos_h, (1, N_HEADS))
    sin_t = jnp.tile(sin_h, (1, N_HEADS))
    cos_t = jnp.concatenate([jnp.ones((lc, GROUP_W), F32), cos_t], axis=0)
    sin_t = jnp.concatenate([jnp.zeros((lc, GROUP_W), F32), sin_t], axis=0)
    return cos_t, sin_t


def _rope_freqs(pos, n):
    inv = ROPE_BASE ** (-jnp.arange(0, n, 2, dtype=F32) / n)
    return pos[:, None] * inv[None, :]


def kernel(x, c, ctx, c_ctx, w_mod, b_mod, pre_norm_g, post_norm_g, w_in, w_out, lru_conv_w, lru_conv_b, lru_w_r, lru_b_r, lru_w_i, lru_b_i, lru_lambda, gdn_conv_w, gdn_a_log, gdn_dt_bias, gdn_norm_g, ret_decay_logit, swa_sink):
    nb, t, d = x.shape
    lc = ctx.shape[1]
    depth = w_mod.shape[0]
    assert t % ROW_BLOCK == 0 and lc % ROW_BLOCK == 0 and d == 4 * GROUP_W

    rows = t // GRID_W
    row = jnp.repeat(jnp.arange(rows, dtype=F32), GRID_W)
    col = jnp.tile(jnp.arange(GRID_W, dtype=F32), rows)
    ang2d = jnp.concatenate([_rope_freqs(row, HEAD_DIM // 2), _rope_freqs(col, HEAD_DIM // 2)], axis=-1)
    ang1d = _rope_freqs(jnp.arange(t, dtype=F32), HEAD_DIM)
    cos1, sin1 = _rope_tables(ang1d, lc)
    cos2, sin2 = _rope_tables(ang2d, lc)

    mod_rows = -(-(nb + 1) // SUBLANES) * SUBLANES
    s_rows = jnp.concatenate([c, c_ctx[None, :], jnp.zeros((mod_rows - nb - 1, d), F32)], axis=0)
    mod = _modulation(s_rows, w_mod, b_mod)

    w_in_p = _pack_w_in(w_in)
    w_out_b = w_out.astype(BF16)
    pre_g = pre_norm_g.reshape(depth, 1, d)
    post_g = post_norm_g.reshape(depth, 1, d)
    lru_cb, lru_wg, lru_bg, lru_lam = _lru_params(lru_conv_b, lru_w_r, lru_b_r, lru_w_i, lru_b_i, lru_lambda)
    gdn_avec, gdn_dtb, gdn_ng = _gdn_params(gdn_a_log, gdn_dt_bias, gdn_norm_g)
    ret_lg = jnp.repeat(ret_decay_logit, HEAD_DIM, axis=-1)
    sink = swa_sink.reshape(depth * N_HEADS)
    xs = jnp.concatenate([ctx, x], axis=1)
    for l in range(depth):
        u_lru, u_gdn, u_ret, u_swa, u_ab = _in_proj(xs, mod, pre_g, w_in_p, l, lc=lc)
        ya = _lru(u_lru, lru_conv_w, lru_cb, lru_wg, lru_bg, lru_lam, l, lc=lc)
        yb = _gdn(u_gdn, u_ab, gdn_conv_w, gdn_avec, gdn_dtb, gdn_ng, l, lc=lc)
        yc = _ret(u_ret, cos1, sin1, ret_lg, l, lc=lc)
        yd = _swa(u_swa, cos2, sin2, sink, l, lc=lc)
        xs = _out_proj((ya, yb, yc, yd), xs, mod, post_g, w_out_b, l, lc=lc)
    return xs[:, lc:, :]
```

```python
import functools

import jax
import jax.numpy as jnp
from jax import lax
from jax.experimental import pallas as pl
from jax.experimental.pallas import tpu as pltpu

F32 = jnp.float32
BF16 = jnp.bfloat16

HEAD_DIM = 64
GROUP_W = 256
N_HEADS = GROUP_W // HEAD_DIM
NORM_EPS = 1e-6
ROPE_BASE = 10000.0
NEG_INF = -1e30
GRID_W = 64
CONV_W = 4
LRU_C = 8.0
LRU_SCAN_UNROLL = 4
GDN_CHUNK = 64
GDN_B1_CHUNKS = 4
RET_CHUNK = 256
SWA_KV_HEADS = 2
SWA_KV_W = SWA_KV_HEADS * HEAD_DIM
WINDOW = 128
ROW_BLOCK = 256
PROJ_ROWS_MAX = 576
SUBLANES = 8
VMEM_LIMIT_BYTES = 56 * 1024 * 1024

W_LRU = 2 * GROUP_W
W_GDN = 4 * GROUP_W
W_RET = 4 * GROUP_W
W_SWA = 2 * GROUP_W + 2 * SWA_KV_W
W_AB = 128
IN_W_PACKED = W_LRU + W_GDN + W_RET + W_SWA + W_AB


def _cparams(sem, flags=None):
    return pltpu.CompilerParams(dimension_semantics=sem, vmem_limit_bytes=VMEM_LIMIT_BYTES, flags=flags)


def _dot(a, b):
    return jnp.dot(a.astype(BF16), b.astype(BF16), preferred_element_type=F32)


def _dot_nt(a, b):
    return lax.dot_general(a.astype(BF16), b.astype(BF16), (((1,), (1,)), ((), ())),
                           preferred_element_type=F32)


def _dot_tn(a, b):
    return lax.dot_general(a.astype(BF16), b.astype(BF16), (((0,), (0,)), ((), ())),
                           preferred_element_type=F32)


def _split_dot(x, w, parts):
    acc = None
    r = x
    for _ in range(parts):
        hi = r.astype(BF16)
        t = jnp.dot(hi, w, preferred_element_type=F32)
        acc = t if acc is None else acc + t
        r = r - hi.astype(F32)
    return acc


def _split_dot_left(w, x, parts):
    acc = None
    r = x
    for _ in range(parts):
        hi = r.astype(BF16)
        t = jnp.dot(w, hi, preferred_element_type=F32)
        acc = t if acc is None else acc + t
        r = r - hi.astype(F32)
    return acc


def _sigmoid(x):
    return 0.5 * jnp.tanh(0.5 * x) + 0.5


def _silu(x):
    return x * _sigmoid(x)


def _softplus(x):
    return jnp.maximum(x, 0.0) + jnp.log1p(jnp.exp(-jnp.abs(x)))


def _head_ones():
    r = lax.broadcasted_iota(jnp.int32, (GROUP_W, GROUP_W), 0) // HEAD_DIM
    c = lax.broadcasted_iota(jnp.int32, (GROUP_W, GROUP_W), 1) // HEAD_DIM
    return (r == c).astype(BF16)


def _row_blocks(lc, total):
    out = []
    for r0 in range(0, total, ROW_BLOCK):
        out.append((r0, 0, lc) if r0 < lc else (r0, lc, total))
    return out


def _mod_body(s_ref, w_ref, b_ref, o_ref):
    s = _silu(s_ref[...])
    o_ref[0] = _dot(s, w_ref[0]) + b_ref[0]


def _modulation(s_rows, w_mod, b_mod):
    depth, d, d3 = w_mod.shape
    rows = s_rows.shape[0]
    nt = d3 // d
    return pl.pallas_call(
        _mod_body,
        grid=(depth, nt),
        in_specs=[pl.BlockSpec((rows, d), lambda l, j: (0, 0)),
                  pl.BlockSpec((1, d, d), lambda l, j: (l, 0, j)),
                  pl.BlockSpec((1, 1, d), lambda l, j: (l, 0, j))],
        out_specs=pl.BlockSpec((1, rows, d), lambda l, j: (l, 0, j)),
        out_shape=jax.ShapeDtypeStruct((depth, rows, d3), F32),
        compiler_params=_cparams(("arbitrary", "arbitrary")),
        name="modulation",
    )(s_rows, w_mod, b_mod.reshape(depth, 1, d3))


def _proj_rows(total):
    for tm in range(PROJ_ROWS_MAX, SUBLANES - 1, -SUBLANES):
        if total % tm == 0:
            return tm
    raise ValueError(total)


def _mod_rows(mod_ref, b, i, tm, lc, nb, c0, c1):
    lat = mod_ref[pl.ds(b, 1), c0:c1]
    if lc % tm == 0:
        ctx = mod_ref[nb:nb + 1, c0:c1]
        return jnp.where(i * tm < lc, ctx, lat)
    ctx = mod_ref[nb:nb + 1, c0:c1]
    row = i * tm + lax.broadcasted_iota(jnp.int32, (tm, 1), 0)
    return jnp.where(row < lc, ctx, lat)


def _inproj_body(x_ref, xp_ref, xn_ref, mod_ref, g_ref, w_ref, cos1_ref, sin1_ref, cos2_ref, sin2_ref,
                 lcw_ref, lcb_ref, gcw_ref, o_lru, o_gdn, o_ret, o_swa, o_ab, *, lc, nb, total):
    i = pl.program_id(0)
    b = pl.program_id(1)
    d = x_ref.shape[2]
    tm = x_ref.shape[1]
    gw = GROUP_W
    scale = HEAD_DIM ** -0.5

    def prenorm(xv, m):
        ms = jnp.mean(xv * xv, axis=-1, keepdims=True)
        y = xv * lax.rsqrt(ms + NORM_EPS) * g_ref[...]
        return (y * (1.0 + m[:, d:2 * d]) + m[:, 0:d]).astype(BF16)

    h = prenorm(x_ref[0], _mod_rows(mod_ref, b, i, tm, lc, nb, 0, 2 * d))
    halo = prenorm(jnp.concatenate([xp_ref[0], xn_ref[0]], axis=0), mod_ref[pl.ds(b, 1), 0:2 * d])
    h_all = jnp.concatenate([h, halo], axis=0)

    def proj(c0, width, lhs=h):
        return jnp.dot(lhs, w_ref[:, c0:c0 + width], preferred_element_type=F32)

    nt = total // tm
    b_tile, b_row = lc // tm, lc % tm
    assert b_row % SUBLANES == 0 and (b_row == 0 or SUBLANES <= b_row <= tm - SUBLANES)
    prev_ok = i > 0
    next_ok = i < nt - 1
    if b_row == 0:
        prev_ok = prev_ok & (i != b_tile)
        next_ok = next_ok & (i != b_tile - 1)
    prev_f = jnp.where(prev_ok, 1.0, 0.0)
    next_f = jnp.where(next_ok, 1.0, 0.0)

    def taps(win, n, cw_ref, c0, c1, mask_row0=None):
        acc = None
        for k in range(CONV_W):
            off = k - 2
            tap = win[SUBLANES + off:SUBLANES + off + n]
            if mask_row0 is not None and off != 0:
                r = mask_row0 + off + lax.broadcasted_iota(jnp.int32, (n, 1), 0)
                same_seg = (r >= b_row) == (r - off >= b_row)
                tap = jnp.where(same_seg, tap, 0.0)
            term = tap * cw_ref[k:k + 1, c0:c1]
            acc = term if acc is None else acc + term
        return acc

    def dwconv(u_all, cw_ref, c0, c1):
        win = jnp.concatenate([u_all[tm:tm + SUBLANES] * prev_f, u_all[0:tm],
                               u_all[tm + SUBLANES:tm + 2 * SUBLANES] * next_f], axis=0)
        acc = taps(win, tm, cw_ref, c0, c1)
        if b_row:
            lo = b_row - SUBLANES
            fixed = taps(win[lo:lo + 4 * SUBLANES], 2 * SUBLANES, cw_ref, c0, c1, mask_row0=lo)
            mid = jnp.where(i == b_tile, fixed, acc[lo:lo + 2 * SUBLANES])
            acc = jnp.concatenate([acc[0:lo], mid, acc[lo + 2 * SUBLANES:tm]], axis=0)
        return acc

    o_lru[0, :, 0:gw] = dwconv(proj(0, gw, h_all), lcw_ref, 0, gw) + lcb_ref[...]
    o_lru[0, :, gw:2 * gw] = _silu(proj(gw, gw))
    c0 = W_LRU
    for p in range(3):
        o_gdn[0, :, p * gw:(p + 1) * gw] = _silu(
            dwconv(proj(c0 + p * gw, gw, h_all), gcw_ref, p * gw, (p + 1) * gw))
    o_gdn[0, :, 3 * gw:4 * gw] = _silu(proj(c0 + 3 * gw, gw))
    c0 += W_GDN
    u = proj(c0, W_RET)
    cos, sin = cos1_ref[...], sin1_ref[...]
    o_ret[0, :, 0:gw] = _rope(u[:, 0:gw], cos, sin)
    o_ret[0, :, gw:2 * gw] = _rope(u[:, gw:2 * gw], cos, sin) * scale
    o_ret[0, :, 2 * gw:3 * gw] = u[:, 2 * gw:3 * gw]
    o_ret[0, :, 3 * gw:4 * gw] = _silu(u[:, 3 * gw:4 * gw])
    c0 += W_RET
    u = proj(c0, W_SWA)
    cos, sin = cos2_ref[...], sin2_ref[...]
    kvw = SWA_KV_W
    o_swa[0, :, 0:gw] = _rope(u[:, 0:gw], cos, sin) * scale
    o_swa[0, :, gw:gw + kvw] = _rope(u[:, gw:gw + kvw], cos[:, 0:kvw], sin[:, 0:kvw])
    o_swa[0, :, gw + kvw:gw + 2 * kvw] = u[:, gw + kvw:gw + 2 * kvw]
    o_swa[0, :, gw + 2 * kvw:2 * gw + 2 * kvw] = _silu(u[:, gw + 2 * kvw:2 * gw + 2 * kvw])
    c0 += W_SWA
    o_ab[0] = proj(c0, W_AB)


def _in_proj(x, mod, g, w, tables, conv_params, l, *, lc):
    nb, total, d = x.shape
    tm = _proj_rows(total)
    nt = total // tm
    per = tm // SUBLANES
    last = total // SUBLANES - 1
    widths = (W_LRU, W_GDN, W_RET, W_SWA, W_AB)
    tspec = pl.BlockSpec((tm, GROUP_W), lambda i, b: (i, 0))
    return pl.pallas_call(
        functools.partial(_inproj_body, lc=lc, nb=nb, total=total),
        grid=(nt, nb),
        in_specs=[pl.BlockSpec((1, tm, d), lambda i, b: (b, i, 0)),
                  pl.BlockSpec((1, SUBLANES, d), lambda i, b: (b, jnp.maximum(i * per - 1, 0), 0)),
                  pl.BlockSpec((1, SUBLANES, d), lambda i, b: (b, jnp.minimum((i + 1) * per, last), 0)),
                  _layer(mod, l), _layer(g, l), _layer(w, l), tspec, tspec, tspec, tspec]
                 + [_layer(p, l) for p in conv_params],
        out_specs=[pl.BlockSpec((1, tm, wd), lambda i, b: (b, i, 0)) for wd in widths],
        out_shape=[jax.ShapeDtypeStruct((nb, total, wd), F32) for wd in widths],
        compiler_params=_cparams(("arbitrary", "arbitrary")),
        name="in_proj",
    )(x, x, x, mod, g, w, *tables, *conv_params)


def _outproj_body(ya, yb, yc, yd, x_ref, mod_ref, g_ref, w_ref, o_ref, *, lc, nb):
    b = pl.program_id(0)
    i = pl.program_id(1)
    d = x_ref.shape[2]
    acc = None
    for k, y_ref in enumerate((ya, yb, yc, yd)):
        t = jnp.dot(y_ref[0], w_ref[k * GROUP_W:(k + 1) * GROUP_W, :], preferred_element_type=F32)
        acc = t if acc is None else acc + t
    ms = jnp.mean(acc * acc, axis=-1, keepdims=True)
    yn = acc * lax.rsqrt(ms + NORM_EPS) * g_ref[...]
    gate = _mod_rows(mod_ref, b, i, acc.shape[0], lc, nb, 2 * d, 3 * d)
    o_ref[0] = x_ref[0] + gate * yn


def _out_proj(ys, x, mod, g, w, l, *, lc):
    nb, total, d = x.shape
    tm = _proj_rows(total)
    nt = total // tm
    yspec = pl.BlockSpec((1, tm, GROUP_W), lambda b, i: (b, i, 0))
    return pl.pallas_call(
        functools.partial(_outproj_body, lc=lc, nb=nb),
        grid=(nb, nt),
        in_specs=[yspec, yspec, yspec, yspec,
                  pl.BlockSpec((1, tm, d), lambda b, i: (b, i, 0)),
                  _layer(mod, l), _layer(g, l), _layer(w, l)],
        out_specs=pl.BlockSpec((1, tm, d), lambda b, i: (b, i, 0)),
        out_shape=jax.ShapeDtypeStruct(x.shape, F32),
        compiler_params=_cparams(("arbitrary", "arbitrary")),
        name="out_proj",
    )(*ys, x, mod, g, w)


def _lru_scan(a_s, b_s, h_s, tile_lo, n_tiles, carry, *, rev, accumulate):
    row = lax.broadcasted_iota(jnp.int32, (SUBLANES, GROUP_W), 0)

    def body(j, carry):
        t = tile_lo + (n_tiles - 1 - j if rev else j)
        r = pl.multiple_of(t * SUBLANES, SUBLANES)
        a = a_s[pl.ds(r, SUBLANES), :]
        b = b_s[pl.ds(r, SUBLANES), :]
        for s in (1, 2, 4):
            if rev:
                ra = pltpu.roll(a, SUBLANES - s, 0)
                rb = pltpu.roll(b, SUBLANES - s, 0)
                m = row < SUBLANES - s
            else:
                ra = pltpu.roll(a, s, 0)
                rb = pltpu.roll(b, s, 0)
                m = row >= s
            b = a * jnp.where(m, rb, 0.0) + b
            a = a * jnp.where(m, ra, 1.0)
        h = a * carry + b
        if accumulate:
            h_s[pl.ds(r, SUBLANES), :] = h_s[pl.ds(r, SUBLANES), :] + h
        else:
            h_s[pl.ds(r, SUBLANES), :] = h
        return h[0:1, :] if rev else h[SUBLANES - 1:SUBLANES, :]

    return lax.fori_loop(0, n_tiles, body, carry, unroll=LRU_SCAN_UNROLL)


def _lru_body(u_ref, wg_ref, bg_ref, lam_ref, o_ref, a_s, b_s, h_s, *, lc):
    total = u_ref.shape[1]
    blocks = _row_blocks(lc, total)
    zero = jnp.zeros((1, GROUP_W), F32)
    for d in range(2):
        sp = _softplus(-lam_ref[d])
        for r0, _, _ in blocks:
            uc = u_ref[0, r0:r0 + ROW_BLOCK, 0:GROUP_W]
            gts = _dot(uc, wg_ref[d]) + bg_ref[d]
            r = _sigmoid(gts[:, 0:GROUP_W])
            ig = _sigmoid(gts[:, GROUP_W:2 * GROUP_W])
            a = jnp.exp(-LRU_C * r * sp)
            a_s[r0:r0 + ROW_BLOCK, :] = a
            b_s[r0:r0 + ROW_BLOCK, :] = jnp.sqrt(1.0 - a * a) * (ig * uc)
        ct, tt = lc // SUBLANES, total // SUBLANES
        if d == 0:
            _lru_scan(a_s, b_s, h_s, 0, tt, zero, rev=False, accumulate=False)
        else:
            carry = _lru_scan(a_s, b_s, h_s, 0, ct, zero, rev=True, accumulate=True)
            _lru_scan(a_s, b_s, h_s, ct, tt - ct, carry, rev=True, accumulate=True)
    for r0, _, _ in blocks:
        gate = u_ref[0, r0:r0 + ROW_BLOCK, GROUP_W:2 * GROUP_W]
        o_ref[0, r0:r0 + ROW_BLOCK, :] = (h_s[r0:r0 + ROW_BLOCK, :] * gate).astype(BF16)


def _block_diag(w):
    n, c = w.shape[-3], w.shape[-1]
    eye = jnp.eye(n, dtype=w.dtype)
    return (eye[:, None, :, None] * w[..., :, :, None, :]).reshape(w.shape[:-3] + (n * c, n * c))


def _layer(arr, l):
    shape = arr.shape[1:]
    return pl.BlockSpec((None,) + shape, lambda *_: (l,) + (0,) * len(shape))


def _lru_params(conv_b, w_r, b_r, w_i, b_i, lam):
    depth = conv_b.shape[0]
    wg = jnp.concatenate([_block_diag(w_r), _block_diag(w_i)], axis=-1).astype(BF16)
    bg = jnp.concatenate([b_r, b_i], axis=-1).reshape(depth, 2, 1, 2 * GROUP_W)
    return conv_b.reshape(depth, 1, GROUP_W), wg, bg, lam.reshape(depth, 2, 1, GROUP_W)


def _lru(u, wg, bg, lam, l, *, lc):
    nb, total, _ = u.shape
    return pl.pallas_call(
        functools.partial(_lru_body, lc=lc),
        grid=(nb,),
        in_specs=[pl.BlockSpec((1, total, W_LRU), lambda b: (b, 0, 0)),
                  _layer(wg, l), _layer(bg, l), _layer(lam, l)],
        out_specs=pl.BlockSpec((1, total, GROUP_W), lambda b: (b, 0, 0)),
        out_shape=jax.ShapeDtypeStruct((nb, total, GROUP_W), BF16),
        scratch_shapes=[pltpu.VMEM((total, GROUP_W), F32) for _ in range(3)],
        compiler_params=_cparams(("arbitrary",)),
        name="lru",
    )(u, wg, bg, lam)


def _rope(x, cos, sin_signed):
    half = HEAD_DIM // 2
    outs = []
    for c0 in range(0, x.shape[1], 128):
        xs = x[:, c0:c0 + 128]
        lane = lax.broadcasted_iota(jnp.int32, xs.shape, 1)
        swapped = jnp.where((lane % HEAD_DIM) < half,
                            pltpu.roll(xs, 128 - half, 1), pltpu.roll(xs, half, 1))
        outs.append(xs * cos[:, c0:c0 + 128] + swapped * sin_signed[:, c0:c0 + 128])
    return outs[0] if len(outs) == 1 else jnp.concatenate(outs, axis=1)


def _ret_body(u_ref, lg_ref, o_ref, ds_s, st_s, m_s, *, lc):
    total = u_ref.shape[1]
    c = RET_CHUNK
    nch = total // c
    n_ctx = lc // c
    gw = GROUP_W
    ones_bd = _head_ones()
    bd_mask = ones_bd > 0
    lg = -_softplus(-lg_ref[...])
    lgf, lgb = lg[0:1, :], lg[1:2, :]
    pos = lax.broadcasted_iota(jnp.int32, (c, 1), 0).astype(F32)
    qdec_f = jnp.exp((pos + 1.0) * lgf)
    qdec_b = jnp.exp((c - pos) * lgb)
    kdec_f = jnp.exp((c - 1.0 - pos) * lgf)
    kdec_b = jnp.exp(pos * lgb)
    cdec_f = jnp.exp(float(c) * lgf)
    cdec_b = jnp.exp(float(c) * lgb)
    dij = (lax.broadcasted_iota(jnp.int32, (c, c), 0)
           - lax.broadcasted_iota(jnp.int32, (c, c), 1)).astype(F32)
    for h in range(N_HEADS):
        lf = lgf[:, h * HEAD_DIM:h * HEAD_DIM + 1]
        lb = lgb[:, h * HEAD_DIM:h * HEAD_DIM + 1]
        fwd = jnp.exp(jnp.maximum(dij, 0.0) * lf)
        bwd = jnp.exp(jnp.maximum(-dij, 0.0) * lb)
        m_s[h] = jnp.where(dij > 0, fwd, jnp.where(dij < 0, bwd, 2.0))

    for ci in range(nch):
        r0 = ci * c
        kr = u_ref[0, r0:r0 + c, gw:2 * gw]
        v = u_ref[0, r0:r0 + c, 2 * gw:3 * gw]
        ds_s[0, ci] = jnp.where(bd_mask, _dot_tn(kr * kdec_f, v), 0.0)
        ds_s[1, ci] = jnp.where(bd_mask, _dot_tn(kr * kdec_b, v), 0.0)

    s = jnp.zeros((gw, gw), F32)
    for ci in range(nch):
        st_s[0, ci] = s
        s = s * cdec_f + ds_s[0, ci]
    s = jnp.zeros((gw, gw), F32)
    for ci in list(range(n_ctx - 1, -1, -1)) + list(range(nch - 1, n_ctx - 1, -1)):
        st_s[1, ci] = s
        s = s * cdec_b + ds_s[1, ci]

    for ci in range(nch):
        r0 = ci * c
        qr = u_ref[0, r0:r0 + c, 0:gw]
        kr = u_ref[0, r0:r0 + c, gw:2 * gw]
        v = u_ref[0, r0:r0 + c, 2 * gw:3 * gw]
        gate = u_ref[0, r0:r0 + c, 3 * gw:4 * gw]
        hss = [slice(h * HEAD_DIM, (h + 1) * HEAD_DIM) for h in range(N_HEADS)]
        inter = _dot(qr * qdec_f, st_s[0, ci]) + _dot(qr * qdec_b, st_s[1, ci])
        atts = [_dot_nt(qr[:, hs], kr[:, hs]) for hs in hss]
        atts = [(a * m_s[h]).astype(BF16) for h, a in enumerate(atts)]
        o = inter + jnp.concatenate(
            [jnp.dot(a, v[:, hs].astype(BF16), preferred_element_type=F32) for a, hs in zip(atts, hss)],
            axis=1)
        mu = _split_dot(o, ones_bd, 2) * (1.0 / HEAD_DIM)
        dlt = o - mu
        var = _split_dot(dlt * dlt, ones_bd, 2) * (1.0 / HEAD_DIM)
        y = dlt * lax.rsqrt(var + NORM_EPS) * gate
        o_ref[0, r0:r0 + c, :] = y.astype(BF16)


def _ret(u, lg, l, *, lc):
    nb, total, _ = u.shape
    nch = total // RET_CHUNK
    return pl.pallas_call(
        functools.partial(_ret_body, lc=lc),
        grid=(nb,),
        in_specs=[pl.BlockSpec((1, total, W_RET), lambda b: (b, 0, 0)), _layer(lg, l)],
        out_specs=pl.BlockSpec((1, total, GROUP_W), lambda b: (b, 0, 0)),
        out_shape=jax.ShapeDtypeStruct((nb, total, GROUP_W), BF16),
        scratch_shapes=[pltpu.VMEM((2, nch, GROUP_W, GROUP_W), F32),
                        pltpu.VMEM((2, nch, GROUP_W, GROUP_W), F32),
                        pltpu.VMEM((N_HEADS, RET_CHUNK, RET_CHUNK), F32)],
        compiler_params=_cparams(("arbitrary",)),
        name="retention",
    )(u, lg)


def _swa_body(sink_ref, u_ref, o_ref, k_s, v_s, *, lc, sink0):
    total = u_ref.shape[1]
    t_lat = total - lc
    gw = GROUP_W
    blk = WINDOW
    nblk = t_lat // blk
    kv0 = lc + blk
    zpad = jnp.zeros((blk, SWA_KV_W), BF16)
    for s_ref in (k_s, v_s):
        s_ref[lc:lc + blk, :] = zpad
        s_ref[kv0 + t_lat:kv0 + t_lat + blk, :] = zpad
    for r0, _, _ in _row_blocks(lc, total):
        dst = r0 if r0 < lc else r0 + blk
        k_s[dst:dst + ROW_BLOCK, :] = u_ref[0, r0:r0 + ROW_BLOCK, gw:gw + SWA_KV_W].astype(BF16)
        v_s[dst:dst + ROW_BLOCK, :] = (
            u_ref[0, r0:r0 + ROW_BLOCK, gw + SWA_KV_W:gw + 2 * SWA_KV_W].astype(BF16))

    grp = N_HEADS // SWA_KV_HEADS

    def attend(items):
        scores = [[_dot_nt(q2, kk) for kk in keys] for q2, keys, _, _, _ in items]
        exps, dens = [], []
        for (_, _, _, masks, sink_col), sc in zip(items, scores):
            sc = [s if mk is None else jnp.where(mk, s, NEG_INF) for s, mk in zip(sc, masks)]
            mx = sink_col
            for s in sc:
                mx = jnp.maximum(mx, jnp.max(s, axis=-1, keepdims=True))
            es = [jnp.exp(s - mx) for s in sc]
            den = jnp.exp(sink_col - mx)
            for e in es:
                den = den + jnp.sum(e, axis=-1, keepdims=True)
            exps.append([e.astype(BF16) for e in es])
            dens.append(den)
        outs = []
        for (_, _, vals, _, _), es, den in zip(items, exps, dens):
            acc = None
            for e, vv in zip(es, vals):
                t = jnp.dot(e, vv.astype(BF16), preferred_element_type=F32)
                acc = t if acc is None else acc + t
            outs.append(acc / den)
        return outs

    def sink_column(hk, rows_per_head):
        row = lax.broadcasted_iota(jnp.int32, (grp * rows_per_head, 1), 0)
        col = jnp.full((grp * rows_per_head, 1), sink_ref[sink0 + hk * grp], F32)
        for g in range(1, grp):
            col = jnp.where(row >= g * rows_per_head, sink_ref[sink0 + hk * grp + g], col)
        return col

    kvs = [slice(hk * HEAD_DIM, (hk + 1) * HEAD_DIM) for hk in range(SWA_KV_HEADS)]

    def stack_q(hk, rows):
        return jnp.concatenate(
            [u_ref[0, rows, (hk * grp + g) * HEAD_DIM:(hk * grp + g + 1) * HEAD_DIM] for g in range(grp)],
            axis=0)

    def unstack_o(o2s, n):
        return jnp.concatenate([o2s[hk][g * n:(g + 1) * n, :]
                                for hk in range(SWA_KV_HEADS) for g in range(grp)], axis=1)

    o2s = attend([(stack_q(hk, slice(0, lc)), [k_s[0:lc, kvs[hk]]], [v_s[0:lc, kvs[hk]]], [None],
                   sink_column(hk, lc)) for hk in range(SWA_KV_HEADS)])
    gate = u_ref[0, 0:lc, gw + 2 * SWA_KV_W:2 * gw + 2 * SWA_KV_W]
    o_ref[0, 0:lc, :] = (unstack_o(o2s, lc) * gate).astype(BF16)

    qi = lax.broadcasted_iota(jnp.int32, (grp * blk, 3 * blk), 0) % blk
    kj = lax.broadcasted_iota(jnp.int32, (grp * blk, 3 * blk), 1)
    in_win = jnp.abs(kj - blk - qi) <= WINDOW

    def block_body(n, carry):
        kpos = n * blk - blk + kj
        mask = in_win & (kpos >= 0) & (kpos < t_lat)
        qrow = pl.multiple_of(lc + n * blk, blk)
        wrow = pl.multiple_of(lc + n * blk, blk)
        o2s = attend([(stack_q(hk, pl.ds(qrow, blk)),
                       [k_s[pl.ds(wrow, 3 * blk), kvs[hk]], k_s[0:lc, kvs[hk]]],
                       [v_s[pl.ds(wrow, 3 * blk), kvs[hk]], v_s[0:lc, kvs[hk]]],
                       [mask, None], sink_column(hk, blk)) for hk in range(SWA_KV_HEADS)])
        gate = u_ref[0, pl.ds(qrow, blk), gw + 2 * SWA_KV_W:2 * gw + 2 * SWA_KV_W]
        o_ref[0, pl.ds(qrow, blk), :] = (unstack_o(o2s, blk) * gate).astype(BF16)
        return carry

    lax.fori_loop(0, nblk, block_body, 0)


def _swa(u, sink, l, *, lc):
    nb, total, _ = u.shape
    grid_spec = pltpu.PrefetchScalarGridSpec(
        num_scalar_prefetch=1,
        grid=(nb,),
        in_specs=[pl.BlockSpec((1, total, W_SWA), lambda b, s: (b, 0, 0))],
        out_specs=pl.BlockSpec((1, total, GROUP_W), lambda b, s: (b, 0, 0)),
        scratch_shapes=[pltpu.VMEM((total + 2 * WINDOW, SWA_KV_W), BF16),
                        pltpu.VMEM((total + 2 * WINDOW, SWA_KV_W), BF16)],
    )
    return pl.pallas_call(
        functools.partial(_swa_body, lc=lc, sink0=l * N_HEADS),
        grid_spec=grid_spec,
        out_shape=jax.ShapeDtypeStruct((nb, total, GROUP_W), BF16),
        compiler_params=_cparams(("arbitrary",)),
        name="swa",
    )(sink, u)


def _gdn_body(u_ref, ab_ref, avec_ref, dtb_ref, ng_ref, o_ref,
              q_s, k_s, v_s, cs_s, gt_s, of_s, ob_s, st_s, *ring_refs, lc):
    ring = (ring_refs[0:4], ring_refs[4:8])
    total = u_ref.shape[1]
    gw = GROUP_W
    c = GDN_CHUNK
    nch = total // c
    n_ctx = lc // c
    per_blk = ROW_BLOCK // c
    ones_bd = _head_ones()

    ri = lax.broadcasted_iota(jnp.int32, (ROW_BLOCK, ROW_BLOCK), 0)
    ci_ = lax.broadcasted_iota(jnp.int32, (ROW_BLOCK, ROW_BLOCK), 1)
    same_chunk = (ri // c) == (ci_ // c)
    tri_lo = (same_chunk & (ri >= ci_)).astype(BF16)
    tri_up = (same_chunk & (ri <= ci_)).astype(BF16)
    lane = lax.broadcasted_iota(jnp.int32, (ROW_BLOCK, W_AB), 1)

    for bi, (r0, lo, hi) in enumerate(_row_blocks(lc, total)):
        h = u_ref[0, r0:r0 + ROW_BLOCK, 0:3 * gw]
        hq, hk, hv = h[:, 0:gw], h[:, gw:2 * gw], h[:, 2 * gw:3 * gw]
        ssq = _split_dot(hq * hq, ones_bd, 2)
        q_s[r0:r0 + ROW_BLOCK, :] = hq * lax.rsqrt(ssq + NORM_EPS) * (HEAD_DIM ** -0.5)
        ssk = _split_dot(hk * hk, ones_bd, 2)
        k_s[r0:r0 + ROW_BLOCK, :] = hk * lax.rsqrt(ssk + NORM_EPS)
        v_s[r0:r0 + ROW_BLOCK, :] = hv
        ab = ab_ref[0, r0:r0 + ROW_BLOCK, :]
        g = -avec_ref[...] * _softplus(ab + dtb_ref[...])
        g = jnp.where(lane < 2 * N_HEADS, g, 0.0)
        beta = _sigmoid(ab)
        cs_lo = _split_dot_left(tri_lo, g, 3)
        cs_up = _split_dot_left(tri_up, g, 3)
        cs = jnp.where(lane < N_HEADS, cs_lo, jnp.where(lane < 2 * N_HEADS, cs_up, beta))
        cs_s[r0:r0 + ROW_BLOCK, :] = cs
        cst = cs.T
        for cc in range(per_blk):
            row0 = (bi * per_blk + cc) * SUBLANES
            gt_s[row0:row0 + SUBLANES, :] = cst[0:SUBLANES, cc * c:(cc + 1) * c]

    ii = lax.broadcasted_iota(jnp.int32, (c, c), 0)
    jj = lax.broadcasted_iota(jnp.int32, (c, c), 1)
    eye = (ii == jj).astype(F32)
    n_sq = c.bit_length() - 2
    incl = ((ii >= jj), (ii <= jj))
    strict = ((ii > jj), (ii < jj))
    last = (c - 1, 0)
    heads = [slice(h * HEAD_DIM, (h + 1) * HEAD_DIM) for h in range(N_HEADS)]
    e_src = lax.broadcasted_iota(jnp.int32, (3 * W_AB, 2 * N_HEADS * 128), 0) % W_AB
    e_grp = lax.broadcasted_iota(jnp.int32, (3 * W_AB, 2 * N_HEADS * 128), 1) // 128
    expand3 = [(e_src == jnp.where(e_grp < N_HEADS, d * N_HEADS + e_grp,
                                   (2 + d) * N_HEADS + e_grp - N_HEADS)).astype(BF16)
               for d in range(2)]

    blk_rows = GDN_B1_CHUNKS * c
    n_steps = total // blk_rows
    assert lc % blk_rows == 0 and n_steps >= 2

    def block_row0(j, d):
        if d == 0:
            return j * blk_rows
        if isinstance(j, int):
            return lc - (j + 1) * blk_rows if (j + 1) * blk_rows <= lc else total + lc - (j + 1) * blk_rows
        return pl.multiple_of(jnp.where((j + 1) * blk_rows <= lc, lc - (j + 1) * blk_rows,
                                        total + lc - (j + 1) * blk_rows), blk_rows)

    def cat_heads(items):
        return jnp.concatenate(items, axis=1)

    def b1_gen(j, slot):
        uvr_s, wqr_s, attr_s, kttr_s = ring[slot]
        probs = []
        for d in range(2):
            r0 = block_row0(j, d)
            parts = []
            rem = cs_s[pl.ds(r0, blk_rows), :]
            for _ in range(3):
                hi = rem.astype(BF16)
                parts.append(hi)
                rem = rem - hi.astype(F32)
            xp = jnp.dot(jnp.concatenate(parts, axis=1), expand3[d], preferred_element_type=F32)
            for cc in range(GDN_B1_CHUNKS):
                r = r0 + cc * c
                kc = k_s[pl.ds(r, c), :]
                qc = q_s[pl.ds(r, c), :]
                vc = v_s[pl.ds(r, c), :]
                gt = gt_s[pl.ds(pl.multiple_of((r0 // c + cc) * SUBLANES, SUBLANES), SUBLANES), :]
                sel = []
                for h in range(N_HEADS):
                    kh, qh, vh = kc[:, heads[h]], qc[:, heads[h]], vc[:, heads[h]]
                    kk = _dot_nt(kh, kh)
                    qk = _dot_nt(qh, kh)
                    col = xp[cc * c:(cc + 1) * c, h * 128:h * 128 + HEAD_DIM]
                    bcol = xp[cc * c:(cc + 1) * c, (N_HEADS + h) * 128:(N_HEADS + h) * 128 + HEAD_DIM]
                    rowv = gt[d * N_HEADS + h:d * N_HEADS + h + 1, :]
                    decay = jnp.where(incl[d], jnp.exp(jnp.where(incl[d], col - rowv, 0.0)), 0.0)
                    n = -jnp.where(strict[d], kk * decay, 0.0) * bcol
                    eg = jnp.exp(col)
                    glast = col[last[d]:last[d] + 1, :]
                    pr = dict(cc=cc, d=d, a=(eye - n).astype(BF16), p=eye + n,
                              rhs=jnp.concatenate([vh * bcol, kh * (bcol * eg)], axis=1).astype(BF16),
                              att=qk * decay, qe=qh * eg, kt=kh * jnp.exp(glast - col))
                    probs.append(pr)
                    sel.append(pr)
                wqr_s[d,(2 * cc + 1) * c:(2 * cc + 2) * c, :] = (
                    cat_heads([pr['qe'] for pr in sel]).astype(BF16))
                attr_s[d,cc * c:(cc + 1) * c, :] = cat_heads([pr['att'] for pr in sel]).astype(BF16)
                ktt = cat_heads([pr['kt'] for pr in sel]).T
                kttr_s[d,cc * gw:(cc + 1) * gw, :] = ktt.astype(BF16)
        yield
        for _ in range(n_sq):
            res = [eye - jnp.dot(pr['a'], pr['p'].astype(BF16), preferred_element_type=F32)
                   for pr in probs]
            yield
            for pr, rr in zip(probs, res):
                pr['p'] = pr['p'] + _dot(pr['p'], rr)
            yield
        for pr in probs:
            pr['uw'] = _dot(pr['p'], pr['rhs'])
        yield
        for d in range(2):
            for cc in range(GDN_B1_CHUNKS):
                sel = [pr for pr in probs if pr['cc'] == cc and pr['d'] == d]
                uvr_s[d,cc * c:(cc + 1) * c, :] = cat_heads([pr['uw'][:, 0:HEAD_DIM] for pr in sel])
                wqr_s[d,2 * cc * c:(2 * cc + 1) * c, :] = (
                    cat_heads([pr['uw'][:, HEAD_DIM:2 * HEAD_DIM] for pr in sel]).astype(BF16))

    def b2_gen(j, slot):
        uvr_s, wqr_s, attr_s, kttr_s = ring[slot]
        r0s = [block_row0(j, d) for d in range(2)]
        states = [st_s[hd] for hd in range(2 * N_HEADS)]
        for step in range(GDN_B1_CHUNKS):
            probs = []
            for d in range(2):
                cc = step if d == 0 else GDN_B1_CHUNKS - 1 - step
                r = r0s[d] + cc * c
                wqm = wqr_s[d, 2 * cc * c:(2 * cc + 2) * c, :]
                uvm = uvr_s[d, cc * c:(cc + 1) * c, :]
                attm = attr_s[d, cc * c:(cc + 1) * c, :]
                grow = cs_s[pl.ds(r + last[d], 1), :]
                for h in range(N_HEADS):
                    hd = d * N_HEADS + h
                    kt = kttr_s[d, cc * gw + h * HEAD_DIM:cc * gw + (h + 1) * HEAD_DIM, :]
                    probs.append(dict(d=d, hd=hd, r=r, wqm=wqm[:, heads[h]],
                                      uv=uvm[:, heads[h]], att=attm[:, heads[h]], kt=kt,
                                      gl=jnp.exp(grow[:, hd:hd + 1])))
            for pr in probs:
                pr['wq'] = jnp.dot(pr['wqm'], states[pr['hd']].astype(BF16), preferred_element_type=F32)
            yield
            for pr in probs:
                vn = (pr['uv'] - pr['wq'][0:c, :]).astype(BF16)
                pr['o'] = pr['wq'][c:2 * c, :] + jnp.dot(pr['att'], vn, preferred_element_type=F32)
                states[pr['hd']] = (states[pr['hd']] * pr['gl']
                                    + jnp.dot(pr['kt'], vn, preferred_element_type=F32))
            for d, o_s in ((0, of_s), (1, ob_s)):
                sel = [pr for pr in probs if pr['d'] == d]
                o_s[pl.ds(sel[0]['r'], c), :] = cat_heads([pr['o'] for pr in sel])
            yield
        for hd in range(2 * N_HEADS):
            st_s[hd] = states[hd]

    def run_interleaved(gens):
        live = list(gens)
        while live:
            still = []
            for g in live:
                try:
                    next(g)
                    still.append(g)
                except StopIteration:
                    pass
            live = still

    st_s[...] = jnp.zeros(st_s.shape, F32)
    run_interleaved([b1_gen(0, 0)])

    def pipe_body(t, carry):
        j = 1 + 2 * t
        run_interleaved([b1_gen(j, 1), b2_gen(j - 1, 0)])
        run_interleaved([b1_gen(j + 1, 0), b2_gen(j, 1)])
        return carry

    lax.fori_loop(0, (n_steps - 1) // 2, pipe_body, 0)
    if (n_steps - 1) % 2:
        run_interleaved([b1_gen(n_steps - 1, (n_steps - 1) % 2), b2_gen(n_steps - 2, n_steps % 2)])
    run_interleaved([b2_gen(n_steps - 1, (n_steps - 1) % 2)])

    for r0, _, _ in _row_blocks(lc, total):
        o = of_s[r0:r0 + ROW_BLOCK, :] + ob_s[r0:r0 + ROW_BLOCK, :]
        ms = _split_dot(o * o, ones_bd, 2) * (1.0 / HEAD_DIM)
        gate = u_ref[0, r0:r0 + ROW_BLOCK, 3 * gw:4 * gw]
        y = o * lax.rsqrt(ms + NORM_EPS) * ng_ref[...] * gate
        o_ref[0, r0:r0 + ROW_BLOCK, :] = y.astype(BF16)


def _gdn_params(a_log, dt_bias, norm_g):
    depth = a_log.shape[0]
    pad = ((0, 0), (0, 0), (0, W_AB - 2 * N_HEADS))
    avec = jnp.pad(jnp.exp(a_log.reshape(depth, 1, 2 * N_HEADS)), pad)
    dtb = jnp.pad(dt_bias.reshape(depth, 1, 2 * N_HEADS), pad)
    ng = jnp.tile(norm_g.reshape(depth, 1, HEAD_DIM), (1, 1, N_HEADS))
    return avec, dtb, ng


def _gdn(u, ab, avec, dtb, ng, l, *, lc):
    nb, total, _ = u.shape
    nch = total // GDN_CHUNK
    blk = GDN_B1_CHUNKS * GDN_CHUNK
    return pl.pallas_call(
        functools.partial(_gdn_body, lc=lc),
        grid=(nb,),
        in_specs=[pl.BlockSpec((1, total, W_GDN), lambda b: (b, 0, 0)),
                  pl.BlockSpec((1, total, W_AB), lambda b: (b, 0, 0)),
                  _layer(avec, l), _layer(dtb, l), _layer(ng, l)],
        out_specs=pl.BlockSpec((1, total, GROUP_W), lambda b: (b, 0, 0)),
        out_shape=jax.ShapeDtypeStruct((nb, total, GROUP_W), BF16),
        scratch_shapes=[pltpu.VMEM((total, GROUP_W), F32),
                        pltpu.VMEM((total, GROUP_W), F32),
                        pltpu.VMEM((total, GROUP_W), F32),
                        pltpu.VMEM((total, W_AB), F32),
                        pltpu.VMEM((nch * SUBLANES, GDN_CHUNK), F32),
                        pltpu.VMEM((total, GROUP_W), F32),
                        pltpu.VMEM((total, GROUP_W), F32),
                        pltpu.VMEM((2 * N_HEADS, HEAD_DIM, HEAD_DIM), F32)] + 2 * [
                        pltpu.VMEM((2, blk, GROUP_W), F32),
                        pltpu.VMEM((2, 2 * blk, GROUP_W), BF16),
                        pltpu.VMEM((2, blk, GROUP_W), BF16),
                        pltpu.VMEM((2, GDN_B1_CHUNKS * GROUP_W, GDN_CHUNK), BF16)],
        compiler_params=_cparams(("arbitrary",)),
        name="gdn",
    )(u, ab, avec, dtb, ng)


def _pack_w_in(w_in):
    gw = GROUP_W
    o_gdn = 2 * gw
    o_ab = o_gdn + 4 * gw
    o_ret = o_ab + 4 * N_HEADS
    o_swa = o_ret + 4 * gw
    end = o_swa + W_SWA
    assert end == w_in.shape[-1]
    wb = w_in.astype(BF16)
    pad = jnp.zeros(wb.shape[:-1] + (W_AB - 4 * N_HEADS,), BF16)
    return jnp.concatenate([wb[..., 0:o_ab], wb[..., o_ret:end], wb[..., o_ab:o_ret], pad], axis=-1)


def _rope_tables(ang, lc):
    cos = jnp.cos(ang)
    sin = jnp.sin(ang)
    cos_h = jnp.concatenate([cos, cos], axis=-1)
    sin_h = jnp.concatenate([-sin, sin], axis=-1)
    cos_t = jnp.tile(cos_h, (1, N_HEADS))
    sin_t = jnp.tile(sin_h, (1, N_HEADS))
    cos_t = jnp.concatenate([jnp.ones((lc, GROUP_W), F32), cos_t], axis=0)
    sin_t = jnp.concatenate([jnp.zeros((lc, GROUP_W), F32), sin_t], axis=0)
    return cos_t, sin_t


def _rope_freqs(pos, n):
    inv = ROPE_BASE ** (-jnp.arange(0, n, 2, dtype=F32) / n)
    return pos[:, None] * inv[None, :]


def kernel(x, c, ctx, c_ctx, w_mod, b_mod, pre_norm_g, post_norm_g, w_in, w_out, lru_conv_w, lru_conv_b, lru_w_r, lru_b_r, lru_w_i, lru_b_i, lru_lambda, gdn_conv_w, gdn_a_log, gdn_dt_bias, gdn_norm_g, ret_decay_logit, swa_sink):
    nb, t, d = x.shape
    lc = ctx.shape[1]
    depth = w_mod.shape[0]
    assert t % ROW_BLOCK == 0 and lc % ROW_BLOCK == 0 and d == 4 * GROUP_W

    rows = t // GRID_W
    row = jnp.repeat(jnp.arange(rows, dtype=F32), GRID_W)
    col = jnp.tile(jnp.arange(GRID_W, dtype=F32), rows)
    ang2d = jnp.concatenate([_rope_freqs(row, HEAD_DIM // 2), _rope_freqs(col, HEAD_DIM // 2)], axis=-1)
    ang1d = _rope_freqs(jnp.arange(t, dtype=F32), HEAD_DIM)
    cos1, sin1 = _rope_tables(ang1d, lc)
    cos2, sin2 = _rope_tables(ang2d, lc)

    mod_rows = -(-(nb + 1) // SUBLANES) * SUBLANES
    s_rows = jnp.concatenate([c, c_ctx[None, :], jnp.zeros((mod_rows - nb - 1, d), F32)], axis=0)
    mod = _modulation(s_rows, w_mod, b_mod)

    w_in_p = _pack_w_in(w_in)
    w_out_b = w_out.astype(BF16)
    pre_g = pre_norm_g.reshape(depth, 1, d)
    post_g = post_norm_g.reshape(depth, 1, d)
    lru_cb, lru_wg, lru_bg, lru_lam = _lru_params(lru_conv_b, lru_w_r, lru_b_r, lru_w_i, lru_b_i, lru_lambda)
    gdn_avec, gdn_dtb, gdn_ng = _gdn_params(gdn_a_log, gdn_dt_bias, gdn_norm_g)
    ret_lg = jnp.repeat(ret_decay_logit, HEAD_DIM, axis=-1)
    sink = swa_sink.reshape(depth * N_HEADS)
    xs = jnp.concatenate([ctx, x], axis=1)
    for l in range(depth):
        u_lru, u_gdn, u_ret, u_swa, u_ab = _in_proj(xs, mod, pre_g, w_in_p, (cos1, sin1, cos2, sin2),
                                                    (lru_conv_w, lru_cb, gdn_conv_w), l, lc=lc)
        ya = _lru(u_lru, lru_wg, lru_bg, lru_lam, l, lc=lc)
        yb = _gdn(u_gdn, u_ab, gdn_avec, gdn_dtb, gdn_ng, l, lc=lc)
        yc = _ret(u_ret, ret_lg, l, lc=lc)
        yd = _swa(u_swa, sink, l, lc=lc)
        xs = _out_proj((ya, yb, yc, yd), xs, mod, post_g, w_out_b, l, lc=lc)
    return xs[:, lc:, :]
```

```python
import functools

import jax
import jax.numpy as jnp
from jax import lax
from jax.experimental import pallas as pl
from jax.experimental.pallas import tpu as pltpu

F32 = jnp.float32
BF16 = jnp.bfloat16

HEAD_DIM = 64
GROUP_W = 256
N_HEADS = GROUP_W // HEAD_DIM
NORM_EPS = 1e-6
ROPE_BASE = 10000.0
NEG_INF = -1e30
GRID_W = 64
CONV_W = 4
LRU_C = 8.0
LRU_SCAN_UNROLL = 4
GDN_CHUNK = 64
GDN_B1_CHUNKS = 4
RET_CHUNK = 256
SWA_KV_HEADS = 2
SWA_KV_W = SWA_KV_HEADS * HEAD_DIM
WINDOW = 128
SWA_BLOCKS_PER_STEP = 2
ROW_BLOCK = 256
CONV_ROWS = 64
PROJ_ROWS_MAX = 576
SUBLANES = 8
VMEM_LIMIT_BYTES = 56 * 1024 * 1024

W_LRU = 2 * GROUP_W
W_GDN = 4 * GROUP_W
W_RET = 4 * GROUP_W
W_SWA = 2 * GROUP_W + 2 * SWA_KV_W
W_AB = 128
IN_W_PACKED = W_LRU + W_GDN + W_RET + W_SWA + W_AB


def _cparams(sem, flags=None):
    return pltpu.CompilerParams(dimension_semantics=sem, vmem_limit_bytes=VMEM_LIMIT_BYTES, flags=flags)


def _dot(a, b):
    return jnp.dot(a.astype(BF16), b.astype(BF16), preferred_element_type=F32)


def _dot_nt(a, b):
    return lax.dot_general(a.astype(BF16), b.astype(BF16), (((1,), (1,)), ((), ())),
                           preferred_element_type=F32)


def _dot_tn(a, b):
    return lax.dot_general(a.astype(BF16), b.astype(BF16), (((0,), (0,)), ((), ())),
                           preferred_element_type=F32)


def _split_dot(x, w, parts):
    acc = None
    r = x
    for _ in range(parts):
        hi = r.astype(BF16)
        t = jnp.dot(hi, w, preferred_element_type=F32)
        acc = t if acc is None else acc + t
        r = r - hi.astype(F32)
    return acc


def _split_dot_left(w, x, parts):
    acc = None
    r = x
    for _ in range(parts):
        hi = r.astype(BF16)
        t = jnp.dot(w, hi, preferred_element_type=F32)
        acc = t if acc is None else acc + t
        r = r - hi.astype(F32)
    return acc


def _sigmoid(x):
    return 0.5 * jnp.tanh(0.5 * x) + 0.5


def _silu(x):
    return x * _sigmoid(x)


def _softplus(x):
    return jnp.maximum(x, 0.0) + jnp.log1p(jnp.exp(-jnp.abs(x)))


def _head_ones():
    r = lax.broadcasted_iota(jnp.int32, (GROUP_W, GROUP_W), 0) // HEAD_DIM
    c = lax.broadcasted_iota(jnp.int32, (GROUP_W, GROUP_W), 1) // HEAD_DIM
    return (r == c).astype(BF16)


def _row_blocks(lc, total):
    out = []
    for r0 in range(0, total, ROW_BLOCK):
        out.append((r0, 0, lc) if r0 < lc else (r0, lc, total))
    return out


def _mod_body(s_ref, w_ref, b_ref, o_ref):
    s = _silu(s_ref[...])
    o_ref[0] = _dot(s, w_ref[0]) + b_ref[0]


def _modulation(s_rows, w_mod, b_mod):
    depth, d, d3 = w_mod.shape
    rows = s_rows.shape[0]
    nt = d3 // d
    return pl.pallas_call(
        _mod_body,
        grid=(depth, nt),
        in_specs=[pl.BlockSpec((rows, d), lambda l, j: (0, 0)),
                  pl.BlockSpec((1, d, d), lambda l, j: (l, 0, j)),
                  pl.BlockSpec((1, 1, d), lambda l, j: (l, 0, j))],
        out_specs=pl.BlockSpec((1, rows, d), lambda l, j: (l, 0, j)),
        out_shape=jax.ShapeDtypeStruct((depth, rows, d3), F32),
        compiler_params=_cparams(("arbitrary", "arbitrary")),
        name="modulation",
    )(s_rows, w_mod, b_mod.reshape(depth, 1, d3))


def _proj_rows(total):
    for tm in range(PROJ_ROWS_MAX, SUBLANES - 1, -SUBLANES):
        if total % tm == 0:
            return tm
    raise ValueError(total)


def _mod_rows(mod_ref, b, i, tm, lc, nb, c0, c1):
    lat = mod_ref[pl.ds(b, 1), c0:c1]
    if lc % tm == 0:
        ctx = mod_ref[nb:nb + 1, c0:c1]
        return jnp.where(i * tm < lc, ctx, lat)
    ctx = mod_ref[nb:nb + 1, c0:c1]
    row = i * tm + lax.broadcasted_iota(jnp.int32, (tm, 1), 0)
    return jnp.where(row < lc, ctx, lat)


def _inproj_body(x_ref, xp_ref, xn_ref, mod_ref, g_ref, w_ref, cos1_ref, sin1_ref, cos2_ref, sin2_ref,
                 lcw_ref, lcb_ref, gcw_ref, o_lru, o_gdn, o_ret, o_swa, o_ab, *conv_scr, lc, nb, total):
    i = pl.program_id(0)
    b = pl.program_id(1)
    d = x_ref.shape[2]
    tm = x_ref.shape[1]
    gw = GROUP_W
    scale = HEAD_DIM ** -0.5

    def prenorm(xv, m):
        ms = jnp.mean(xv * xv, axis=-1, keepdims=True)
        y = xv * lax.rsqrt(ms + NORM_EPS) * g_ref[...]
        return (y * (1.0 + m[:, d:2 * d]) + m[:, 0:d]).astype(BF16)

    h = prenorm(x_ref[0], _mod_rows(mod_ref, b, i, tm, lc, nb, 0, 2 * d))
    halo = prenorm(jnp.concatenate([xp_ref[0], xn_ref[0]], axis=0), mod_ref[pl.ds(b, 1), 0:2 * d])
    h_scr = conv_scr[4]
    h_scr[0:tm, :] = h
    h_scr[tm:tm + 2 * SUBLANES, :] = halo
    h_all = slice(0, tm + 2 * SUBLANES)

    def proj(c0, width, rows=slice(0, tm)):
        return jnp.dot(h_scr[rows, :], w_ref[:, c0:c0 + width], preferred_element_type=F32)

    nt = total // tm
    b_tile, b_row = lc // tm, lc % tm
    assert b_row % SUBLANES == 0 and (b_row == 0 or SUBLANES <= b_row <= tm - SUBLANES)
    prev_ok = i > 0
    next_ok = i < nt - 1
    if b_row == 0:
        prev_ok = prev_ok & (i != b_tile)
        next_ok = next_ok & (i != b_tile - 1)
    prev_f = jnp.where(prev_ok, 1.0, 0.0)
    next_f = jnp.where(next_ok, 1.0, 0.0)

    def dwconv_to(o_ref, oc0, u_all, scr, cw_ref, c0, post):
        scr[0:SUBLANES, :] = u_all[tm:tm + SUBLANES] * prev_f
        scr[SUBLANES:SUBLANES + tm, :] = u_all[0:tm]
        scr[SUBLANES + tm:2 * SUBLANES + tm, :] = u_all[tm + SUBLANES:tm + 2 * SUBLANES] * next_f
        for r0 in range(0, tm, CONV_ROWS):
            n = min(CONV_ROWS, tm - r0)
            near_boundary = b_row and r0 - 1 <= b_row <= r0 + n + 1
            acc = None
            for k in range(CONV_W):
                off = k - 2
                tap = scr[SUBLANES + r0 + off:SUBLANES + r0 + off + n, :]
                if near_boundary and off != 0:
                    r = r0 + off + lax.broadcasted_iota(jnp.int32, (n, 1), 0)
                    keep = ((r >= b_row) == (r - off >= b_row)) | (i != b_tile)
                    tap = jnp.where(keep, tap, 0.0)
                term = tap * cw_ref[k:k + 1, c0:c0 + gw]
                acc = term if acc is None else acc + term
            o_ref[0, r0:r0 + n, oc0:oc0 + gw] = post(acc)

    dwconv_to(o_lru, 0, proj(0, gw, h_all), conv_scr[0], lcw_ref, 0, lambda a: a + lcb_ref[...])
    o_lru[0, :, gw:2 * gw] = _silu(proj(gw, gw))
    c0 = W_LRU
    for p in range(3):
        dwconv_to(o_gdn, p * gw, proj(c0 + p * gw, gw, h_all), conv_scr[1 + p], gcw_ref, p * gw, _silu)
    o_gdn[0, :, 3 * gw:4 * gw] = _silu(proj(c0 + 3 * gw, gw))
    c0 += W_GDN
    u = proj(c0, W_RET)
    cos, sin = cos1_ref[...], sin1_ref[...]
    o_ret[0, :, 0:gw] = _rope(u[:, 0:gw], cos, sin)
    o_ret[0, :, gw:2 * gw] = _rope(u[:, gw:2 * gw], cos, sin) * scale
    o_ret[0, :, 2 * gw:3 * gw] = u[:, 2 * gw:3 * gw]
    o_ret[0, :, 3 * gw:4 * gw] = _silu(u[:, 3 * gw:4 * gw])
    c0 += W_RET
    u = proj(c0, W_SWA)
    cos, sin = cos2_ref[...], sin2_ref[...]
    kvw = SWA_KV_W
    o_swa[0, :, 0:gw] = _rope(u[:, 0:gw], cos, sin) * scale
    o_swa[0, :, gw:gw + kvw] = _rope(u[:, gw:gw + kvw], cos[:, 0:kvw], sin[:, 0:kvw])
    o_swa[0, :, gw + kvw:gw + 2 * kvw] = u[:, gw + kvw:gw + 2 * kvw]
    o_swa[0, :, gw + 2 * kvw:2 * gw + 2 * kvw] = _silu(u[:, gw + 2 * kvw:2 * gw + 2 * kvw])
    c0 += W_SWA
    o_ab[0] = proj(c0, W_AB)


def _in_proj(x, mod, g, w, tables, conv_params, l, *, lc):
    nb, total, d = x.shape
    tm = _proj_rows(total)
    nt = total // tm
    per = tm // SUBLANES
    last = total // SUBLANES - 1
    widths = (W_LRU, W_GDN, W_RET, W_SWA, W_AB)
    tspec = pl.BlockSpec((tm, GROUP_W), lambda i, b: (i, 0))
    return pl.pallas_call(
        functools.partial(_inproj_body, lc=lc, nb=nb, total=total),
        grid=(nt, nb),
        in_specs=[pl.BlockSpec((1, tm, d), lambda i, b: (b, i, 0)),
                  pl.BlockSpec((1, SUBLANES, d), lambda i, b: (b, jnp.maximum(i * per - 1, 0), 0)),
                  pl.BlockSpec((1, SUBLANES, d), lambda i, b: (b, jnp.minimum((i + 1) * per, last), 0)),
                  _layer(mod, l), _layer(g, l), _layer(w, l), tspec, tspec, tspec, tspec]
                 + [_layer(p, l) for p in conv_params],
        out_specs=[pl.BlockSpec((1, tm, wd), lambda i, b: (b, i, 0)) for wd in widths],
        out_shape=[jax.ShapeDtypeStruct((nb, total, wd), F32) for wd in widths],
        scratch_shapes=[pltpu.VMEM((tm + 2 * SUBLANES, GROUP_W), F32) for _ in range(4)]
                       + [pltpu.VMEM((tm + 2 * SUBLANES, d), BF16)],
        compiler_params=_cparams(("arbitrary", "arbitrary")),
        name="in_proj",
    )(x, x, x, mod, g, w, *tables, *conv_params)


def _outproj_body(ya, yb, yc, yd, x_ref, mod_ref, g_ref, w_ref, o_ref, *, lc, nb):
    b = pl.program_id(0)
    i = pl.program_id(1)
    d = x_ref.shape[2]
    acc = None
    for k, y_ref in enumerate((ya, yb, yc, yd)):
        t = jnp.dot(y_ref[0], w_ref[k * GROUP_W:(k + 1) * GROUP_W, :], preferred_element_type=F32)
        acc = t if acc is None else acc + t
    ms = jnp.mean(acc * acc, axis=-1, keepdims=True)
    yn = acc * lax.rsqrt(ms + NORM_EPS) * g_ref[...]
    gate = _mod_rows(mod_ref, b, i, acc.shape[0], lc, nb, 2 * d, 3 * d)
    o_ref[0] = x_ref[0] + gate * yn


def _out_proj(ys, x, mod, g, w, l, *, lc):
    nb, total, d = x.shape
    tm = _proj_rows(total)
    nt = total // tm
    yspec = pl.BlockSpec((1, tm, GROUP_W), lambda b, i: (b, i, 0))
    return pl.pallas_call(
        functools.partial(_outproj_body, lc=lc, nb=nb),
        grid=(nb, nt),
        in_specs=[yspec, yspec, yspec, yspec,
                  pl.BlockSpec((1, tm, d), lambda b, i: (b, i, 0)),
                  _layer(mod, l), _layer(g, l), _layer(w, l)],
        out_specs=pl.BlockSpec((1, tm, d), lambda b, i: (b, i, 0)),
        out_shape=jax.ShapeDtypeStruct(x.shape, F32),
        compiler_params=_cparams(("arbitrary", "arbitrary")),
        name="out_proj",
    )(*ys, x, mod, g, w)


def _lru_scan(a_s, b_s, h_s, tile_lo, n_tiles, carry, *, rev, accumulate):
    row = lax.broadcasted_iota(jnp.int32, (SUBLANES, GROUP_W), 0)

    def body(j, carry):
        t = tile_lo + (n_tiles - 1 - j if rev else j)
        r = pl.multiple_of(t * SUBLANES, SUBLANES)
        a = a_s[pl.ds(r, SUBLANES), :]
        b = b_s[pl.ds(r, SUBLANES), :]
        for s in (1, 2, 4):
            if rev:
                ra = pltpu.roll(a, SUBLANES - s, 0)
                rb = pltpu.roll(b, SUBLANES - s, 0)
                m = row < SUBLANES - s
            else:
                ra = pltpu.roll(a, s, 0)
                rb = pltpu.roll(b, s, 0)
                m = row >= s
            b = a * jnp.where(m, rb, 0.0) + b
            a = a * jnp.where(m, ra, 1.0)
        h = a * carry + b
        if accumulate:
            h_s[pl.ds(r, SUBLANES), :] = h_s[pl.ds(r, SUBLANES), :] + h
        else:
            h_s[pl.ds(r, SUBLANES), :] = h
        return h[0:1, :] if rev else h[SUBLANES - 1:SUBLANES, :]

    return lax.fori_loop(0, n_tiles, body, carry, unroll=LRU_SCAN_UNROLL)


def _lru_body(u_ref, wg_ref, bg_ref, lam_ref, o_ref, a_s, b_s, h_s, *, lc):
    total = u_ref.shape[1]
    blocks = _row_blocks(lc, total)
    zero = jnp.zeros((1, GROUP_W), F32)
    for d in range(2):
        sp = _softplus(-lam_ref[d])
        for r0, _, _ in blocks:
            uc = u_ref[0, r0:r0 + ROW_BLOCK, 0:GROUP_W]
            gts = _dot(uc, wg_ref[d]) + bg_ref[d]
            r = _sigmoid(gts[:, 0:GROUP_W])
            ig = _sigmoid(gts[:, GROUP_W:2 * GROUP_W])
            a = jnp.exp(-LRU_C * r * sp)
            a_s[r0:r0 + ROW_BLOCK, :] = a
            b_s[r0:r0 + ROW_BLOCK, :] = jnp.sqrt(1.0 - a * a) * (ig * uc)
        ct, tt = lc // SUBLANES, total // SUBLANES
        if d == 0:
            _lru_scan(a_s, b_s, h_s, 0, tt, zero, rev=False, accumulate=False)
        else:
            carry = _lru_scan(a_s, b_s, h_s, 0, ct, zero, rev=True, accumulate=True)
            _lru_scan(a_s, b_s, h_s, ct, tt - ct, carry, rev=True, accumulate=True)
    for r0, _, _ in blocks:
        gate = u_ref[0, r0:r0 + ROW_BLOCK, GROUP_W:2 * GROUP_W]
        o_ref[0, r0:r0 + ROW_BLOCK, :] = (h_s[r0:r0 + ROW_BLOCK, :] * gate).astype(BF16)


def _block_diag(w):
    n, c = w.shape[-3], w.shape[-1]
    eye = jnp.eye(n, dtype=w.dtype)
    return (eye[:, None, :, None] * w[..., :, :, None, :]).reshape(w.shape[:-3] + (n * c, n * c))


def _layer(arr, l):
    shape = arr.shape[1:]
    return pl.BlockSpec((None,) + shape, lambda *_: (l,) + (0,) * len(shape))


def _lru_params(conv_b, w_r, b_r, w_i, b_i, lam):
    depth = conv_b.shape[0]
    wg = jnp.concatenate([_block_diag(w_r), _block_diag(w_i)], axis=-1).astype(BF16)
    bg = jnp.concatenate([b_r, b_i], axis=-1).reshape(depth, 2, 1, 2 * GROUP_W)
    return conv_b.reshape(depth, 1, GROUP_W), wg, bg, lam.reshape(depth, 2, 1, GROUP_W)


def _lru(u, wg, bg, lam, l, *, lc):
    nb, total, _ = u.shape
    return pl.pallas_call(
        functools.partial(_lru_body, lc=lc),
        grid=(nb,),
        in_specs=[pl.BlockSpec((1, total, W_LRU), lambda b: (b, 0, 0)),
                  _layer(wg, l), _layer(bg, l), _layer(lam, l)],
        out_specs=pl.BlockSpec((1, total, GROUP_W), lambda b: (b, 0, 0)),
        out_shape=jax.ShapeDtypeStruct((nb, total, GROUP_W), BF16),
        scratch_shapes=[pltpu.VMEM((total, GROUP_W), F32) for _ in range(3)],
        compiler_params=_cparams(("arbitrary",)),
        name="lru",
    )(u, wg, bg, lam)


def _rope(x, cos, sin_signed):
    half = HEAD_DIM // 2
    outs = []
    for c0 in range(0, x.shape[1], 128):
        xs = x[:, c0:c0 + 128]
        lane = lax.broadcasted_iota(jnp.int32, xs.shape, 1)
        swapped = jnp.where((lane % HEAD_DIM) < half,
                            pltpu.roll(xs, 128 - half, 1), pltpu.roll(xs, half, 1))
        outs.append(xs * cos[:, c0:c0 + 128] + swapped * sin_signed[:, c0:c0 + 128])
    return outs[0] if len(outs) == 1 else jnp.concatenate(outs, axis=1)


def _ret_body(u_ref, lg_ref, o_ref, ds_s, st_s, m_s, *, lc):
    total = u_ref.shape[1]
    c = RET_CHUNK
    nch = total // c
    n_ctx = lc // c
    gw = GROUP_W
    ones_bd = _head_ones()
    bd_mask = ones_bd > 0
    lg = -_softplus(-lg_ref[...])
    lgf, lgb = lg[0:1, :], lg[1:2, :]
    pos = lax.broadcasted_iota(jnp.int32, (c, 1), 0).astype(F32)
    qdec_f = jnp.exp((pos + 1.0) * lgf)
    qdec_b = jnp.exp((c - pos) * lgb)
    kdec_f = jnp.exp((c - 1.0 - pos) * lgf)
    kdec_b = jnp.exp(pos * lgb)
    cdec_f = jnp.exp(float(c) * lgf)
    cdec_b = jnp.exp(float(c) * lgb)
    dij = (lax.broadcasted_iota(jnp.int32, (c, c), 0)
           - lax.broadcasted_iota(jnp.int32, (c, c), 1)).astype(F32)
    for h in range(N_HEADS):
        lf = lgf[:, h * HEAD_DIM:h * HEAD_DIM + 1]
        lb = lgb[:, h * HEAD_DIM:h * HEAD_DIM + 1]
        fwd = jnp.exp(jnp.maximum(dij, 0.0) * lf)
        bwd = jnp.exp(jnp.maximum(-dij, 0.0) * lb)
        m_s[h] = jnp.where(dij > 0, fwd, jnp.where(dij < 0, bwd, 2.0))

    for ci in range(nch):
        r0 = ci * c
        kr = u_ref[0, r0:r0 + c, gw:2 * gw]
        v = u_ref[0, r0:r0 + c, 2 * gw:3 * gw]
        ds_s[0, ci] = jnp.where(bd_mask, _dot_tn(kr * kdec_f, v), 0.0)
        ds_s[1, ci] = jnp.where(bd_mask, _dot_tn(kr * kdec_b, v), 0.0)

    s = jnp.zeros((gw, gw), F32)
    for ci in range(nch):
        st_s[0, ci] = s
        s = s * cdec_f + ds_s[0, ci]
    s = jnp.zeros((gw, gw), F32)
    for ci in list(range(n_ctx - 1, -1, -1)) + list(range(nch - 1, n_ctx - 1, -1)):
        st_s[1, ci] = s
        s = s * cdec_b + ds_s[1, ci]

    for ci in range(nch):
        r0 = ci * c
        qr = u_ref[0, r0:r0 + c, 0:gw]
        kr = u_ref[0, r0:r0 + c, gw:2 * gw]
        v = u_ref[0, r0:r0 + c, 2 * gw:3 * gw]
        gate = u_ref[0, r0:r0 + c, 3 * gw:4 * gw]
        hss = [slice(h * HEAD_DIM, (h + 1) * HEAD_DIM) for h in range(N_HEADS)]
        inter = _dot(qr * qdec_f, st_s[0, ci]) + _dot(qr * qdec_b, st_s[1, ci])
        atts = [_dot_nt(qr[:, hs], kr[:, hs]) for hs in hss]
        atts = [(a * m_s[h]).astype(BF16) for h, a in enumerate(atts)]
        o = inter + jnp.concatenate(
            [jnp.dot(a, v[:, hs].astype(BF16), preferred_element_type=F32) for a, hs in zip(atts, hss)],
            axis=1)
        mu = _split_dot(o, ones_bd, 2) * (1.0 / HEAD_DIM)
        dlt = o - mu
        var = _split_dot(dlt * dlt, ones_bd, 2) * (1.0 / HEAD_DIM)
        y = dlt * lax.rsqrt(var + NORM_EPS) * gate
        o_ref[0, r0:r0 + c, :] = y.astype(BF16)


def _ret(u, lg, l, *, lc):
    nb, total, _ = u.shape
    nch = total // RET_CHUNK
    return pl.pallas_call(
        functools.partial(_ret_body, lc=lc),
        grid=(nb,),
        in_specs=[pl.BlockSpec((1, total, W_RET), lambda b: (b, 0, 0)), _layer(lg, l)],
        out_specs=pl.BlockSpec((1, total, GROUP_W), lambda b: (b, 0, 0)),
        out_shape=jax.ShapeDtypeStruct((nb, total, GROUP_W), BF16),
        scratch_shapes=[pltpu.VMEM((2, nch, GROUP_W, GROUP_W), F32),
                        pltpu.VMEM((2, nch, GROUP_W, GROUP_W), F32),
                        pltpu.VMEM((N_HEADS, RET_CHUNK, RET_CHUNK), F32)],
        compiler_params=_cparams(("arbitrary",)),
        name="retention",
    )(u, lg)


def _swa_body(sink_ref, u_ref, o_ref, k_s, v_s, *, lc, sink0):
    total = u_ref.shape[1]
    t_lat = total - lc
    gw = GROUP_W
    blk = WINDOW
    nblk = t_lat // blk
    kv0 = lc + blk
    zpad = jnp.zeros((blk, SWA_KV_W), BF16)
    for s_ref in (k_s, v_s):
        s_ref[lc:lc + blk, :] = zpad
        s_ref[kv0 + t_lat:kv0 + t_lat + blk, :] = zpad
    for r0, _, _ in _row_blocks(lc, total):
        dst = r0 if r0 < lc else r0 + blk
        k_s[dst:dst + ROW_BLOCK, :] = u_ref[0, r0:r0 + ROW_BLOCK, gw:gw + SWA_KV_W].astype(BF16)
        v_s[dst:dst + ROW_BLOCK, :] = (
            u_ref[0, r0:r0 + ROW_BLOCK, gw + SWA_KV_W:gw + 2 * SWA_KV_W].astype(BF16))

    grp = N_HEADS // SWA_KV_HEADS

    def attend(items):
        scores = [[_dot_nt(q2, kk) for kk in keys] for q2, keys, _, _, _ in items]
        exps, dens = [], []
        for (_, _, _, masks, sink_col), sc in zip(items, scores):
            sc = [s if mk is None else jnp.where(mk, s, NEG_INF) for s, mk in zip(sc, masks)]
            mx = sink_col
            for s in sc:
                mx = jnp.maximum(mx, jnp.max(s, axis=-1, keepdims=True))
            es = [jnp.exp(s - mx) for s in sc]
            den = jnp.exp(sink_col - mx)
            for e in es:
                den = den + jnp.sum(e, axis=-1, keepdims=True)
            exps.append([e.astype(BF16) for e in es])
            dens.append(den)
        outs = []
        for (_, _, vals, _, _), es, den in zip(items, exps, dens):
            acc = None
            for e, vv in zip(es, vals):
                t = jnp.dot(e, vv.astype(BF16), preferred_element_type=F32)
                acc = t if acc is None else acc + t
            outs.append(acc / den)
        return outs

    def sink_column(hk, rows_per_head):
        row = lax.broadcasted_iota(jnp.int32, (grp * rows_per_head, 1), 0)
        col = jnp.full((grp * rows_per_head, 1), sink_ref[sink0 + hk * grp], F32)
        for g in range(1, grp):
            col = jnp.where(row >= g * rows_per_head, sink_ref[sink0 + hk * grp + g], col)
        return col

    kvs = [slice(hk * HEAD_DIM, (hk + 1) * HEAD_DIM) for hk in range(SWA_KV_HEADS)]

    def stack_q(hk, rows):
        return jnp.concatenate(
            [u_ref[0, rows, (hk * grp + g) * HEAD_DIM:(hk * grp + g + 1) * HEAD_DIM] for g in range(grp)],
            axis=0)

    def unstack_o(o2s, n):
        return jnp.concatenate([o2s[hk][g * n:(g + 1) * n, :]
                                for hk in range(SWA_KV_HEADS) for g in range(grp)], axis=1)

    o2s = attend([(stack_q(hk, slice(0, lc)), [k_s[0:lc, kvs[hk]]], [v_s[0:lc, kvs[hk]]], [None],
                   sink_column(hk, lc)) for hk in range(SWA_KV_HEADS)])
    gate = u_ref[0, 0:lc, gw + 2 * SWA_KV_W:2 * gw + 2 * SWA_KV_W]
    o_ref[0, 0:lc, :] = (unstack_o(o2s, lc) * gate).astype(BF16)

    qi = lax.broadcasted_iota(jnp.int32, (grp * blk, 3 * blk), 0) % blk
    kj = lax.broadcasted_iota(jnp.int32, (grp * blk, 3 * blk), 1)
    in_win = jnp.abs(kj - blk - qi) <= WINDOW

    assert nblk % SWA_BLOCKS_PER_STEP == 0

    def block_body(step, carry):
        items, qrows = [], []
        for s in range(SWA_BLOCKS_PER_STEP):
            n = step * SWA_BLOCKS_PER_STEP + s
            kpos = n * blk - blk + kj
            mask = in_win & (kpos >= 0) & (kpos < t_lat)
            qrow = pl.multiple_of(lc + n * blk, blk)
            wrow = qrow
            qrows.append(qrow)
            items += [(stack_q(hk, pl.ds(qrow, blk)),
                       [k_s[pl.ds(wrow, 3 * blk), kvs[hk]], k_s[0:lc, kvs[hk]]],
                       [v_s[pl.ds(wrow, 3 * blk), kvs[hk]], v_s[0:lc, kvs[hk]]],
                       [mask, None], sink_column(hk, blk)) for hk in range(SWA_KV_HEADS)]
        o2s = attend(items)
        for s, qrow in enumerate(qrows):
            gate = u_ref[0, pl.ds(qrow, blk), gw + 2 * SWA_KV_W:2 * gw + 2 * SWA_KV_W]
            o = unstack_o(o2s[s * SWA_KV_HEADS:(s + 1) * SWA_KV_HEADS], blk)
            o_ref[0, pl.ds(qrow, blk), :] = (o * gate).astype(BF16)
        return carry

    lax.fori_loop(0, nblk // SWA_BLOCKS_PER_STEP, block_body, 0)


def _swa(u, sink, l, *, lc):
    nb, total, _ = u.shape
    grid_spec = pltpu.PrefetchScalarGridSpec(
        num_scalar_prefetch=1,
        grid=(nb,),
        in_specs=[pl.BlockSpec((1, total, W_SWA), lambda b, s: (b, 0, 0))],
        out_specs=pl.BlockSpec((1, total, GROUP_W), lambda b, s: (b, 0, 0)),
        scratch_shapes=[pltpu.VMEM((total + 2 * WINDOW, SWA_KV_W), BF16),
                        pltpu.VMEM((total + 2 * WINDOW, SWA_KV_W), BF16)],
    )
    return pl.pallas_call(
        functools.partial(_swa_body, lc=lc, sink0=l * N_HEADS),
        grid_spec=grid_spec,
        out_shape=jax.ShapeDtypeStruct((nb, total, GROUP_W), BF16),
        compiler_params=_cparams(("arbitrary",)),
        name="swa",
    )(sink, u)


def _gdn_body(u_ref, ab_ref, avec_ref, dtb_ref, ng_ref, o_ref,
              q_s, k_s, v_s, cs_s, gt_s, of_s, ob_s, st_s, *ring_refs, lc):
    ring = (ring_refs[0:4], ring_refs[4:8])
    total = u_ref.shape[1]
    gw = GROUP_W
    c = GDN_CHUNK
    nch = total // c
    n_ctx = lc // c
    per_blk = ROW_BLOCK // c
    ones_bd = _head_ones()

    ri = lax.broadcasted_iota(jnp.int32, (ROW_BLOCK, ROW_BLOCK), 0)
    ci_ = lax.broadcasted_iota(jnp.int32, (ROW_BLOCK, ROW_BLOCK), 1)
    same_chunk = (ri // c) == (ci_ // c)
    tri_lo = (same_chunk & (ri >= ci_)).astype(BF16)
    tri_up = (same_chunk & (ri <= ci_)).astype(BF16)
    lane = lax.broadcasted_iota(jnp.int32, (ROW_BLOCK, W_AB), 1)

    for bi, (r0, lo, hi) in enumerate(_row_blocks(lc, total)):
        h = u_ref[0, r0:r0 + ROW_BLOCK, 0:3 * gw]
        hq, hk, hv = h[:, 0:gw], h[:, gw:2 * gw], h[:, 2 * gw:3 * gw]
        ssq = _split_dot(hq * hq, ones_bd, 2)
        q_s[r0:r0 + ROW_BLOCK, :] = hq * lax.rsqrt(ssq + NORM_EPS) * (HEAD_DIM ** -0.5)
        ssk = _split_dot(hk * hk, ones_bd, 2)
        k_s[r0:r0 + ROW_BLOCK, :] = hk * lax.rsqrt(ssk + NORM_EPS)
        v_s[r0:r0 + ROW_BLOCK, :] = hv
        ab = ab_ref[0, r0:r0 + ROW_BLOCK, :]
        g = -avec_ref[...] * _softplus(ab + dtb_ref[...])
        g = jnp.where(lane < 2 * N_HEADS, g, 0.0)
        beta = _sigmoid(ab)
        cs_lo = _split_dot_left(tri_lo, g, 3)
        cs_up = _split_dot_left(tri_up, g, 3)
        cs = jnp.where(lane < N_HEADS, cs_lo, jnp.where(lane < 2 * N_HEADS, cs_up, beta))
        cs_s[r0:r0 + ROW_BLOCK, :] = cs
        cst = cs.T
        for cc in range(per_blk):
            row0 = (bi * per_blk + cc) * SUBLANES
            gt_s[row0:row0 + SUBLANES, :] = cst[0:SUBLANES, cc * c:(cc + 1) * c]

    ii = lax.broadcasted_iota(jnp.int32, (c, c), 0)
    jj = lax.broadcasted_iota(jnp.int32, (c, c), 1)
    eye = (ii == jj).astype(F32)
    n_sq = c.bit_length() - 2
    incl = ((ii >= jj), (ii <= jj))
    strict = ((ii > jj), (ii < jj))
    last = (c - 1, 0)
    heads = [slice(h * HEAD_DIM, (h + 1) * HEAD_DIM) for h in range(N_HEADS)]
    def expander(n_terms, lane0):
        src = lax.broadcasted_iota(jnp.int32, (n_terms * W_AB, N_HEADS * 128), 0) % W_AB
        grp = lax.broadcasted_iota(jnp.int32, (n_terms * W_AB, N_HEADS * 128), 1) // 128
        return (src == lane0 + grp).astype(BF16)

    expand_g = [expander(3, d * N_HEADS) for d in range(2)]
    expand_b = [expander(2, (2 + d) * N_HEADS) for d in range(2)]

    blk_rows = GDN_B1_CHUNKS * c
    n_steps = total // blk_rows
    assert lc % blk_rows == 0 and n_steps >= 2

    def block_row0(j, d):
        if d == 0:
            return j * blk_rows
        if isinstance(j, int):
            return lc - (j + 1) * blk_rows if (j + 1) * blk_rows <= lc else total + lc - (j + 1) * blk_rows
        return pl.multiple_of(jnp.where((j + 1) * blk_rows <= lc, lc - (j + 1) * blk_rows,
                                        total + lc - (j + 1) * blk_rows), blk_rows)

    def cat_heads(items):
        return jnp.concatenate(items, axis=1)

    def b1_gen(j, slot):
        uvr_s, wqr_s, attr_s, kttr_s = ring[slot]
        probs = []
        for d in range(2):
            r0 = block_row0(j, d)
            parts = []
            rem = cs_s[pl.ds(r0, blk_rows), :]
            for _ in range(3):
                hi = rem.astype(BF16)
                parts.append(hi)
                rem = rem - hi.astype(F32)
            xp_g = jnp.dot(jnp.concatenate(parts, axis=1), expand_g[d], preferred_element_type=F32)
            xp_b = jnp.dot(jnp.concatenate(parts[0:2], axis=1), expand_b[d], preferred_element_type=F32)
            for cc in range(GDN_B1_CHUNKS):
                r = r0 + cc * c
                kc = k_s[pl.ds(r, c), :]
                qc = q_s[pl.ds(r, c), :]
                vc = v_s[pl.ds(r, c), :]
                gt = gt_s[pl.ds(pl.multiple_of((r0 // c + cc) * SUBLANES, SUBLANES), SUBLANES), :]
                sel = []
                for h in range(N_HEADS):
                    kh, qh, vh = kc[:, heads[h]], qc[:, heads[h]], vc[:, heads[h]]
                    kk = _dot_nt(kh, kh)
                    qk = _dot_nt(qh, kh)
                    col = xp_g[cc * c:(cc + 1) * c, h * 128:h * 128 + HEAD_DIM]
                    bcol = xp_b[cc * c:(cc + 1) * c, h * 128:h * 128 + HEAD_DIM]
                    rowv = gt[d * N_HEADS + h:d * N_HEADS + h + 1, :]
                    decay = jnp.where(incl[d], jnp.exp(jnp.where(incl[d], col - rowv, 0.0)), 0.0)
                    n = -jnp.where(strict[d], kk * decay, 0.0) * bcol
                    eg = jnp.exp(col)
                    glast = col[last[d]:last[d] + 1, :]
                    pr = dict(cc=cc, d=d, a=(eye - n).astype(BF16), p=eye + n,
                              rhs=jnp.concatenate([vh * bcol, kh * (bcol * eg)], axis=1).astype(BF16),
                              att=qk * decay, qe=qh * eg, kt=kh * jnp.exp(glast - col))
                    probs.append(pr)
                    sel.append(pr)
                wqr_s[d,(2 * cc + 1) * c:(2 * cc + 2) * c, :] = (
                    cat_heads([pr['qe'] for pr in sel]).astype(BF16))
                attr_s[d,cc * c:(cc + 1) * c, :] = cat_heads([pr['att'] for pr in sel]).astype(BF16)
                ktt = cat_heads([pr['kt'] for pr in sel]).T
                kttr_s[d,cc * gw:(cc + 1) * gw, :] = ktt.astype(BF16)
        yield
        for _ in range(n_sq):
            res = [eye - jnp.dot(pr['a'], pr['p'].astype(BF16), preferred_element_type=F32)
                   for pr in probs]
            yield
            for pr, rr in zip(probs, res):
                pr['p'] = pr['p'] + _dot(pr['p'], rr)
            yield
        for pr in probs:
            pr['uw'] = _dot(pr['p'], pr['rhs'])
        yield
        for d in range(2):
            for cc in range(GDN_B1_CHUNKS):
                sel = [pr for pr in probs if pr['cc'] == cc and pr['d'] == d]
                uvr_s[d,cc * c:(cc + 1) * c, :] = cat_heads([pr['uw'][:, 0:HEAD_DIM] for pr in sel])
                wqr_s[d,2 * cc * c:(2 * cc + 1) * c, :] = (
                    cat_heads([pr['uw'][:, HEAD_DIM:2 * HEAD_DIM] for pr in sel]).astype(BF16))

    def b2_gen(j, slot):
        uvr_s, wqr_s, attr_s, kttr_s = ring[slot]
        r0s = [block_row0(j, d) for d in range(2)]
        states = [st_s[hd] for hd in range(2 * N_HEADS)]
        for step in range(GDN_B1_CHUNKS):
            probs = []
            for d in range(2):
                cc = step if d == 0 else GDN_B1_CHUNKS - 1 - step
                r = r0s[d] + cc * c
                wqm = wqr_s[d, 2 * cc * c:(2 * cc + 2) * c, :]
                uvm = uvr_s[d, cc * c:(cc + 1) * c, :]
                attm = attr_s[d, cc * c:(cc + 1) * c, :]
                grow = cs_s[pl.ds(r + last[d], 1), :]
                for h in range(N_HEADS):
                    hd = d * N_HEADS + h
                    kt = kttr_s[d, cc * gw + h * HEAD_DIM:cc * gw + (h + 1) * HEAD_DIM, :]
                    probs.append(dict(d=d, hd=hd, r=r, wqm=wqm[:, heads[h]],
                                      uv=uvm[:, heads[h]], att=attm[:, heads[h]], kt=kt,
                                      gl=jnp.exp(grow[:, hd:hd + 1])))
            for pr in probs:
                pr['wq'] = jnp.dot(pr['wqm'], states[pr['hd']].astype(BF16), preferred_element_type=F32)
            yield
            for pr in probs:
                vn = (pr['uv'] - pr['wq'][0:c, :]).astype(BF16)
                pr['o'] = pr['wq'][c:2 * c, :] + jnp.dot(pr['att'], vn, preferred_element_type=F32)
                states[pr['hd']] = (states[pr['hd']] * pr['gl']
                                    + jnp.dot(pr['kt'], vn, preferred_element_type=F32))
            for d, o_s in ((0, of_s), (1, ob_s)):
                sel = [pr for pr in probs if pr['d'] == d]
                o_s[pl.ds(sel[0]['r'], c), :] = cat_heads([pr['o'] for pr in sel])
            yield
        for hd in range(2 * N_HEADS):
            st_s[hd] = states[hd]

    def run_interleaved(gens):
        live = list(gens)
        while live:
            still = []
            for g in live:
                try:
                    next(g)
                    still.append(g)
                except StopIteration:
                    pass
            live = still

    st_s[...] = jnp.zeros(st_s.shape, F32)
    run_interleaved([b1_gen(0, 0)])

    def pipe_body(t, carry):
        j = 1 + 2 * t
        run_interleaved([b1_gen(j, 1), b2_gen(j - 1, 0)])
        run_interleaved([b1_gen(j + 1, 0), b2_gen(j, 1)])
        return carry

    lax.fori_loop(0, (n_steps - 1) // 2, pipe_body, 0)
    if (n_steps - 1) % 2:
        run_interleaved([b1_gen(n_steps - 1, (n_steps - 1) % 2), b2_gen(n_steps - 2, n_steps % 2)])
    run_interleaved([b2_gen(n_steps - 1, (n_steps - 1) % 2)])

    for r0, _, _ in _row_blocks(lc, total):
        o = of_s[r0:r0 + ROW_BLOCK, :] + ob_s[r0:r0 + ROW_BLOCK, :]
        ms = _split_dot(o * o, ones_bd, 2) * (1.0 / HEAD_DIM)
        gate = u_ref[0, r0:r0 + ROW_BLOCK, 3 * gw:4 * gw]
        y = o * lax.rsqrt(ms + NORM_EPS) * ng_ref[...] * gate
        o_ref[0, r0:r0 + ROW_BLOCK, :] = y.astype(BF16)


def _gdn_params(a_log, dt_bias, norm_g):
    depth = a_log.shape[0]
    pad = ((0, 0), (0, 0), (0, W_AB - 2 * N_HEADS))
    avec = jnp.pad(jnp.exp(a_log.reshape(depth, 1, 2 * N_HEADS)), pad)
    dtb = jnp.pad(dt_bias.reshape(depth, 1, 2 * N_HEADS), pad)
    ng = jnp.tile(norm_g.reshape(depth, 1, HEAD_DIM), (1, 1, N_HEADS))
    return avec, dtb, ng


def _gdn(u, ab, avec, dtb, ng, l, *, lc):
    nb, total, _ = u.shape
    nch = total // GDN_CHUNK
    blk = GDN_B1_CHUNKS * GDN_CHUNK
    return pl.pallas_call(
        functools.partial(_gdn_body, lc=lc),
        grid=(nb,),
        in_specs=[pl.BlockSpec((1, total, W_GDN), lambda b: (b, 0, 0)),
                  pl.BlockSpec((1, total, W_AB), lambda b: (b, 0, 0)),
                  _layer(avec, l), _layer(dtb, l), _layer(ng, l)],
        out_specs=pl.BlockSpec((1, total, GROUP_W), lambda b: (b, 0, 0)),
        out_shape=jax.ShapeDtypeStruct((nb, total, GROUP_W), BF16),
        scratch_shapes=[pltpu.VMEM((total, GROUP_W), F32),
                        pltpu.VMEM((total, GROUP_W), F32),
                        pltpu.VMEM((total, GROUP_W), F32),
                        pltpu.VMEM((total, W_AB), F32),
                        pltpu.VMEM((nch * SUBLANES, GDN_CHUNK), F32),
                        pltpu.VMEM((total, GROUP_W), F32),
                        pltpu.VMEM((total, GROUP_W), F32),
                        pltpu.VMEM((2 * N_HEADS, HEAD_DIM, HEAD_DIM), F32)] + 2 * [
                        pltpu.VMEM((2, blk, GROUP_W), F32),
                        pltpu.VMEM((2, 2 * blk, GROUP_W), BF16),
                        pltpu.VMEM((2, blk, GROUP_W), BF16),
                        pltpu.VMEM((2, GDN_B1_CHUNKS * GROUP_W, GDN_CHUNK), BF16)],
        compiler_params=_cparams(("arbitrary",)),
        name="gdn",
    )(u, ab, avec, dtb, ng)


def _pack_w_in(w_in):
    gw = GROUP_W
    o_gdn = 2 * gw
    o_ab = o_gdn + 4 * gw
    o_ret = o_ab + 4 * N_HEADS
    o_swa = o_ret + 4 * gw
    end = o_swa + W_SWA
    assert end == w_in.shape[-1]
    wb = w_in.astype(BF16)
    pad = jnp.zeros(wb.shape[:-1] + (W_AB - 4 * N_HEADS,), BF16)
    return jnp.concatenate([wb[..., 0:o_ab], wb[..., o_ret:end], wb[..., o_ab:o_ret], pad], axis=-1)


def _rope_tables(ang, lc):
    cos = jnp.cos(ang)
    sin = jnp.sin(ang)
    cos_h = jnp.concatenate([cos, cos], axis=-1)
    sin_h = jnp.concatenate([-sin, sin], axis=-1)
    cos_t = jnp.tile(cos_h, (1, N_HEADS))
    sin_t = jnp.tile(sin_h, (1, N_HEADS))
    cos_t = jnp.concatenate([jnp.ones((lc, GROUP_W), F32), cos_t], axis=0)
    sin_t = jnp.concatenate([jnp.zeros((lc, GROUP_W), F32), sin_t], axis=0)
    return cos_t, sin_t


def _rope_freqs(pos, n):
    inv = ROPE_BASE ** (-jnp.arange(0, n, 2, dtype=F32) / n)
    return pos[:, None] * inv[None, :]


def kernel(x, c, ctx, c_ctx, w_mod, b_mod, pre_norm_g, post_norm_g, w_in, w_out, lru_conv_w, lru_conv_b, lru_w_r, lru_b_r, lru_w_i, lru_b_i, lru_lambda, gdn_conv_w, gdn_a_log, gdn_dt_bias, gdn_norm_g, ret_decay_logit, swa_sink):
    nb, t, d = x.shape
    lc = ctx.shape[1]
    depth = w_mod.shape[0]
    assert t % ROW_BLOCK == 0 and lc % ROW_BLOCK == 0 and d == 4 * GROUP_W

    rows = t // GRID_W
    row = jnp.repeat(jnp.arange(rows, dtype=F32), GRID_W)
    col = jnp.tile(jnp.arange(GRID_W, dtype=F32), rows)
    ang2d = jnp.concatenate([_rope_freqs(row, HEAD_DIM // 2), _rope_freqs(col, HEAD_DIM // 2)], axis=-1)
    ang1d = _rope_freqs(jnp.arange(t, dtype=F32), HEAD_DIM)
    cos1, sin1 = _rope_tables(ang1d, lc)
    cos2, sin2 = _rope_tables(ang2d, lc)

    mod_rows = -(-(nb + 1) // SUBLANES) * SUBLANES
    s_rows = jnp.concatenate([c, c_ctx[None, :], jnp.zeros((mod_rows - nb - 1, d), F32)], axis=0)
    mod = _modulation(s_rows, w_mod, b_mod)

    w_in_p = _pack_w_in(w_in)
    w_out_b = w_out.astype(BF16)
    pre_g = pre_norm_g.reshape(depth, 1, d)
    post_g = post_norm_g.reshape(depth, 1, d)
    lru_cb, lru_wg, lru_bg, lru_lam = _lru_params(lru_conv_b, lru_w_r, lru_b_r, lru_w_i, lru_b_i, lru_lambda)
    gdn_avec, gdn_dtb, gdn_ng = _gdn_params(gdn_a_log, gdn_dt_bias, gdn_norm_g)
    ret_lg = jnp.repeat(ret_decay_logit, HEAD_DIM, axis=-1)
    sink = swa_sink.reshape(depth * N_HEADS)
    xs = jnp.concatenate([ctx, x], axis=1)
    for l in range(depth):
        u_lru, u_gdn, u_ret, u_swa, u_ab = _in_proj(xs, mod, pre_g, w_in_p, (cos1, sin1, cos2, sin2),
                                                    (lru_conv_w, lru_cb, gdn_conv_w), l, lc=lc)
        ya = _lru(u_lru, lru_wg, lru_bg, lru_lam, l, lc=lc)
        yb = _gdn(u_gdn, u_ab, gdn_avec, gdn_dtb, gdn_ng, l, lc=lc)
        yc = _ret(u_ret, ret_lg, l, lc=lc)
        yd = _swa(u_swa, sink, l, lc=lc)
        xs = _out_proj((ya, yb, yc, yd), xs, mod, post_g, w_out_b, l, lc=lc)
    return xs[:, lc:, :]
```

```python
import functools

import jax
import jax.numpy as jnp
from jax import lax
from jax.experimental import pallas as pl
from jax.experimental.pallas import tpu as pltpu

F32 = jnp.float32
BF16 = jnp.bfloat16

HEAD_DIM = 64
GROUP_W = 256
N_HEADS = GROUP_W // HEAD_DIM
NORM_EPS = 1e-6
ROPE_BASE = 10000.0
NEG_INF = -1e30
GRID_W = 64
CONV_W = 4
LRU_C = 8.0
LRU_SCAN_UNROLL = 4
GDN_CHUNK = 64
GDN_B1_CHUNKS = 4
RET_CHUNK = 256
SWA_KV_HEADS = 2
SWA_KV_W = SWA_KV_HEADS * HEAD_DIM
WINDOW = 128
SWA_BLOCKS_PER_STEP = 2
ROW_BLOCK = 256
CONV_ROWS = 64
PROJ_ROWS_MAX = 576
SUBLANES = 8
VMEM_LIMIT_BYTES = 56 * 1024 * 1024

W_LRU = 2 * GROUP_W
W_GDN = 4 * GROUP_W
W_RET = 4 * GROUP_W
W_SWA = 2 * GROUP_W + 2 * SWA_KV_W
W_AB = 128
IN_W_PACKED = W_LRU + W_GDN + W_RET + W_SWA + W_AB


def _cparams(sem, flags=None):
    return pltpu.CompilerParams(dimension_semantics=sem, vmem_limit_bytes=VMEM_LIMIT_BYTES, flags=flags)


def _dot(a, b):
    return jnp.dot(a.astype(BF16), b.astype(BF16), preferred_element_type=F32)


def _dot_nt(a, b):
    return lax.dot_general(a.astype(BF16), b.astype(BF16), (((1,), (1,)), ((), ())),
                           preferred_element_type=F32)


def _dot_tn(a, b):
    return lax.dot_general(a.astype(BF16), b.astype(BF16), (((0,), (0,)), ((), ())),
                           preferred_element_type=F32)


def _split_dot(x, w, parts):
    acc = None
    r = x
    for _ in range(parts):
        hi = r.astype(BF16)
        t = jnp.dot(hi, w, preferred_element_type=F32)
        acc = t if acc is None else acc + t
        r = r - hi.astype(F32)
    return acc


def _split_dot_left(w, x, parts):
    acc = None
    r = x
    for _ in range(parts):
        hi = r.astype(BF16)
        t = jnp.dot(w, hi, preferred_element_type=F32)
        acc = t if acc is None else acc + t
        r = r - hi.astype(F32)
    return acc


def _sigmoid(x):
    return 0.5 * jnp.tanh(0.5 * x) + 0.5


def _silu(x):
    return x * _sigmoid(x)


def _softplus(x):
    return jnp.maximum(x, 0.0) + jnp.log1p(jnp.exp(-jnp.abs(x)))


def _head_ones():
    r = lax.broadcasted_iota(jnp.int32, (GROUP_W, GROUP_W), 0) // HEAD_DIM
    c = lax.broadcasted_iota(jnp.int32, (GROUP_W, GROUP_W), 1) // HEAD_DIM
    return (r == c).astype(BF16)


def _row_blocks(lc, total):
    out = []
    for r0 in range(0, total, ROW_BLOCK):
        out.append((r0, 0, lc) if r0 < lc else (r0, lc, total))
    return out


def _mod_body(s_ref, w_ref, b_ref, o_ref):
    s = _silu(s_ref[...])
    o_ref[0] = _dot(s, w_ref[0]) + b_ref[0]


def _modulation(s_rows, w_mod, b_mod):
    depth, d, d3 = w_mod.shape
    rows = s_rows.shape[0]
    nt = d3 // d
    return pl.pallas_call(
        _mod_body,
        grid=(depth, nt),
        in_specs=[pl.BlockSpec((rows, d), lambda l, j: (0, 0)),
                  pl.BlockSpec((1, d, d), lambda l, j: (l, 0, j)),
                  pl.BlockSpec((1, 1, d), lambda l, j: (l, 0, j))],
        out_specs=pl.BlockSpec((1, rows, d), lambda l, j: (l, 0, j)),
        out_shape=jax.ShapeDtypeStruct((depth, rows, d3), F32),
        compiler_params=_cparams(("arbitrary", "arbitrary")),
        name="modulation",
    )(s_rows, w_mod, b_mod.reshape(depth, 1, d3))


def _proj_rows(total):
    for tm in range(PROJ_ROWS_MAX, SUBLANES - 1, -SUBLANES):
        if total % tm == 0:
            return tm
    raise ValueError(total)


def _mod_rows(mod_ref, b, i, tm, lc, nb, c0, c1):
    lat = mod_ref[pl.ds(b, 1), c0:c1]
    if lc % tm == 0:
        ctx = mod_ref[nb:nb + 1, c0:c1]
        return jnp.where(i * tm < lc, ctx, lat)
    ctx = mod_ref[nb:nb + 1, c0:c1]
    row = i * tm + lax.broadcasted_iota(jnp.int32, (tm, 1), 0)
    return jnp.where(row < lc, ctx, lat)


def _inproj_body(x_ref, xp_ref, xn_ref, mod_ref, g_ref, w_ref, cos1_ref, sin1_ref, cos2_ref, sin2_ref,
                 lcw_ref, lcb_ref, gcw_ref, o_lru, o_gdn, o_ret, o_swa, o_ab, *conv_scr, lc, nb, total):
    i = pl.program_id(0)
    b = pl.program_id(1)
    d = x_ref.shape[2]
    tm = x_ref.shape[1]
    gw = GROUP_W
    scale = HEAD_DIM ** -0.5

    def prenorm(xv, m):
        ms = jnp.mean(xv * xv, axis=-1, keepdims=True)
        y = xv * lax.rsqrt(ms + NORM_EPS) * g_ref[...]
        return (y * (1.0 + m[:, d:2 * d]) + m[:, 0:d]).astype(BF16)

    h = prenorm(x_ref[0], _mod_rows(mod_ref, b, i, tm, lc, nb, 0, 2 * d))
    halo = prenorm(jnp.concatenate([xp_ref[0], xn_ref[0]], axis=0), mod_ref[pl.ds(b, 1), 0:2 * d])
    h_scr = conv_scr[4]
    h_scr[0:tm, :] = h
    h_scr[tm:tm + 2 * SUBLANES, :] = halo
    h_all = slice(0, tm + 2 * SUBLANES)

    def proj(c0, width, rows=slice(0, tm)):
        return jnp.dot(h_scr[rows, :], w_ref[:, c0:c0 + width], preferred_element_type=F32)

    nt = total // tm
    b_tile, b_row = lc // tm, lc % tm
    assert b_row % SUBLANES == 0 and (b_row == 0 or SUBLANES <= b_row <= tm - SUBLANES)
    prev_ok = i > 0
    next_ok = i < nt - 1
    if b_row == 0:
        prev_ok = prev_ok & (i != b_tile)
        next_ok = next_ok & (i != b_tile - 1)
    prev_f = jnp.where(prev_ok, 1.0, 0.0)
    next_f = jnp.where(next_ok, 1.0, 0.0)

    def dwconv_to(o_ref, oc0, u_all, scr, cw_ref, c0, post):
        scr[0:SUBLANES, :] = u_all[tm:tm + SUBLANES] * prev_f
        scr[SUBLANES:SUBLANES + tm, :] = u_all[0:tm]
        scr[SUBLANES + tm:2 * SUBLANES + tm, :] = u_all[tm + SUBLANES:tm + 2 * SUBLANES] * next_f
        for r0 in range(0, tm, CONV_ROWS):
            n = min(CONV_ROWS, tm - r0)
            near_boundary = b_row and r0 - 1 <= b_row <= r0 + n + 1
            acc = None
            for k in range(CONV_W):
                off = k - 2
                tap = scr[SUBLANES + r0 + off:SUBLANES + r0 + off + n, :]
                if near_boundary and off != 0:
                    r = r0 + off + lax.broadcasted_iota(jnp.int32, (n, 1), 0)
                    keep = ((r >= b_row) == (r - off >= b_row)) | (i != b_tile)
                    tap = jnp.where(keep, tap, 0.0)
                term = tap * cw_ref[k:k + 1, c0:c0 + gw]
                acc = term if acc is None else acc + term
            o_ref[0, r0:r0 + n, oc0:oc0 + gw] = post(acc)

    c_gdn, c_ret, c_swa, c_ab = W_LRU, W_LRU + W_GDN, W_LRU + W_GDN + W_RET, W_LRU + W_GDN + W_RET + W_SWA
    kvw = SWA_KV_W

    def ep_lru_x(u):
        dwconv_to(o_lru, 0, u, conv_scr[0], lcw_ref, 0, lambda a: a + lcb_ref[...])

    def ep_gdn_conv(p):
        return lambda u: dwconv_to(o_gdn, p * gw, u, conv_scr[1 + p], gcw_ref, p * gw, _silu)

    def ep_gates(u):
        o_lru[0, :, gw:2 * gw] = _silu(u[:, 0:gw])
        o_gdn[0, :, 3 * gw:4 * gw] = _silu(u[:, gw:2 * gw])

    def ep_ret(u):
        cos, sin = cos1_ref[...], sin1_ref[...]
        o_ret[0, :, 0:gw] = _rope(u[:, 0:gw], cos, sin)
        o_ret[0, :, gw:2 * gw] = _rope(u[:, gw:2 * gw], cos, sin) * scale
        o_ret[0, :, 2 * gw:3 * gw] = u[:, 2 * gw:3 * gw]
        o_ret[0, :, 3 * gw:4 * gw] = _silu(u[:, 3 * gw:4 * gw])

    def ep_swa(u):
        cos, sin = cos2_ref[...], sin2_ref[...]
        o_swa[0, :, 0:gw] = _rope(u[:, 0:gw], cos, sin) * scale
        o_swa[0, :, gw:gw + kvw] = _rope(u[:, gw:gw + kvw], cos[:, 0:kvw], sin[:, 0:kvw])
        o_swa[0, :, gw + kvw:gw + 2 * kvw] = u[:, gw + kvw:gw + 2 * kvw]
        o_swa[0, :, gw + 2 * kvw:2 * gw + 2 * kvw] = _silu(u[:, gw + 2 * kvw:2 * gw + 2 * kvw])

    def ep_ab(u):
        o_ab[0] = u

    def gates_proj():
        return jnp.concatenate([proj(gw, gw), proj(c_gdn + 3 * gw, gw)], axis=1)

    work = [(lambda: proj(0, gw, h_all), ep_lru_x),
            (lambda: proj(c_ret, W_RET), ep_ret),
            (lambda: proj(c_gdn, gw, h_all), ep_gdn_conv(0)),
            (lambda: proj(c_swa, W_SWA), ep_swa),
            (lambda: proj(c_gdn + gw, gw, h_all), ep_gdn_conv(1)),
            (gates_proj, ep_gates),
            (lambda: proj(c_gdn + 2 * gw, gw, h_all), ep_gdn_conv(2)),
            (lambda: proj(c_ab, W_AB), ep_ab)]
    pending = work[0][0]()
    for k, (_, epilogue) in enumerate(work):
        cur = pending
        if k + 1 < len(work):
            pending = work[k + 1][0]()
        epilogue(cur)


def _in_proj(x, mod, g, w, tables, conv_params, l, *, lc):
    nb, total, d = x.shape
    tm = _proj_rows(total)
    nt = total // tm
    per = tm // SUBLANES
    last = total // SUBLANES - 1
    widths = (W_LRU, W_GDN, W_RET, W_SWA, W_AB)
    tspec = pl.BlockSpec((tm, GROUP_W), lambda i, b: (i, 0))
    return pl.pallas_call(
        functools.partial(_inproj_body, lc=lc, nb=nb, total=total),
        grid=(nt, nb),
        in_specs=[pl.BlockSpec((1, tm, d), lambda i, b: (b, i, 0)),
                  pl.BlockSpec((1, SUBLANES, d), lambda i, b: (b, jnp.maximum(i * per - 1, 0), 0)),
                  pl.BlockSpec((1, SUBLANES, d), lambda i, b: (b, jnp.minimum((i + 1) * per, last), 0)),
                  _layer(mod, l), _layer(g, l), _layer(w, l), tspec, tspec, tspec, tspec]
                 + [_layer(p, l) for p in conv_params],
        out_specs=[pl.BlockSpec((1, tm, wd), lambda i, b: (b, i, 0)) for wd in widths],
        out_shape=[jax.ShapeDtypeStruct((nb, total, wd), F32) for wd in widths],
        scratch_shapes=[pltpu.VMEM((tm + 2 * SUBLANES, GROUP_W), F32) for _ in range(4)]
                       + [pltpu.VMEM((tm + 2 * SUBLANES, d), BF16)],
        compiler_params=_cparams(("arbitrary", "arbitrary")),
        name="in_proj",
    )(x, x, x, mod, g, w, *tables, *conv_params)


def _outproj_body(ya, yb, yc, yd, x_ref, mod_ref, g_ref, w_ref, o_ref, *, lc, nb):
    b = pl.program_id(0)
    i = pl.program_id(1)
    d = x_ref.shape[2]
    acc = None
    for k, y_ref in enumerate((ya, yb, yc, yd)):
        t = jnp.dot(y_ref[0], w_ref[k * GROUP_W:(k + 1) * GROUP_W, :], preferred_element_type=F32)
        acc = t if acc is None else acc + t
    ms = jnp.mean(acc * acc, axis=-1, keepdims=True)
    yn = acc * lax.rsqrt(ms + NORM_EPS) * g_ref[...]
    gate = _mod_rows(mod_ref, b, i, acc.shape[0], lc, nb, 2 * d, 3 * d)
    o_ref[0] = x_ref[0] + gate * yn


def _out_proj(ys, x, mod, g, w, l, *, lc):
    nb, total, d = x.shape
    tm = _proj_rows(total)
    nt = total // tm
    yspec = pl.BlockSpec((1, tm, GROUP_W), lambda b, i: (b, i, 0))
    return pl.pallas_call(
        functools.partial(_outproj_body, lc=lc, nb=nb),
        grid=(nb, nt),
        in_specs=[yspec, yspec, yspec, yspec,
                  pl.BlockSpec((1, tm, d), lambda b, i: (b, i, 0)),
                  _layer(mod, l), _layer(g, l), _layer(w, l)],
        out_specs=pl.BlockSpec((1, tm, d), lambda b, i: (b, i, 0)),
        out_shape=jax.ShapeDtypeStruct(x.shape, F32),
        compiler_params=_cparams(("arbitrary", "arbitrary")),
        name="out_proj",
    )(*ys, x, mod, g, w)


def _lru_scan(a_s, b_s, h_s, tile_lo, n_tiles, carry, *, rev, accumulate):
    row = lax.broadcasted_iota(jnp.int32, (SUBLANES, GROUP_W), 0)

    def body(j, carry):
        t = tile_lo + (n_tiles - 1 - j if rev else j)
        r = pl.multiple_of(t * SUBLANES, SUBLANES)
        a = a_s[pl.ds(r, SUBLANES), :]
        b = b_s[pl.ds(r, SUBLANES), :]
        for s in (1, 2, 4):
            if rev:
                ra = pltpu.roll(a, SUBLANES - s, 0)
                rb = pltpu.roll(b, SUBLANES - s, 0)
                m = row < SUBLANES - s
            else:
                ra = pltpu.roll(a, s, 0)
                rb = pltpu.roll(b, s, 0)
                m = row >= s
            b = a * jnp.where(m, rb, 0.0) + b
            a = a * jnp.where(m, ra, 1.0)
        h = a * carry + b
        if accumulate:
            h_s[pl.ds(r, SUBLANES), :] = h_s[pl.ds(r, SUBLANES), :] + h
        else:
            h_s[pl.ds(r, SUBLANES), :] = h
        return h[0:1, :] if rev else h[SUBLANES - 1:SUBLANES, :]

    return lax.fori_loop(0, n_tiles, body, carry, unroll=LRU_SCAN_UNROLL)


def _lru_body(u_ref, wg_ref, bg_ref, lam_ref, o_ref, a_s, b_s, h_s, *, lc):
    total = u_ref.shape[1]
    blocks = _row_blocks(lc, total)
    zero = jnp.zeros((1, GROUP_W), F32)
    for d in range(2):
        sp = _softplus(-lam_ref[d])
        for r0, _, _ in blocks:
            uc = u_ref[0, r0:r0 + ROW_BLOCK, 0:GROUP_W]
            gts = _dot(uc, wg_ref[d]) + bg_ref[d]
            r = _sigmoid(gts[:, 0:GROUP_W])
            ig = _sigmoid(gts[:, GROUP_W:2 * GROUP_W])
            a = jnp.exp(-LRU_C * r * sp)
            a_s[r0:r0 + ROW_BLOCK, :] = a
            b_s[r0:r0 + ROW_BLOCK, :] = jnp.sqrt(1.0 - a * a) * (ig * uc)
        ct, tt = lc // SUBLANES, total // SUBLANES
        if d == 0:
            _lru_scan(a_s, b_s, h_s, 0, tt, zero, rev=False, accumulate=False)
        else:
            carry = _lru_scan(a_s, b_s, h_s, 0, ct, zero, rev=True, accumulate=True)
            _lru_scan(a_s, b_s, h_s, ct, tt - ct, carry, rev=True, accumulate=True)
    for r0, _, _ in blocks:
        gate = u_ref[0, r0:r0 + ROW_BLOCK, GROUP_W:2 * GROUP_W]
        o_ref[0, r0:r0 + ROW_BLOCK, :] = (h_s[r0:r0 + ROW_BLOCK, :] * gate).astype(BF16)


def _block_diag(w):
    n, c = w.shape[-3], w.shape[-1]
    eye = jnp.eye(n, dtype=w.dtype)
    return (eye[:, None, :, None] * w[..., :, :, None, :]).reshape(w.shape[:-3] + (n * c, n * c))


def _layer(arr, l):
    shape = arr.shape[1:]
    return pl.BlockSpec((None,) + shape, lambda *_: (l,) + (0,) * len(shape))


def _lru_params(conv_b, w_r, b_r, w_i, b_i, lam):
    depth = conv_b.shape[0]
    wg = jnp.concatenate([_block_diag(w_r), _block_diag(w_i)], axis=-1).astype(BF16)
    bg = jnp.concatenate([b_r, b_i], axis=-1).reshape(depth, 2, 1, 2 * GROUP_W)
    return conv_b.reshape(depth, 1, GROUP_W), wg, bg, lam.reshape(depth, 2, 1, GROUP_W)


def _lru(u, wg, bg, lam, l, *, lc):
    nb, total, _ = u.shape
    return pl.pallas_call(
        functools.partial(_lru_body, lc=lc),
        grid=(nb,),
        in_specs=[pl.BlockSpec((1, total, W_LRU), lambda b: (b, 0, 0)),
                  _layer(wg, l), _layer(bg, l), _layer(lam, l)],
        out_specs=pl.BlockSpec((1, total, GROUP_W), lambda b: (b, 0, 0)),
        out_shape=jax.ShapeDtypeStruct((nb, total, GROUP_W), BF16),
        scratch_shapes=[pltpu.VMEM((total, GROUP_W), F32) for _ in range(3)],
        compiler_params=_cparams(("arbitrary",)),
        name="lru",
    )(u, wg, bg, lam)


def _rope(x, cos, sin_signed):
    half = HEAD_DIM // 2
    outs = []
    for c0 in range(0, x.shape[1], 128):
        xs = x[:, c0:c0 + 128]
        lane = lax.broadcasted_iota(jnp.int32, xs.shape, 1)
        swapped = jnp.where((lane % HEAD_DIM) < half,
                            pltpu.roll(xs, 128 - half, 1), pltpu.roll(xs, half, 1))
        outs.append(xs * cos[:, c0:c0 + 128] + swapped * sin_signed[:, c0:c0 + 128])
    return outs[0] if len(outs) == 1 else jnp.concatenate(outs, axis=1)


def _ret_body(u_ref, lg_ref, o_ref, ds_s, st_s, m_s, *, lc):
    total = u_ref.shape[1]
    c = RET_CHUNK
    nch = total // c
    n_ctx = lc // c
    gw = GROUP_W
    ones_bd = _head_ones()
    bd_mask = ones_bd > 0
    lg = -_softplus(-lg_ref[...])
    lgf, lgb = lg[0:1, :], lg[1:2, :]
    pos = lax.broadcasted_iota(jnp.int32, (c, 1), 0).astype(F32)
    qdec_f = jnp.exp((pos + 1.0) * lgf)
    qdec_b = jnp.exp((c - pos) * lgb)
    kdec_f = jnp.exp((c - 1.0 - pos) * lgf)
    kdec_b = jnp.exp(pos * lgb)
    cdec_f = jnp.exp(float(c) * lgf)
    cdec_b = jnp.exp(float(c) * lgb)
    dij = (lax.broadcasted_iota(jnp.int32, (c, c), 0)
           - lax.broadcasted_iota(jnp.int32, (c, c), 1)).astype(F32)
    for h in range(N_HEADS):
        lf = lgf[:, h * HEAD_DIM:h * HEAD_DIM + 1]
        lb = lgb[:, h * HEAD_DIM:h * HEAD_DIM + 1]
        fwd = jnp.exp(jnp.maximum(dij, 0.0) * lf)
        bwd = jnp.exp(jnp.maximum(-dij, 0.0) * lb)
        m_s[h] = jnp.where(dij > 0, fwd, jnp.where(dij < 0, bwd, 2.0))

    for ci in range(nch):
        r0 = ci * c
        kr = u_ref[0, r0:r0 + c, gw:2 * gw]
        v = u_ref[0, r0:r0 + c, 2 * gw:3 * gw]
        ds_s[0, ci] = jnp.where(bd_mask, _dot_tn(kr * kdec_f, v), 0.0)
        ds_s[1, ci] = jnp.where(bd_mask, _dot_tn(kr * kdec_b, v), 0.0)

    s = jnp.zeros((gw, gw), F32)
    for ci in range(nch):
        st_s[0, ci] = s
        s = s * cdec_f + ds_s[0, ci]
    s = jnp.zeros((gw, gw), F32)
    for ci in list(range(n_ctx - 1, -1, -1)) + list(range(nch - 1, n_ctx - 1, -1)):
        st_s[1, ci] = s
        s = s * cdec_b + ds_s[1, ci]

    for ci in range(nch):
        r0 = ci * c
        qr = u_ref[0, r0:r0 + c, 0:gw]
        kr = u_ref[0, r0:r0 + c, gw:2 * gw]
        v = u_ref[0, r0:r0 + c, 2 * gw:3 * gw]
        gate = u_ref[0, r0:r0 + c, 3 * gw:4 * gw]
        hss = [slice(h * HEAD_DIM, (h + 1) * HEAD_DIM) for h in range(N_HEADS)]
        inter = _dot(qr * qdec_f, st_s[0, ci]) + _dot(qr * qdec_b, st_s[1, ci])
        atts = [_dot_nt(qr[:, hs], kr[:, hs]) for hs in hss]
        atts = [(a * m_s[h]).astype(BF16) for h, a in enumerate(atts)]
        o = inter + jnp.concatenate(
            [jnp.dot(a, v[:, hs].astype(BF16), preferred_element_type=F32) for a, hs in zip(atts, hss)],
            axis=1)
        mu = _split_dot(o, ones_bd, 2) * (1.0 / HEAD_DIM)
        dlt = o - mu
        var = _split_dot(dlt * dlt, ones_bd, 2) * (1.0 / HEAD_DIM)
        y = dlt * lax.rsqrt(var + NORM_EPS) * gate
        o_ref[0, r0:r0 + c, :] = y.astype(BF16)


def _ret(u, lg, l, *, lc):
    nb, total, _ = u.shape
    nch = total // RET_CHUNK
    return pl.pallas_call(
        functools.partial(_ret_body, lc=lc),
        grid=(nb,),
        in_specs=[pl.BlockSpec((1, total, W_RET), lambda b: (b, 0, 0)), _layer(lg, l)],
        out_specs=pl.BlockSpec((1, total, GROUP_W), lambda b: (b, 0, 0)),
        out_shape=jax.ShapeDtypeStruct((nb, total, GROUP_W), BF16),
        scratch_shapes=[pltpu.VMEM((2, nch, GROUP_W, GROUP_W), F32),
                        pltpu.VMEM((2, nch, GROUP_W, GROUP_W), F32),
                        pltpu.VMEM((N_HEADS, RET_CHUNK, RET_CHUNK), F32)],
        compiler_params=_cparams(("arbitrary",)),
        name="retention",
    )(u, lg)


def _swa_body(sink_ref, u_ref, o_ref, k_s, v_s, *, lc, sink0):
    total = u_ref.shape[1]
    t_lat = total - lc
    gw = GROUP_W
    blk = WINDOW
    nblk = t_lat // blk
    kv0 = lc + blk
    zpad = jnp.zeros((blk, SWA_KV_W), BF16)
    for s_ref in (k_s, v_s):
        s_ref[lc:lc + blk, :] = zpad
        s_ref[kv0 + t_lat:kv0 + t_lat + blk, :] = zpad
    for r0, _, _ in _row_blocks(lc, total):
        dst = r0 if r0 < lc else r0 + blk
        k_s[dst:dst + ROW_BLOCK, :] = u_ref[0, r0:r0 + ROW_BLOCK, gw:gw + SWA_KV_W].astype(BF16)
        v_s[dst:dst + ROW_BLOCK, :] = (
            u_ref[0, r0:r0 + ROW_BLOCK, gw + SWA_KV_W:gw + 2 * SWA_KV_W].astype(BF16))

    grp = N_HEADS // SWA_KV_HEADS

    def attend(items):
        scores = [[_dot_nt(q2, kk) for kk in keys] for q2, keys, _, _, _ in items]
        exps, dens = [], []
        for (_, _, _, masks, sink_col), sc in zip(items, scores):
            sc = [s if mk is None else jnp.where(mk, s, NEG_INF) for s, mk in zip(sc, masks)]
            mx = sink_col
            for s in sc:
                mx = jnp.maximum(mx, jnp.max(s, axis=-1, keepdims=True))
            es = [jnp.exp(s - mx) for s in sc]
            den = jnp.exp(sink_col - mx)
            for e in es:
                den = den + jnp.sum(e, axis=-1, keepdims=True)
            exps.append([e.astype(BF16) for e in es])
            dens.append(den)
        outs = []
        for (_, _, vals, _, _), es, den in zip(items, exps, dens):
            acc = None
            for e, vv in zip(es, vals):
                t = jnp.dot(e, vv.astype(BF16), preferred_element_type=F32)
                acc = t if acc is None else acc + t
            outs.append(acc / den)
        return outs

    def sink_column(hk, rows_per_head):
        row = lax.broadcasted_iota(jnp.int32, (grp * rows_per_head, 1), 0)
        col = jnp.full((grp * rows_per_head, 1), sink_ref[sink0 + hk * grp], F32)
        for g in range(1, grp):
            col = jnp.where(row >= g * rows_per_head, sink_ref[sink0 + hk * grp + g], col)
        return col

    kvs = [slice(hk * HEAD_DIM, (hk + 1) * HEAD_DIM) for hk in range(SWA_KV_HEADS)]

    def stack_q(hk, rows):
        return jnp.concatenate(
            [u_ref[0, rows, (hk * grp + g) * HEAD_DIM:(hk * grp + g + 1) * HEAD_DIM] for g in range(grp)],
            axis=0)

    def unstack_o(o2s, n):
        return jnp.concatenate([o2s[hk][g * n:(g + 1) * n, :]
                                for hk in range(SWA_KV_HEADS) for g in range(grp)], axis=1)

    o2s = attend([(stack_q(hk, slice(0, lc)), [k_s[0:lc, kvs[hk]]], [v_s[0:lc, kvs[hk]]], [None],
                   sink_column(hk, lc)) for hk in range(SWA_KV_HEADS)])
    gate = u_ref[0, 0:lc, gw + 2 * SWA_KV_W:2 * gw + 2 * SWA_KV_W]
    o_ref[0, 0:lc, :] = (unstack_o(o2s, lc) * gate).astype(BF16)

    qi = lax.broadcasted_iota(jnp.int32, (grp * blk, 3 * blk), 0) % blk
    kj = lax.broadcasted_iota(jnp.int32, (grp * blk, 3 * blk), 1)
    in_win = jnp.abs(kj - blk - qi) <= WINDOW

    assert nblk % SWA_BLOCKS_PER_STEP == 0

    def block_body(step, carry):
        items, qrows = [], []
        for s in range(SWA_BLOCKS_PER_STEP):
            n = step * SWA_BLOCKS_PER_STEP + s
            kpos = n * blk - blk + kj
            mask = in_win & (kpos >= 0) & (kpos < t_lat)
            qrow = pl.multiple_of(lc + n * blk, blk)
            wrow = qrow
            qrows.append(qrow)
            items += [(stack_q(hk, pl.ds(qrow, blk)),
                       [k_s[pl.ds(wrow, 3 * blk), kvs[hk]], k_s[0:lc, kvs[hk]]],
                       [v_s[pl.ds(wrow, 3 * blk), kvs[hk]], v_s[0:lc, kvs[hk]]],
                       [mask, None], sink_column(hk, blk)) for hk in range(SWA_KV_HEADS)]
        o2s = attend(items)
        for s, qrow in enumerate(qrows):
            gate = u_ref[0, pl.ds(qrow, blk), gw + 2 * SWA_KV_W:2 * gw + 2 * SWA_KV_W]
            o = unstack_o(o2s[s * SWA_KV_HEADS:(s + 1) * SWA_KV_HEADS], blk)
            o_ref[0, pl.ds(qrow, blk), :] = (o * gate).astype(BF16)
        return carry

    lax.fori_loop(0, nblk // SWA_BLOCKS_PER_STEP, block_body, 0)


def _swa(u, sink, l, *, lc):
    nb, total, _ = u.shape
    grid_spec = pltpu.PrefetchScalarGridSpec(
        num_scalar_prefetch=1,
        grid=(nb,),
        in_specs=[pl.BlockSpec((1, total, W_SWA), lambda b, s: (b, 0, 0))],
        out_specs=pl.BlockSpec((1, total, GROUP_W), lambda b, s: (b, 0, 0)),
        scratch_shapes=[pltpu.VMEM((total + 2 * WINDOW, SWA_KV_W), BF16),
                        pltpu.VMEM((total + 2 * WINDOW, SWA_KV_W), BF16)],
    )
    return pl.pallas_call(
        functools.partial(_swa_body, lc=lc, sink0=l * N_HEADS),
        grid_spec=grid_spec,
        out_shape=jax.ShapeDtypeStruct((nb, total, GROUP_W), BF16),
        compiler_params=_cparams(("arbitrary",)),
        name="swa",
    )(sink, u)


def _gdn_body(u_ref, ab_ref, avec_ref, dtb_ref, ng_ref, o_ref,
              q_s, k_s, v_s, cs_s, gt_s, of_s, ob_s, st_s, *ring_refs, lc):
    ring = (ring_refs[0:4], ring_refs[4:8])
    total = u_ref.shape[1]
    gw = GROUP_W
    c = GDN_CHUNK
    nch = total // c
    n_ctx = lc // c
    per_blk = ROW_BLOCK // c
    ones_bd = _head_ones()

    ri = lax.broadcasted_iota(jnp.int32, (ROW_BLOCK, ROW_BLOCK), 0)
    ci_ = lax.broadcasted_iota(jnp.int32, (ROW_BLOCK, ROW_BLOCK), 1)
    same_chunk = (ri // c) == (ci_ // c)
    tri_lo = (same_chunk & (ri >= ci_)).astype(BF16)
    tri_up = (same_chunk & (ri <= ci_)).astype(BF16)
    lane = lax.broadcasted_iota(jnp.int32, (ROW_BLOCK, W_AB), 1)

    def phase_a(bi, r0):
        h = u_ref[0, r0:r0 + ROW_BLOCK, 0:3 * gw]
        hq, hk, hv = h[:, 0:gw], h[:, gw:2 * gw], h[:, 2 * gw:3 * gw]
        ssq = _split_dot(hq * hq, ones_bd, 2)
        q_s[r0:r0 + ROW_BLOCK, :] = hq * lax.rsqrt(ssq + NORM_EPS) * (HEAD_DIM ** -0.5)
        ssk = _split_dot(hk * hk, ones_bd, 2)
        k_s[r0:r0 + ROW_BLOCK, :] = hk * lax.rsqrt(ssk + NORM_EPS)
        v_s[r0:r0 + ROW_BLOCK, :] = hv
        ab = ab_ref[0, r0:r0 + ROW_BLOCK, :]
        g = -avec_ref[...] * _softplus(ab + dtb_ref[...])
        g = jnp.where(lane < 2 * N_HEADS, g, 0.0)
        beta = _sigmoid(ab)
        cs_lo = _split_dot_left(tri_lo, g, 3)
        cs_up = _split_dot_left(tri_up, g, 3)
        cs = jnp.where(lane < N_HEADS, cs_lo, jnp.where(lane < 2 * N_HEADS, cs_up, beta))
        cs_s[r0:r0 + ROW_BLOCK, :] = cs
        cst = cs.T
        for cc in range(per_blk):
            row0 = (bi * per_blk + cc) * SUBLANES
            gt_s[row0:row0 + SUBLANES, :] = cst[0:SUBLANES, cc * c:(cc + 1) * c]

    def phase_a_gen(block_ids):
        for bi in block_ids:
            phase_a(bi, bi * ROW_BLOCK)
            yield

    ii = lax.broadcasted_iota(jnp.int32, (c, c), 0)
    jj = lax.broadcasted_iota(jnp.int32, (c, c), 1)
    eye = (ii == jj).astype(F32)
    n_sq = c.bit_length() - 2
    incl = ((ii >= jj), (ii <= jj))
    strict = ((ii > jj), (ii < jj))
    last = (c - 1, 0)
    heads = [slice(h * HEAD_DIM, (h + 1) * HEAD_DIM) for h in range(N_HEADS)]
    def expander(n_terms, lane0):
        src = lax.broadcasted_iota(jnp.int32, (n_terms * W_AB, N_HEADS * 128), 0) % W_AB
        grp = lax.broadcasted_iota(jnp.int32, (n_terms * W_AB, N_HEADS * 128), 1) // 128
        return (src == lane0 + grp).astype(BF16)

    expand_g = [expander(3, d * N_HEADS) for d in range(2)]
    expand_b = [expander(2, (2 + d) * N_HEADS) for d in range(2)]

    blk_rows = GDN_B1_CHUNKS * c
    n_steps = total // blk_rows
    assert lc % blk_rows == 0 and n_steps >= 2

    def block_row0(j, d):
        if d == 0:
            return j * blk_rows
        if isinstance(j, int):
            return lc - (j + 1) * blk_rows if (j + 1) * blk_rows <= lc else total + lc - (j + 1) * blk_rows
        return pl.multiple_of(jnp.where((j + 1) * blk_rows <= lc, lc - (j + 1) * blk_rows,
                                        total + lc - (j + 1) * blk_rows), blk_rows)

    def cat_heads(items):
        return jnp.concatenate(items, axis=1)

    def b1_gen(j, slot):
        uvr_s, wqr_s, attr_s, kttr_s = ring[slot]
        probs = []
        for d in range(2):
            r0 = block_row0(j, d)
            parts = []
            rem = cs_s[pl.ds(r0, blk_rows), :]
            for _ in range(3):
                hi = rem.astype(BF16)
                parts.append(hi)
                rem = rem - hi.astype(F32)
            xp_g = jnp.dot(jnp.concatenate(parts, axis=1), expand_g[d], preferred_element_type=F32)
            xp_b = jnp.dot(jnp.concatenate(parts[0:2], axis=1), expand_b[d], preferred_element_type=F32)
            for cc in range(GDN_B1_CHUNKS):
                r = r0 + cc * c
                kc = k_s[pl.ds(r, c), :]
                qc = q_s[pl.ds(r, c), :]
                vc = v_s[pl.ds(r, c), :]
                gt = gt_s[pl.ds(pl.multiple_of((r0 // c + cc) * SUBLANES, SUBLANES), SUBLANES), :]
                sel = []
                for h in range(N_HEADS):
                    kh, qh, vh = kc[:, heads[h]], qc[:, heads[h]], vc[:, heads[h]]
                    kk = _dot_nt(kh, kh)
                    qk = _dot_nt(qh, kh)
                    col = xp_g[cc * c:(cc + 1) * c, h * 128:h * 128 + HEAD_DIM]
                    bcol = xp_b[cc * c:(cc + 1) * c, h * 128:h * 128 + HEAD_DIM]
                    rowv = gt[d * N_HEADS + h:d * N_HEADS + h + 1, :]
                    decay = jnp.where(incl[d], jnp.exp(jnp.where(incl[d], col - rowv, 0.0)), 0.0)
                    n = -jnp.where(strict[d], kk * decay, 0.0) * bcol
                    eg = jnp.exp(col)
                    glast = col[last[d]:last[d] + 1, :]
                    pr = dict(cc=cc, d=d, a=(eye - n).astype(BF16), p=eye + n,
                              rhs=jnp.concatenate([vh * bcol, kh * (bcol * eg)], axis=1).astype(BF16),
                              att=qk * decay, qe=qh * eg, kt=kh * jnp.exp(glast - col))
                    probs.append(pr)
                    sel.append(pr)
                wqr_s[d,(2 * cc + 1) * c:(2 * cc + 2) * c, :] = (
                    cat_heads([pr['qe'] for pr in sel]).astype(BF16))
                attr_s[d,cc * c:(cc + 1) * c, :] = cat_heads([pr['att'] for pr in sel]).astype(BF16)
                ktt = cat_heads([pr['kt'] for pr in sel]).T
                kttr_s[d,cc * gw:(cc + 1) * gw, :] = ktt.astype(BF16)
        yield
        for _ in range(n_sq):
            res = [eye - jnp.dot(pr['a'], pr['p'].astype(BF16), preferred_element_type=F32)
                   for pr in probs]
            yield
            for pr, rr in zip(probs, res):
                pr['p'] = pr['p'] + _dot(pr['p'], rr)
            yield
        for pr in probs:
            pr['uw'] = _dot(pr['p'], pr['rhs'])
        yield
        for d in range(2):
            for cc in range(GDN_B1_CHUNKS):
                sel = [pr for pr in probs if pr['cc'] == cc and pr['d'] == d]
                uvr_s[d,cc * c:(cc + 1) * c, :] = cat_heads([pr['uw'][:, 0:HEAD_DIM] for pr in sel])
                wqr_s[d,2 * cc * c:(2 * cc + 1) * c, :] = (
                    cat_heads([pr['uw'][:, HEAD_DIM:2 * HEAD_DIM] for pr in sel]).astype(BF16))

    def b2_gen(j, slot):
        uvr_s, wqr_s, attr_s, kttr_s = ring[slot]
        r0s = [block_row0(j, d) for d in range(2)]
        states = [st_s[hd] for hd in range(2 * N_HEADS)]
        for step in range(GDN_B1_CHUNKS):
            probs = []
            for d in range(2):
                cc = step if d == 0 else GDN_B1_CHUNKS - 1 - step
                r = r0s[d] + cc * c
                wqm = wqr_s[d, 2 * cc * c:(2 * cc + 2) * c, :]
                uvm = uvr_s[d, cc * c:(cc + 1) * c, :]
                attm = attr_s[d, cc * c:(cc + 1) * c, :]
                grow = cs_s[pl.ds(r + last[d], 1), :]
                for h in range(N_HEADS):
                    hd = d * N_HEADS + h
                    kt = kttr_s[d, cc * gw + h * HEAD_DIM:cc * gw + (h + 1) * HEAD_DIM, :]
                    probs.append(dict(d=d, hd=hd, r=r, wqm=wqm[:, heads[h]],
                                      uv=uvm[:, heads[h]], att=attm[:, heads[h]], kt=kt,
                                      gl=jnp.exp(grow[:, hd:hd + 1])))
            for pr in probs:
                pr['wq'] = jnp.dot(pr['wqm'], states[pr['hd']].astype(BF16), preferred_element_type=F32)
            yield
            for pr in probs:
                vn = (pr['uv'] - pr['wq'][0:c, :]).astype(BF16)
                pr['o'] = pr['wq'][c:2 * c, :] + jnp.dot(pr['att'], vn, preferred_element_type=F32)
                states[pr['hd']] = (states[pr['hd']] * pr['gl']
                                    + jnp.dot(pr['kt'], vn, preferred_element_type=F32))
            for d, o_s in ((0, of_s), (1, ob_s)):
                sel = [pr for pr in probs if pr['d'] == d]
                o_s[pl.ds(sel[0]['r'], c), :] = cat_heads([pr['o'] for pr in sel])
            yield
        for hd in range(2 * N_HEADS):
            st_s[hd] = states[hd]

    def run_interleaved(gens):
        live = list(gens)
        while live:
            still = []
            for g in live:
                try:
                    next(g)
                    still.append(g)
                except StopIteration:
                    pass
            live = still

    assert blk_rows == ROW_BLOCK
    n_blocks = total // ROW_BLOCK
    n_pre = lc // ROW_BLOCK
    for bi in range(n_pre):
        phase_a(bi, bi * ROW_BLOCK)
    st_s[...] = jnp.zeros(st_s.shape, F32)
    run_interleaved([b1_gen(0, 0), phase_a_gen(range(n_pre, n_blocks))])

    def pipe_body(t, carry):
        j = 1 + 2 * t
        run_interleaved([b1_gen(j, 1), b2_gen(j - 1, 0)])
        run_interleaved([b1_gen(j + 1, 0), b2_gen(j, 1)])
        return carry

    lax.fori_loop(0, (n_steps - 1) // 2, pipe_body, 0)
    if (n_steps - 1) % 2:
        run_interleaved([b1_gen(n_steps - 1, (n_steps - 1) % 2), b2_gen(n_steps - 2, n_steps % 2)])
    def finish(r0):
        o = of_s[r0:r0 + ROW_BLOCK, :] + ob_s[r0:r0 + ROW_BLOCK, :]
        ms = _split_dot(o * o, ones_bd, 2) * (1.0 / HEAD_DIM)
        gate = u_ref[0, r0:r0 + ROW_BLOCK, 3 * gw:4 * gw]
        y = o * lax.rsqrt(ms + NORM_EPS) * ng_ref[...] * gate
        o_ref[0, r0:r0 + ROW_BLOCK, :] = y.astype(BF16)

    def finish_gen(rows):
        for r0 in rows:
            finish(r0)
            yield

    last_rows = [block_row0(n_steps - 1, d) for d in range(2)]
    early = [bi * ROW_BLOCK for bi in range(n_blocks) if bi * ROW_BLOCK not in last_rows]
    run_interleaved([b2_gen(n_steps - 1, (n_steps - 1) % 2), finish_gen(early)])
    for r0 in sorted(set(last_rows)):
        finish(r0)


def _gdn_params(a_log, dt_bias, norm_g):
    depth = a_log.shape[0]
    pad = ((0, 0), (0, 0), (0, W_AB - 2 * N_HEADS))
    avec = jnp.pad(jnp.exp(a_log.reshape(depth, 1, 2 * N_HEADS)), pad)
    dtb = jnp.pad(dt_bias.reshape(depth, 1, 2 * N_HEADS), pad)
    ng = jnp.tile(norm_g.reshape(depth, 1, HEAD_DIM), (1, 1, N_HEADS))
    return avec, dtb, ng


def _gdn(u, ab, avec, dtb, ng, l, *, lc):
    nb, total, _ = u.shape
    nch = total // GDN_CHUNK
    blk = GDN_B1_CHUNKS * GDN_CHUNK
    return pl.pallas_call(
        functools.partial(_gdn_body, lc=lc),
        grid=(nb,),
        in_specs=[pl.BlockSpec((1, total, W_GDN), lambda b: (b, 0, 0)),
                  pl.BlockSpec((1, total, W_AB), lambda b: (b, 0, 0)),
                  _layer(avec, l), _layer(dtb, l), _layer(ng, l)],
        out_specs=pl.BlockSpec((1, total, GROUP_W), lambda b: (b, 0, 0)),
        out_shape=jax.ShapeDtypeStruct((nb, total, GROUP_W), BF16),
        scratch_shapes=[pltpu.VMEM((total, GROUP_W), F32),
                        pltpu.VMEM((total, GROUP_W), F32),
                        pltpu.VMEM((total, GROUP_W), F32),
                        pltpu.VMEM((total, W_AB), F32),
                        pltpu.VMEM((nch * SUBLANES, GDN_CHUNK), F32),
                        pltpu.VMEM((total, GROUP_W), F32),
                        pltpu.VMEM((total, GROUP_W), F32),
                        pltpu.VMEM((2 * N_HEADS, HEAD_DIM, HEAD_DIM), F32)] + 2 * [
                        pltpu.VMEM((2, blk, GROUP_W), F32),
                        pltpu.VMEM((2, 2 * blk, GROUP_W), BF16),
                        pltpu.VMEM((2, blk, GROUP_W), BF16),
                        pltpu.VMEM((2, GDN_B1_CHUNKS * GROUP_W, GDN_CHUNK), BF16)],
        compiler_params=_cparams(("arbitrary",)),
        name="gdn",
    )(u, ab, avec, dtb, ng)


def _pack_w_in(w_in):
    gw = GROUP_W
    o_gdn = 2 * gw
    o_ab = o_gdn + 4 * gw
    o_ret = o_ab + 4 * N_HEADS
    o_swa = o_ret + 4 * gw
    end = o_swa + W_SWA
    assert end == w_in.shape[-1]
    wb = w_in.astype(BF16)
    pad = jnp.zeros(wb.shape[:-1] + (W_AB - 4 * N_HEADS,), BF16)
    return jnp.concatenate([wb[..., 0:o_ab], wb[..., o_ret:end], wb[..., o_ab:o_ret], pad], axis=-1)


def _rope_tables(ang, lc):
    cos = jnp.cos(ang)
    sin = jnp.sin(ang)
    cos_h = jnp.concatenate([cos, cos], axis=-1)
    sin_h = jnp.concatenate([-sin, sin], axis=-1)
    cos_t = jnp.tile(cos_h, (1, N_HEADS))
    sin_t = jnp.tile(sin_h, (1, N_HEADS))
    cos_t = jnp.concatenate([jnp.ones((lc, GROUP_W), F32), cos_t], axis=0)
    sin_t = jnp.concatenate([jnp.zeros((lc, GROUP_W), F32), sin_t], axis=0)
    return cos_t, sin_t


def _rope_freqs(pos, n):
    inv = ROPE_BASE ** (-jnp.arange(0, n, 2, dtype=F32) / n)
    return pos[:, None] * inv[None, :]


def kernel(x, c, ctx, c_ctx, w_mod, b_mod, pre_norm_g, post_norm_g, w_in, w_out, lru_conv_w, lru_conv_b, lru_w_r, lru_b_r, lru_w_i, lru_b_i, lru_lambda, gdn_conv_w, gdn_a_log, gdn_dt_bias, gdn_norm_g, ret_decay_logit, swa_sink):
    nb, t, d = x.shape
    lc = ctx.shape[1]
    depth = w_mod.shape[0]
    assert t % ROW_BLOCK == 0 and lc % ROW_BLOCK == 0 and d == 4 * GROUP_W

    rows = t // GRID_W
    row = jnp.repeat(jnp.arange(rows, dtype=F32), GRID_W)
    col = jnp.tile(jnp.arange(GRID_W, dtype=F32), rows)
    ang2d = jnp.concatenate([_rope_freqs(row, HEAD_DIM // 2), _rope_freqs(col, HEAD_DIM // 2)], axis=-1)
    ang1d = _rope_freqs(jnp.arange(t, dtype=F32), HEAD_DIM)
    cos1, sin1 = _rope_tables(ang1d, lc)
    cos2, sin2 = _rope_tables(ang2d, lc)

    mod_rows = -(-(nb + 1) // SUBLANES) * SUBLANES
    s_rows = jnp.concatenate([c, c_ctx[None, :], jnp.zeros((mod_rows - nb - 1, d), F32)], axis=0)
    mod = _modulation(s_rows, w_mod, b_mod)

    w_in_p = _pack_w_in(w_in)
    w_out_b = w_out.astype(BF16)
    pre_g = pre_norm_g.reshape(depth, 1, d)
    post_g = post_norm_g.reshape(depth, 1, d)
    lru_cb, lru_wg, lru_bg, lru_lam = _lru_params(lru_conv_b, lru_w_r, lru_b_r, lru_w_i, lru_b_i, lru_lambda)
    gdn_avec, gdn_dtb, gdn_ng = _gdn_params(gdn_a_log, gdn_dt_bias, gdn_norm_g)
    ret_lg = jnp.repeat(ret_decay_logit, HEAD_DIM, axis=-1)
    sink = swa_sink.reshape(depth * N_HEADS)
    xs = jnp.concatenate([ctx, x], axis=1)
    for l in range(depth):
        u_lru, u_gdn, u_ret, u_swa, u_ab = _in_proj(xs, mod, pre_g, w_in_p, (cos1, sin1, cos2, sin2),
                                                    (lru_conv_w, lru_cb, gdn_conv_w), l, lc=lc)
        ya = _lru(u_lru, lru_wg, lru_bg, lru_lam, l, lc=lc)
        yb = _gdn(u_gdn, u_ab, gdn_avec, gdn_dtb, gdn_ng, l, lc=lc)
        yc = _ret(u_ret, ret_lg, l, lc=lc)
        yd = _swa(u_swa, sink, l, lc=lc)
        xs = _out_proj((ya, yb, yc, yd), xs, mod, post_g, w_out_b, l, lc=lc)
    return xs[:, lc:, :]
```

```python
import functools
import math

import jax
import jax.numpy as jnp
from jax import lax
from jax.experimental import pallas as pl
from jax.experimental.pallas import tpu as pltpu

F32 = jnp.float32
BF16 = jnp.bfloat16

HEAD_DIM = 64
GROUP_W = 256
N_HEADS = GROUP_W // HEAD_DIM
NORM_EPS = 1e-6
ROPE_BASE = 10000.0
NEG_INF = -1e30
GRID_W = 64
CONV_W = 4
LRU_C = 8.0
LRU_SCAN_UNROLL = 4
GDN_CHUNK = 64
GDN_B1_CHUNKS = 4
RET_CHUNK = 256
RET_CHUNKS_PER_STEP = 5
SWA_KV_HEADS = 2
SWA_KV_W = SWA_KV_HEADS * HEAD_DIM
WINDOW = 128
SWA_BLOCKS_PER_STEP = 2
ROW_BLOCK = 256
CONV_ROWS = 64
PROJ_ROWS_MAX = 576
SUBLANES = 8
VMEM_LIMIT_BYTES = 56 * 1024 * 1024

W_LRU = 2 * GROUP_W
W_GDN = 4 * GROUP_W
W_RET = 4 * GROUP_W
W_SWA = 2 * GROUP_W + 2 * SWA_KV_W
W_AB = 128
IN_W_PACKED = W_LRU + W_GDN + W_RET + W_SWA + W_AB


def _cparams(sem, flags=None):
    return pltpu.CompilerParams(dimension_semantics=sem, vmem_limit_bytes=VMEM_LIMIT_BYTES, flags=flags)


def _dot(a, b):
    return jnp.dot(a.astype(BF16), b.astype(BF16), preferred_element_type=F32)


def _dot_nt(a, b):
    return lax.dot_general(a.astype(BF16), b.astype(BF16), (((1,), (1,)), ((), ())),
                           preferred_element_type=F32)


def _dot_tn(a, b):
    return lax.dot_general(a.astype(BF16), b.astype(BF16), (((0,), (0,)), ((), ())),
                           preferred_element_type=F32)


def _split_dot(x, w, parts):
    acc = None
    r = x
    for _ in range(parts):
        hi = r.astype(BF16)
        t = jnp.dot(hi, w, preferred_element_type=F32)
        acc = t if acc is None else acc + t
        r = r - hi.astype(F32)
    return acc


def _split_dot_left(w, x, parts):
    acc = None
    r = x
    for _ in range(parts):
        hi = r.astype(BF16)
        t = jnp.dot(w, hi, preferred_element_type=F32)
        acc = t if acc is None else acc + t
        r = r - hi.astype(F32)
    return acc


def _sigmoid(x):
    return 0.5 * jnp.tanh(0.5 * x) + 0.5


def _silu(x):
    return x * _sigmoid(x)


def _softplus(x):
    return jnp.maximum(x, 0.0) + jnp.log1p(jnp.exp(-jnp.abs(x)))


def _head_ones():
    r = lax.broadcasted_iota(jnp.int32, (GROUP_W, GROUP_W), 0) // HEAD_DIM
    c = lax.broadcasted_iota(jnp.int32, (GROUP_W, GROUP_W), 1) // HEAD_DIM
    return (r == c).astype(BF16)


def _row_blocks(lc, total):
    out = []
    for r0 in range(0, total, ROW_BLOCK):
        out.append((r0, 0, lc) if r0 < lc else (r0, lc, total))
    return out


def _mod_body(s_ref, w_ref, b_ref, o_ref):
    s = _silu(s_ref[...])
    o_ref[0] = _dot(s, w_ref[0]) + b_ref[0]


def _modulation(s_rows, w_mod, b_mod):
    depth, d, d3 = w_mod.shape
    rows = s_rows.shape[0]
    nt = d3 // d
    return pl.pallas_call(
        _mod_body,
        grid=(depth, nt),
        in_specs=[pl.BlockSpec((rows, d), lambda l, j: (0, 0)),
                  pl.BlockSpec((1, d, d), lambda l, j: (l, 0, j)),
                  pl.BlockSpec((1, 1, d), lambda l, j: (l, 0, j))],
        out_specs=pl.BlockSpec((1, rows, d), lambda l, j: (l, 0, j)),
        out_shape=jax.ShapeDtypeStruct((depth, rows, d3), F32),
        compiler_params=_cparams(("arbitrary", "arbitrary")),
        name="modulation",
    )(s_rows, w_mod, b_mod.reshape(depth, 1, d3))


def _proj_rows(total):
    for tm in range(PROJ_ROWS_MAX, SUBLANES - 1, -SUBLANES):
        if total % tm == 0:
            return tm
    raise ValueError(total)


def _mod_rows(mod_ref, b, i, tm, lc, nb, c0, c1):
    lat = mod_ref[pl.ds(b, 1), c0:c1]
    if lc % tm == 0:
        ctx = mod_ref[nb:nb + 1, c0:c1]
        return jnp.where(i * tm < lc, ctx, lat)
    ctx = mod_ref[nb:nb + 1, c0:c1]
    row = i * tm + lax.broadcasted_iota(jnp.int32, (tm, 1), 0)
    return jnp.where(row < lc, ctx, lat)


def _inproj_body(x_ref, xp_ref, xn_ref, mod_ref, g_ref, w_ref, cos1_ref, sin1_ref, cos2_ref, sin2_ref,
                 lcw_ref, lcb_ref, gcw_ref, o_lru, o_gdn, o_ret, o_swa, o_ab, *conv_scr, lc, nb, total):
    i = pl.program_id(0)
    b = pl.program_id(1)
    d = x_ref.shape[2]
    tm = x_ref.shape[1]
    gw = GROUP_W
    scale = HEAD_DIM ** -0.5

    def prenorm(xv, m):
        ms = jnp.mean(xv * xv, axis=-1, keepdims=True)
        y = xv * lax.rsqrt(ms + NORM_EPS) * g_ref[...]
        return (y * (1.0 + m[:, d:2 * d]) + m[:, 0:d]).astype(BF16)

    h = prenorm(x_ref[0], _mod_rows(mod_ref, b, i, tm, lc, nb, 0, 2 * d))
    halo = prenorm(jnp.concatenate([xp_ref[0], xn_ref[0]], axis=0), mod_ref[pl.ds(b, 1), 0:2 * d])
    h_scr = conv_scr[4]
    h_scr[0:tm, :] = h
    h_scr[tm:tm + 2 * SUBLANES, :] = halo
    h_all = slice(0, tm + 2 * SUBLANES)

    def proj(c0, width, rows=slice(0, tm)):
        return jnp.dot(h_scr[rows, :], w_ref[:, c0:c0 + width], preferred_element_type=F32)

    nt = total // tm
    b_tile, b_row = lc // tm, lc % tm
    assert b_row % SUBLANES == 0 and (b_row == 0 or SUBLANES <= b_row <= tm - SUBLANES)
    prev_ok = i > 0
    next_ok = i < nt - 1
    if b_row == 0:
        prev_ok = prev_ok & (i != b_tile)
        next_ok = next_ok & (i != b_tile - 1)
    prev_f = jnp.where(prev_ok, 1.0, 0.0)
    next_f = jnp.where(next_ok, 1.0, 0.0)

    def dwconv_to(o_ref, oc0, u_all, scr, cw_ref, c0, post):
        scr[0:SUBLANES, :] = u_all[tm:tm + SUBLANES] * prev_f
        scr[SUBLANES:SUBLANES + tm, :] = u_all[0:tm]
        scr[SUBLANES + tm:2 * SUBLANES + tm, :] = u_all[tm + SUBLANES:tm + 2 * SUBLANES] * next_f
        for r0 in range(0, tm, CONV_ROWS):
            n = min(CONV_ROWS, tm - r0)
            near_boundary = b_row and r0 - 1 <= b_row <= r0 + n + 1
            acc = None
            for k in range(CONV_W):
                off = k - 2
                tap = scr[SUBLANES + r0 + off:SUBLANES + r0 + off + n, :]
                if near_boundary and off != 0:
                    r = r0 + off + lax.broadcasted_iota(jnp.int32, (n, 1), 0)
                    keep = ((r >= b_row) == (r - off >= b_row)) | (i != b_tile)
                    tap = jnp.where(keep, tap, 0.0)
                term = tap * cw_ref[k:k + 1, c0:c0 + gw]
                acc = term if acc is None else acc + term
            o_ref[0, r0:r0 + n, oc0:oc0 + gw] = post(acc)

    c_gdn, c_ret, c_swa, c_ab = W_LRU, W_LRU + W_GDN, W_LRU + W_GDN + W_RET, W_LRU + W_GDN + W_RET + W_SWA
    kvw = SWA_KV_W

    def ep_lru_x(u):
        dwconv_to(o_lru, 0, u, conv_scr[0], lcw_ref, 0, lambda a: a + lcb_ref[...])

    def ep_gdn_conv(p):
        return lambda u: dwconv_to(o_gdn, p * gw, u, conv_scr[1 + p], gcw_ref, p * gw, _silu)

    def ep_gates(u):
        o_lru[0, :, gw:2 * gw] = _silu(u[:, 0:gw])
        o_gdn[0, :, 3 * gw:4 * gw] = _silu(u[:, gw:2 * gw])

    def ep_ret(u):
        cos, sin = cos1_ref[...], sin1_ref[...]
        o_ret[0, :, 0:gw] = _rope(u[:, 0:gw], cos, sin)
        o_ret[0, :, gw:2 * gw] = _rope(u[:, gw:2 * gw], cos, sin) * scale
        o_ret[0, :, 2 * gw:3 * gw] = u[:, 2 * gw:3 * gw]
        o_ret[0, :, 3 * gw:4 * gw] = _silu(u[:, 3 * gw:4 * gw])

    def ep_swa(u):
        cos, sin = cos2_ref[...], sin2_ref[...]
        o_swa[0, :, 0:gw] = _rope(u[:, 0:gw], cos, sin) * scale
        o_swa[0, :, gw:gw + kvw] = _rope(u[:, gw:gw + kvw], cos[:, 0:kvw], sin[:, 0:kvw])
        o_swa[0, :, gw + kvw:gw + 2 * kvw] = u[:, gw + kvw:gw + 2 * kvw]
        o_swa[0, :, gw + 2 * kvw:2 * gw + 2 * kvw] = _silu(u[:, gw + 2 * kvw:2 * gw + 2 * kvw])

    def ep_ab(u):
        o_ab[0] = u

    def gates_proj():
        return jnp.concatenate([proj(gw, gw), proj(c_gdn + 3 * gw, gw)], axis=1)

    work = [(lambda: proj(0, gw, h_all), ep_lru_x),
            (lambda: proj(c_ret, W_RET), ep_ret),
            (lambda: proj(c_gdn, gw, h_all), ep_gdn_conv(0)),
            (lambda: proj(c_swa, W_SWA), ep_swa),
            (lambda: proj(c_gdn + gw, gw, h_all), ep_gdn_conv(1)),
            (gates_proj, ep_gates),
            (lambda: proj(c_gdn + 2 * gw, gw, h_all), ep_gdn_conv(2)),
            (lambda: proj(c_ab, W_AB), ep_ab)]
    pending = work[0][0]()
    for k, (_, epilogue) in enumerate(work):
        cur = pending
        if k + 1 < len(work):
            pending = work[k + 1][0]()
        epilogue(cur)


def _in_proj(x, mod, g, w, tables, conv_params, l, *, lc):
    nb, total, d = x.shape
    tm = _proj_rows(total)
    nt = total // tm
    per = tm // SUBLANES
    last = total // SUBLANES - 1
    widths = (W_LRU, W_GDN, W_RET, W_SWA, W_AB)
    tspec = pl.BlockSpec((tm, GROUP_W), lambda i, b: (i, 0))
    return pl.pallas_call(
        functools.partial(_inproj_body, lc=lc, nb=nb, total=total),
        grid=(nt, nb),
        in_specs=[pl.BlockSpec((1, tm, d), lambda i, b: (b, i, 0)),
                  pl.BlockSpec((1, SUBLANES, d), lambda i, b: (b, jnp.maximum(i * per - 1, 0), 0)),
                  pl.BlockSpec((1, SUBLANES, d), lambda i, b: (b, jnp.minimum((i + 1) * per, last), 0)),
                  _layer(mod, l), _layer(g, l), _layer(w, l), tspec, tspec, tspec, tspec]
                 + [_layer(p, l) for p in conv_params],
        out_specs=[pl.BlockSpec((1, tm, wd), lambda i, b: (b, i, 0)) for wd in widths],
        out_shape=[jax.ShapeDtypeStruct((nb, total, wd), F32) for wd in widths],
        scratch_shapes=[pltpu.VMEM((tm + 2 * SUBLANES, GROUP_W), F32) for _ in range(4)]
                       + [pltpu.VMEM((tm + 2 * SUBLANES, d), BF16)],
        compiler_params=_cparams(("arbitrary", "arbitrary")),
        name="in_proj",
    )(x, x, x, mod, g, w, *tables, *conv_params)


def _outproj_body(ya, yb, yc, yd, x_ref, mod_ref, g_ref, w_ref, o_ref, *, lc, nb, tile0):
    b = pl.program_id(0)
    i = pl.program_id(1) + tile0
    d = x_ref.shape[2]
    acc = None
    for k, y_ref in enumerate((ya, yb, yc, yd)):
        t = jnp.dot(y_ref[0], w_ref[k * GROUP_W:(k + 1) * GROUP_W, :], preferred_element_type=F32)
        acc = t if acc is None else acc + t
    ms = jnp.mean(acc * acc, axis=-1, keepdims=True)
    yn = acc * lax.rsqrt(ms + NORM_EPS) * g_ref[...]
    gate = _mod_rows(mod_ref, b, i, acc.shape[0], lc, nb, 2 * d, 3 * d)
    o_ref[0] = x_ref[0] + gate * yn


def _out_proj(ys, x, mod, g, w, l, *, lc, latent_only=False):
    nb, total, d = x.shape
    if latent_only:
        tm = _proj_rows(math.gcd(lc, total - lc))
        tile0, rows_out = lc // tm, total - lc
    else:
        tm = _proj_rows(total)
        tile0, rows_out = 0, total
    yspec = pl.BlockSpec((1, tm, GROUP_W), lambda b, i: (b, i + tile0, 0))
    return pl.pallas_call(
        functools.partial(_outproj_body, lc=lc, nb=nb, tile0=tile0),
        grid=(nb, rows_out // tm),
        in_specs=[yspec, yspec, yspec, yspec,
                  pl.BlockSpec((1, tm, d), lambda b, i: (b, i + tile0, 0)),
                  _layer(mod, l), _layer(g, l), _layer(w, l)],
        out_specs=pl.BlockSpec((1, tm, d), lambda b, i: (b, i, 0)),
        out_shape=jax.ShapeDtypeStruct((nb, rows_out, d), F32),
        compiler_params=_cparams(("arbitrary", "arbitrary")),
        name="out_proj",
    )(*ys, x, mod, g, w)


def _lru_scan(a_s, b_s, h_s, tile_lo, n_tiles, carry, *, rev, accumulate):
    row = lax.broadcasted_iota(jnp.int32, (SUBLANES, GROUP_W), 0)

    def body(j, carry):
        t = tile_lo + (n_tiles - 1 - j if rev else j)
        r = pl.multiple_of(t * SUBLANES, SUBLANES)
        a = a_s[pl.ds(r, SUBLANES), :]
        b = b_s[pl.ds(r, SUBLANES), :]
        for s in (1, 2, 4):
            if rev:
                ra = pltpu.roll(a, SUBLANES - s, 0)
                rb = pltpu.roll(b, SUBLANES - s, 0)
                m = row < SUBLANES - s
            else:
                ra = pltpu.roll(a, s, 0)
                rb = pltpu.roll(b, s, 0)
                m = row >= s
            b = a * jnp.where(m, rb, 0.0) + b
            a = a * jnp.where(m, ra, 1.0)
        h = a * carry + b
        if accumulate:
            h_s[pl.ds(r, SUBLANES), :] = h_s[pl.ds(r, SUBLANES), :] + h
        else:
            h_s[pl.ds(r, SUBLANES), :] = h
        return h[0:1, :] if rev else h[SUBLANES - 1:SUBLANES, :]

    return lax.fori_loop(0, n_tiles, body, carry, unroll=LRU_SCAN_UNROLL)


def _lru_body(u_ref, wg_ref, bg_ref, lam_ref, o_ref, a_s, b_s, h_s, *, lc):
    total = u_ref.shape[1]
    blocks = _row_blocks(lc, total)
    zero = jnp.zeros((1, GROUP_W), F32)
    for d in range(2):
        sp = _softplus(-lam_ref[d])
        for r0, _, _ in blocks:
            uc = u_ref[0, r0:r0 + ROW_BLOCK, 0:GROUP_W]
            gts = _dot(uc, wg_ref[d]) + bg_ref[d]
            r = _sigmoid(gts[:, 0:GROUP_W])
            ig = _sigmoid(gts[:, GROUP_W:2 * GROUP_W])
            a = jnp.exp(-LRU_C * r * sp)
            a_s[r0:r0 + ROW_BLOCK, :] = a
            b_s[r0:r0 + ROW_BLOCK, :] = jnp.sqrt(1.0 - a * a) * (ig * uc)
        ct, tt = lc // SUBLANES, total // SUBLANES
        if d == 0:
            _lru_scan(a_s, b_s, h_s, 0, tt, zero, rev=False, accumulate=False)
        else:
            carry = _lru_scan(a_s, b_s, h_s, 0, ct, zero, rev=True, accumulate=True)
            _lru_scan(a_s, b_s, h_s, ct, tt - ct, carry, rev=True, accumulate=True)
    for r0, _, _ in blocks:
        gate = u_ref[0, r0:r0 + ROW_BLOCK, GROUP_W:2 * GROUP_W]
        o_ref[0, r0:r0 + ROW_BLOCK, :] = (h_s[r0:r0 + ROW_BLOCK, :] * gate).astype(BF16)


def _block_diag(w):
    n, c = w.shape[-3], w.shape[-1]
    eye = jnp.eye(n, dtype=w.dtype)
    return (eye[:, None, :, None] * w[..., :, :, None, :]).reshape(w.shape[:-3] + (n * c, n * c))


def _layer(arr, l):
    shape = arr.shape[1:]
    return pl.BlockSpec((None,) + shape, lambda *_: (l,) + (0,) * len(shape))


def _lru_params(conv_b, w_r, b_r, w_i, b_i, lam):
    depth = conv_b.shape[0]
    wg = jnp.concatenate([_block_diag(w_r), _block_diag(w_i)], axis=-1).astype(BF16)
    bg = jnp.concatenate([b_r, b_i], axis=-1).reshape(depth, 2, 1, 2 * GROUP_W)
    return conv_b.reshape(depth, 1, GROUP_W), wg, bg, lam.reshape(depth, 2, 1, GROUP_W)


def _lru(u, wg, bg, lam, l, *, lc):
    nb, total, _ = u.shape
    return pl.pallas_call(
        functools.partial(_lru_body, lc=lc),
        grid=(nb,),
        in_specs=[pl.BlockSpec((1, total, W_LRU), lambda b: (b, 0, 0)),
                  _layer(wg, l), _layer(bg, l), _layer(lam, l)],
        out_specs=pl.BlockSpec((1, total, GROUP_W), lambda b: (b, 0, 0)),
        out_shape=jax.ShapeDtypeStruct((nb, total, GROUP_W), BF16),
        scratch_shapes=[pltpu.VMEM((total, GROUP_W), F32) for _ in range(3)],
        compiler_params=_cparams(("arbitrary",)),
        name="lru",
    )(u, wg, bg, lam)


def _rope(x, cos, sin_signed):
    half = HEAD_DIM // 2
    outs = []
    for c0 in range(0, x.shape[1], 128):
        xs = x[:, c0:c0 + 128]
        lane = lax.broadcasted_iota(jnp.int32, xs.shape, 1)
        swapped = jnp.where((lane % HEAD_DIM) < half,
                            pltpu.roll(xs, 128 - half, 1), pltpu.roll(xs, half, 1))
        outs.append(xs * cos[:, c0:c0 + 128] + swapped * sin_signed[:, c0:c0 + 128])
    return outs[0] if len(outs) == 1 else jnp.concatenate(outs, axis=1)


def _ret_body(u_ref, lg_ref, o_ref, ds_s, st_s, m_s, *, lc):
    total = u_ref.shape[1]
    c = RET_CHUNK
    nch = total // c
    n_ctx = lc // c
    gw = GROUP_W
    ones_bd = _head_ones()
    bd_mask = ones_bd > 0
    lg = -_softplus(-lg_ref[...])
    lgf, lgb = lg[0:1, :], lg[1:2, :]
    pos = lax.broadcasted_iota(jnp.int32, (c, 1), 0).astype(F32)
    qdec_f = jnp.exp((pos + 1.0) * lgf)
    qdec_b = jnp.exp((c - pos) * lgb)
    kdec_f = jnp.exp((c - 1.0 - pos) * lgf)
    kdec_b = jnp.exp(pos * lgb)
    cdec_f = jnp.exp(float(c) * lgf)
    cdec_b = jnp.exp(float(c) * lgb)
    dij = (lax.broadcasted_iota(jnp.int32, (c, c), 0)
           - lax.broadcasted_iota(jnp.int32, (c, c), 1)).astype(F32)
    for h in range(N_HEADS):
        lf = lgf[:, h * HEAD_DIM:h * HEAD_DIM + 1]
        lb = lgb[:, h * HEAD_DIM:h * HEAD_DIM + 1]
        fwd = jnp.exp(jnp.maximum(dij, 0.0) * lf)
        bwd = jnp.exp(jnp.maximum(-dij, 0.0) * lb)
        m_s[h] = jnp.where(dij > 0, fwd, jnp.where(dij < 0, bwd, 2.0))

    for ci in range(nch):
        r0 = ci * c
        kr = u_ref[0, r0:r0 + c, gw:2 * gw]
        v = u_ref[0, r0:r0 + c, 2 * gw:3 * gw]
        ds_s[0, ci] = jnp.where(bd_mask, _dot_tn(kr * kdec_f, v), 0.0)
        ds_s[1, ci] = jnp.where(bd_mask, _dot_tn(kr * kdec_b, v), 0.0)

    s = jnp.zeros((gw, gw), F32)
    for ci in range(nch):
        st_s[0, ci] = s
        s = s * cdec_f + ds_s[0, ci]
    s = jnp.zeros((gw, gw), F32)
    for ci in list(range(n_ctx - 1, -1, -1)) + list(range(nch - 1, n_ctx - 1, -1)):
        st_s[1, ci] = s
        s = s * cdec_b + ds_s[1, ci]

    hss = [slice(h * HEAD_DIM, (h + 1) * HEAD_DIM) for h in range(N_HEADS)]
    for ci0 in range(0, nch, RET_CHUNKS_PER_STEP):
        cis = list(range(ci0, min(ci0 + RET_CHUNKS_PER_STEP, nch)))
        qrs = [u_ref[0, ci * c:(ci + 1) * c, 0:gw] for ci in cis]
        krs = [u_ref[0, ci * c:(ci + 1) * c, gw:2 * gw] for ci in cis]
        vs = [u_ref[0, ci * c:(ci + 1) * c, 2 * gw:3 * gw].astype(BF16) for ci in cis]
        atts = [[_dot_nt(qr[:, hs], kr[:, hs]) for hs in hss] for qr, kr in zip(qrs, krs)]
        inters = [_dot(qr * qdec_f, st_s[0, ci]) + _dot(qr * qdec_b, st_s[1, ci]) for ci, qr in zip(cis, qrs)]
        atts = [[(a * m_s[h]).astype(BF16) for h, a in enumerate(att)] for att in atts]
        os_ = [inter + jnp.concatenate(
            [jnp.dot(a, v[:, hs], preferred_element_type=F32) for a, hs in zip(att, hss)], axis=1)
            for inter, att, v in zip(inters, atts, vs)]
        mus = [_split_dot(o, ones_bd, 2) * (1.0 / HEAD_DIM) for o in os_]
        dlts = [o - mu for o, mu in zip(os_, mus)]
        vrs = [_split_dot(dlt * dlt, ones_bd, 2) * (1.0 / HEAD_DIM) for dlt in dlts]
        for ci, dlt, var in zip(cis, dlts, vrs):
            gate = u_ref[0, ci * c:(ci + 1) * c, 3 * gw:4 * gw]
            o_ref[0, ci * c:(ci + 1) * c, :] = (dlt * lax.rsqrt(var + NORM_EPS) * gate).astype(BF16)


def _ret(u, lg, l, *, lc):
    nb, total, _ = u.shape
    nch = total // RET_CHUNK
    return pl.pallas_call(
        functools.partial(_ret_body, lc=lc),
        grid=(nb,),
        in_specs=[pl.BlockSpec((1, total, W_RET), lambda b: (b, 0, 0)), _layer(lg, l)],
        out_specs=pl.BlockSpec((1, total, GROUP_W), lambda b: (b, 0, 0)),
        out_shape=jax.ShapeDtypeStruct((nb, total, GROUP_W), BF16),
        scratch_shapes=[pltpu.VMEM((2, nch, GROUP_W, GROUP_W), F32),
                        pltpu.VMEM((2, nch, GROUP_W, GROUP_W), F32),
                        pltpu.VMEM((N_HEADS, RET_CHUNK, RET_CHUNK), F32)],
        compiler_params=_cparams(("arbitrary",)),
        name="retention",
    )(u, lg)


def _swa_body(sink_ref, u_ref, o_ref, k_s, v_s, *, lc, sink0):
    total = u_ref.shape[1]
    t_lat = total - lc
    gw = GROUP_W
    blk = WINDOW
    nblk = t_lat // blk
    kv0 = lc + blk
    zpad = jnp.zeros((blk, SWA_KV_W), BF16)
    for s_ref in (k_s, v_s):
        s_ref[lc:lc + blk, :] = zpad
        s_ref[kv0 + t_lat:kv0 + t_lat + blk, :] = zpad
    for r0, _, _ in _row_blocks(lc, total):
        dst = r0 if r0 < lc else r0 + blk
        k_s[dst:dst + ROW_BLOCK, :] = u_ref[0, r0:r0 + ROW_BLOCK, gw:gw + SWA_KV_W].astype(BF16)
        v_s[dst:dst + ROW_BLOCK, :] = (
            u_ref[0, r0:r0 + ROW_BLOCK, gw + SWA_KV_W:gw + 2 * SWA_KV_W].astype(BF16))

    grp = N_HEADS // SWA_KV_HEADS

    def attend(items):
        scores = [[_dot_nt(q2, kk) for kk in keys] for q2, keys, _, _, _ in items]
        exps, dens = [], []
        for (_, _, _, masks, sink_col), sc in zip(items, scores):
            sc = [s if mk is None else jnp.where(mk, s, NEG_INF) for s, mk in zip(sc, masks)]
            mx = sink_col
            for s in sc:
                mx = jnp.maximum(mx, jnp.max(s, axis=-1, keepdims=True))
            es = [jnp.exp(s - mx) for s in sc]
            den = jnp.exp(sink_col - mx)
            for e in es:
                den = den + jnp.sum(e, axis=-1, keepdims=True)
            exps.append([e.astype(BF16) for e in es])
            dens.append(den)
        outs = []
        for (_, _, vals, _, _), es, den in zip(items, exps, dens):
            acc = None
            for e, vv in zip(es, vals):
                t = jnp.dot(e, vv.astype(BF16), preferred_element_type=F32)
                acc = t if acc is None else acc + t
            outs.append(acc / den)
        return outs

    def sink_column(hk, rows_per_head):
        row = lax.broadcasted_iota(jnp.int32, (grp * rows_per_head, 1), 0)
        col = jnp.full((grp * rows_per_head, 1), sink_ref[sink0 + hk * grp], F32)
        for g in range(1, grp):
            col = jnp.where(row >= g * rows_per_head, sink_ref[sink0 + hk * grp + g], col)
        return col

    kvs = [slice(hk * HEAD_DIM, (hk + 1) * HEAD_DIM) for hk in range(SWA_KV_HEADS)]

    def stack_q(hk, rows):
        return jnp.concatenate(
            [u_ref[0, rows, (hk * grp + g) * HEAD_DIM:(hk * grp + g + 1) * HEAD_DIM] for g in range(grp)],
            axis=0)

    def unstack_o(o2s, n):
        return jnp.concatenate([o2s[hk][g * n:(g + 1) * n, :]
                                for hk in range(SWA_KV_HEADS) for g in range(grp)], axis=1)

    o2s = attend([(stack_q(hk, slice(0, lc)), [k_s[0:lc, kvs[hk]]], [v_s[0:lc, kvs[hk]]], [None],
                   sink_column(hk, lc)) for hk in range(SWA_KV_HEADS)])
    gate = u_ref[0, 0:lc, gw + 2 * SWA_KV_W:2 * gw + 2 * SWA_KV_W]
    o_ref[0, 0:lc, :] = (unstack_o(o2s, lc) * gate).astype(BF16)

    qi = lax.broadcasted_iota(jnp.int32, (grp * blk, 3 * blk), 0) % blk
    kj = lax.broadcasted_iota(jnp.int32, (grp * blk, 3 * blk), 1)
    in_win = jnp.abs(kj - blk - qi) <= WINDOW

    assert nblk % SWA_BLOCKS_PER_STEP == 0

    def block_body(step, carry):
        items, qrows = [], []
        for s in range(SWA_BLOCKS_PER_STEP):
            n = step * SWA_BLOCKS_PER_STEP + s
            kpos = n * blk - blk + kj
            mask = in_win & (kpos >= 0) & (kpos < t_lat)
            qrow = pl.multiple_of(lc + n * blk, blk)
            wrow = qrow
            qrows.append(qrow)
            items += [(stack_q(hk, pl.ds(qrow, blk)),
                       [k_s[pl.ds(wrow, 3 * blk), kvs[hk]], k_s[0:lc, kvs[hk]]],
                       [v_s[pl.ds(wrow, 3 * blk), kvs[hk]], v_s[0:lc, kvs[hk]]],
                       [mask, None], sink_column(hk, blk)) for hk in range(SWA_KV_HEADS)]
        o2s = attend(items)
        for s, qrow in enumerate(qrows):
            gate = u_ref[0, pl.ds(qrow, blk), gw + 2 * SWA_KV_W:2 * gw + 2 * SWA_KV_W]
            o = unstack_o(o2s[s * SWA_KV_HEADS:(s + 1) * SWA_KV_HEADS], blk)
            o_ref[0, pl.ds(qrow, blk), :] = (o * gate).astype(BF16)
        return carry

    lax.fori_loop(0, nblk // SWA_BLOCKS_PER_STEP, block_body, 0)


def _swa(u, sink, l, *, lc):
    nb, total, _ = u.shape
    grid_spec = pltpu.PrefetchScalarGridSpec(
        num_scalar_prefetch=1,
        grid=(nb,),
        in_specs=[pl.BlockSpec((1, total, W_SWA), lambda b, s: (b, 0, 0))],
        out_specs=pl.BlockSpec((1, total, GROUP_W), lambda b, s: (b, 0, 0)),
        scratch_shapes=[pltpu.VMEM((total + 2 * WINDOW, SWA_KV_W), BF16),
                        pltpu.VMEM((total + 2 * WINDOW, SWA_KV_W), BF16)],
    )
    return pl.pallas_call(
        functools.partial(_swa_body, lc=lc, sink0=l * N_HEADS),
        grid_spec=grid_spec,
        out_shape=jax.ShapeDtypeStruct((nb, total, GROUP_W), BF16),
        compiler_params=_cparams(("arbitrary",)),
        name="swa",
    )(sink, u)


def _gdn_body(u_ref, ab_ref, avec_ref, dtb_ref, ng_ref, o_ref,
              q_s, k_s, v_s, cs_s, gt_s, of_s, ob_s, st_s, *ring_refs, lc):
    ring = (ring_refs[0:4], ring_refs[4:8])
    total = u_ref.shape[1]
    gw = GROUP_W
    c = GDN_CHUNK
    nch = total // c
    n_ctx = lc // c
    per_blk = ROW_BLOCK // c
    ones_bd = _head_ones()

    ri = lax.broadcasted_iota(jnp.int32, (ROW_BLOCK, ROW_BLOCK), 0)
    ci_ = lax.broadcasted_iota(jnp.int32, (ROW_BLOCK, ROW_BLOCK), 1)
    same_chunk = (ri // c) == (ci_ // c)
    tri_lo = (same_chunk & (ri >= ci_)).astype(BF16)
    tri_up = (same_chunk & (ri <= ci_)).astype(BF16)
    lane = lax.broadcasted_iota(jnp.int32, (ROW_BLOCK, W_AB), 1)

    def phase_a(bi, r0):
        h = u_ref[0, r0:r0 + ROW_BLOCK, 0:3 * gw]
        hq, hk, hv = h[:, 0:gw], h[:, gw:2 * gw], h[:, 2 * gw:3 * gw]
        ssq = _split_dot(hq * hq, ones_bd, 2)
        q_s[r0:r0 + ROW_BLOCK, :] = hq * lax.rsqrt(ssq + NORM_EPS) * (HEAD_DIM ** -0.5)
        ssk = _split_dot(hk * hk, ones_bd, 2)
        k_s[r0:r0 + ROW_BLOCK, :] = hk * lax.rsqrt(ssk + NORM_EPS)
        v_s[r0:r0 + ROW_BLOCK, :] = hv
        ab = ab_ref[0, r0:r0 + ROW_BLOCK, :]
        g = -avec_ref[...] * _softplus(ab + dtb_ref[...])
        g = jnp.where(lane < 2 * N_HEADS, g, 0.0)
        beta = _sigmoid(ab)
        cs_lo = _split_dot_left(tri_lo, g, 3)
        cs_up = _split_dot_left(tri_up, g, 3)
        cs = jnp.where(lane < N_HEADS, cs_lo, jnp.where(lane < 2 * N_HEADS, cs_up, beta))
        cs_s[r0:r0 + ROW_BLOCK, :] = cs
        cst = cs.T
        for cc in range(per_blk):
            row0 = (bi * per_blk + cc) * SUBLANES
            gt_s[row0:row0 + SUBLANES, :] = cst[0:SUBLANES, cc * c:(cc + 1) * c]

    def phase_a_gen(block_ids):
        for bi in block_ids:
            phase_a(bi, bi * ROW_BLOCK)
            yield

    ii = lax.broadcasted_iota(jnp.int32, (c, c), 0)
    jj = lax.broadcasted_iota(jnp.int32, (c, c), 1)
    eye = (ii == jj).astype(F32)
    n_sq = c.bit_length() - 2
    incl = ((ii >= jj), (ii <= jj))
    strict = ((ii > jj), (ii < jj))
    last = (c - 1, 0)
    heads = [slice(h * HEAD_DIM, (h + 1) * HEAD_DIM) for h in range(N_HEADS)]
    def expander(n_terms, lane0):
        src = lax.broadcasted_iota(jnp.int32, (n_terms * W_AB, N_HEADS * 128), 0) % W_AB
        grp = lax.broadcasted_iota(jnp.int32, (n_terms * W_AB, N_HEADS * 128), 1) // 128
        return (src == lane0 + grp).astype(BF16)

    expand_g = [expander(3, d * N_HEADS) for d in range(2)]
    expand_b = [expander(2, (2 + d) * N_HEADS) for d in range(2)]

    blk_rows = GDN_B1_CHUNKS * c
    n_steps = total // blk_rows
    assert lc % blk_rows == 0 and n_steps >= 2

    def block_row0(j, d):
        if d == 0:
            return j * blk_rows
        if isinstance(j, int):
            return lc - (j + 1) * blk_rows if (j + 1) * blk_rows <= lc else total + lc - (j + 1) * blk_rows
        return pl.multiple_of(jnp.where((j + 1) * blk_rows <= lc, lc - (j + 1) * blk_rows,
                                        total + lc - (j + 1) * blk_rows), blk_rows)

    def cat_heads(items):
        return jnp.concatenate(items, axis=1)

    def b1_gen(j, slot):
        uvr_s, wqr_s, attr_s, kttr_s = ring[slot]
        probs = []
        for d in range(2):
            r0 = block_row0(j, d)
            parts = []
            rem = cs_s[pl.ds(r0, blk_rows), :]
            for _ in range(3):
                hi = rem.astype(BF16)
                parts.append(hi)
                rem = rem - hi.astype(F32)
            xp_g = jnp.dot(jnp.concatenate(parts, axis=1), expand_g[d], preferred_element_type=F32)
            xp_b = jnp.dot(jnp.concatenate(parts[0:2], axis=1), expand_b[d], preferred_element_type=F32)
            for cc in range(GDN_B1_CHUNKS):
                r = r0 + cc * c
                kc = k_s[pl.ds(r, c), :]
                qc = q_s[pl.ds(r, c), :]
                vc = v_s[pl.ds(r, c), :]
                gt = gt_s[pl.ds(pl.multiple_of((r0 // c + cc) * SUBLANES, SUBLANES), SUBLANES), :]
                sel = []
                for h in range(N_HEADS):
                    kh, qh, vh = kc[:, heads[h]], qc[:, heads[h]], vc[:, heads[h]]
                    kk = _dot_nt(kh, kh)
                    qk = _dot_nt(qh, kh)
                    col = xp_g[cc * c:(cc + 1) * c, h * 128:h * 128 + HEAD_DIM]
                    bcol = xp_b[cc * c:(cc + 1) * c, h * 128:h * 128 + HEAD_DIM]
                    rowv = gt[d * N_HEADS + h:d * N_HEADS + h + 1, :]
                    decay = jnp.where(incl[d], jnp.exp(jnp.where(incl[d], col - rowv, 0.0)), 0.0)
                    n = -jnp.where(strict[d], kk * decay, 0.0) * bcol
                    eg = jnp.exp(col)
                    glast = col[last[d]:last[d] + 1, :]
                    pr = dict(cc=cc, d=d, a=(eye - n).astype(BF16), p=eye + n,
                              rhs=jnp.concatenate([vh * bcol, kh * (bcol * eg)], axis=1).astype(BF16),
                              att=qk * decay, qe=qh * eg, kt=kh * jnp.exp(glast - col))
                    probs.append(pr)
                    sel.append(pr)
                wqr_s[d,(2 * cc + 1) * c:(2 * cc + 2) * c, :] = (
                    cat_heads([pr['qe'] for pr in sel]).astype(BF16))
                attr_s[d,cc * c:(cc + 1) * c, :] = cat_heads([pr['att'] for pr in sel]).astype(BF16)
                ktt = cat_heads([pr['kt'] for pr in sel]).T
                kttr_s[d,cc * gw:(cc + 1) * gw, :] = ktt.astype(BF16)
        yield
        for _ in range(n_sq):
            res = [eye - jnp.dot(pr['a'], pr['p'].astype(BF16), preferred_element_type=F32)
                   for pr in probs]
            yield
            for pr, rr in zip(probs, res):
                pr['p'] = pr['p'] + _dot(pr['p'], rr)
            yield
        for pr in probs:
            pr['uw'] = _dot(pr['p'], pr['rhs'])
        yield
        for d in range(2):
            for cc in range(GDN_B1_CHUNKS):
                sel = [pr for pr in probs if pr['cc'] == cc and pr['d'] == d]
                uvr_s[d,cc * c:(cc + 1) * c, :] = cat_heads([pr['uw'][:, 0:HEAD_DIM] for pr in sel])
                wqr_s[d,2 * cc * c:(2 * cc + 1) * c, :] = (
                    cat_heads([pr['uw'][:, HEAD_DIM:2 * HEAD_DIM] for pr in sel]).astype(BF16))

    def b2_gen(j, slot):
        uvr_s, wqr_s, attr_s, kttr_s = ring[slot]
        r0s = [block_row0(j, d) for d in range(2)]
        states = [st_s[hd] for hd in range(2 * N_HEADS)]
        for step in range(GDN_B1_CHUNKS):
            probs = []
            for d in range(2):
                cc = step if d == 0 else GDN_B1_CHUNKS - 1 - step
                r = r0s[d] + cc * c
                wqm = wqr_s[d, 2 * cc * c:(2 * cc + 2) * c, :]
                uvm = uvr_s[d, cc * c:(cc + 1) * c, :]
                attm = attr_s[d, cc * c:(cc + 1) * c, :]
                grow = cs_s[pl.ds(r + last[d], 1), :]
                for h in range(N_HEADS):
                    hd = d * N_HEADS + h
                    kt = kttr_s[d, cc * gw + h * HEAD_DIM:cc * gw + (h + 1) * HEAD_DIM, :]
                    probs.append(dict(d=d, hd=hd, r=r, wqm=wqm[:, heads[h]],
                                      uv=uvm[:, heads[h]], att=attm[:, heads[h]], kt=kt,
                                      gl=jnp.exp(grow[:, hd:hd + 1])))
            for pr in probs:
                pr['wq'] = jnp.dot(pr['wqm'], states[pr['hd']].astype(BF16), preferred_element_type=F32)
            yield
            for pr in probs:
                vn = (pr['uv'] - pr['wq'][0:c, :]).astype(BF16)
                pr['o'] = pr['wq'][c:2 * c, :] + jnp.dot(pr['att'], vn, preferred_element_type=F32)
                states[pr['hd']] = (states[pr['hd']] * pr['gl']
                                    + jnp.dot(pr['kt'], vn, preferred_element_type=F32))
            for d, o_s in ((0, of_s), (1, ob_s)):
                sel = [pr for pr in probs if pr['d'] == d]
                o_s[pl.ds(sel[0]['r'], c), :] = cat_heads([pr['o'] for pr in sel])
            yield
        for hd in range(2 * N_HEADS):
            st_s[hd] = states[hd]

    def run_interleaved(gens):
        live = list(gens)
        while live:
            still = []
            for g in live:
                try:
                    next(g)
                    still.append(g)
                except StopIteration:
                    pass
            live = still

    assert blk_rows == ROW_BLOCK
    n_blocks = total // ROW_BLOCK
    n_pre = lc // ROW_BLOCK
    for bi in range(n_pre):
        phase_a(bi, bi * ROW_BLOCK)
    st_s[...] = jnp.zeros(st_s.shape, F32)
    run_interleaved([b1_gen(0, 0), phase_a_gen(range(n_pre, n_blocks))])

    def pipe_body(t, carry):
        j = 1 + 2 * t
        run_interleaved([b1_gen(j, 1), b2_gen(j - 1, 0)])
        run_interleaved([b1_gen(j + 1, 0), b2_gen(j, 1)])
        return carry

    lax.fori_loop(0, (n_steps - 1) // 2, pipe_body, 0)
    if (n_steps - 1) % 2:
        run_interleaved([b1_gen(n_steps - 1, (n_steps - 1) % 2), b2_gen(n_steps - 2, n_steps % 2)])
    def finish(r0):
        o = of_s[r0:r0 + ROW_BLOCK, :] + ob_s[r0:r0 + ROW_BLOCK, :]
        ms = _split_dot(o * o, ones_bd, 2) * (1.0 / HEAD_DIM)
        gate = u_ref[0, r0:r0 + ROW_BLOCK, 3 * gw:4 * gw]
        y = o * lax.rsqrt(ms + NORM_EPS) * ng_ref[...] * gate
        o_ref[0, r0:r0 + ROW_BLOCK, :] = y.astype(BF16)

    def finish_gen(rows):
        for r0 in rows:
            finish(r0)
            yield

    last_rows = [block_row0(n_steps - 1, d) for d in range(2)]
    early = [bi * ROW_BLOCK for bi in range(n_blocks) if bi * ROW_BLOCK not in last_rows]
    run_interleaved([b2_gen(n_steps - 1, (n_steps - 1) % 2), finish_gen(early)])
    for r0 in sorted(set(last_rows)):
        finish(r0)


def _gdn_params(a_log, dt_bias, norm_g):
    depth = a_log.shape[0]
    pad = ((0, 0), (0, 0), (0, W_AB - 2 * N_HEADS))
    avec = jnp.pad(jnp.exp(a_log.reshape(depth, 1, 2 * N_HEADS)), pad)
    dtb = jnp.pad(dt_bias.reshape(depth, 1, 2 * N_HEADS), pad)
    ng = jnp.tile(norm_g.reshape(depth, 1, HEAD_DIM), (1, 1, N_HEADS))
    return avec, dtb, ng


def _gdn(u, ab, avec, dtb, ng, l, *, lc):
    nb, total, _ = u.shape
    nch = total // GDN_CHUNK
    blk = GDN_B1_CHUNKS * GDN_CHUNK
    return pl.pallas_call(
        functools.partial(_gdn_body, lc=lc),
        grid=(nb,),
        in_specs=[pl.BlockSpec((1, total, W_GDN), lambda b: (b, 0, 0)),
                  pl.BlockSpec((1, total, W_AB), lambda b: (b, 0, 0)),
                  _layer(avec, l), _layer(dtb, l), _layer(ng, l)],
        out_specs=pl.BlockSpec((1, total, GROUP_W), lambda b: (b, 0, 0)),
        out_shape=jax.ShapeDtypeStruct((nb, total, GROUP_W), BF16),
        scratch_shapes=[pltpu.VMEM((total, GROUP_W), F32),
                        pltpu.VMEM((total, GROUP_W), F32),
                        pltpu.VMEM((total, GROUP_W), F32),
                        pltpu.VMEM((total, W_AB), F32),
                        pltpu.VMEM((nch * SUBLANES, GDN_CHUNK), F32),
                        pltpu.VMEM((total, GROUP_W), F32),
                        pltpu.VMEM((total, GROUP_W), F32),
                        pltpu.VMEM((2 * N_HEADS, HEAD_DIM, HEAD_DIM), F32)] + 2 * [
                        pltpu.VMEM((2, blk, GROUP_W), F32),
                        pltpu.VMEM((2, 2 * blk, GROUP_W), BF16),
                        pltpu.VMEM((2, blk, GROUP_W), BF16),
                        pltpu.VMEM((2, GDN_B1_CHUNKS * GROUP_W, GDN_CHUNK), BF16)],
        compiler_params=_cparams(("arbitrary",)),
        name="gdn",
    )(u, ab, avec, dtb, ng)


def _pack_w_in(w_in):
    gw = GROUP_W
    o_gdn = 2 * gw
    o_ab = o_gdn + 4 * gw
    o_ret = o_ab + 4 * N_HEADS
    o_swa = o_ret + 4 * gw
    end = o_swa + W_SWA
    assert end == w_in.shape[-1]
    wb = w_in.astype(BF16)
    pad = jnp.zeros(wb.shape[:-1] + (W_AB - 4 * N_HEADS,), BF16)
    return jnp.concatenate([wb[..., 0:o_ab], wb[..., o_ret:end], wb[..., o_ab:o_ret], pad], axis=-1)


def _rope_tables(ang, lc):
    cos = jnp.cos(ang)
    sin = jnp.sin(ang)
    cos_h = jnp.concatenate([cos, cos], axis=-1)
    sin_h = jnp.concatenate([-sin, sin], axis=-1)
    cos_t = jnp.tile(cos_h, (1, N_HEADS))
    sin_t = jnp.tile(sin_h, (1, N_HEADS))
    cos_t = jnp.concatenate([jnp.ones((lc, GROUP_W), F32), cos_t], axis=0)
    sin_t = jnp.concatenate([jnp.zeros((lc, GROUP_W), F32), sin_t], axis=0)
    return cos_t, sin_t


def _rope_freqs(pos, n):
    inv = ROPE_BASE ** (-jnp.arange(0, n, 2, dtype=F32) / n)
    return pos[:, None] * inv[None, :]


def kernel(x, c, ctx, c_ctx, w_mod, b_mod, pre_norm_g, post_norm_g, w_in, w_out, lru_conv_w, lru_conv_b, lru_w_r, lru_b_r, lru_w_i, lru_b_i, lru_lambda, gdn_conv_w, gdn_a_log, gdn_dt_bias, gdn_norm_g, ret_decay_logit, swa_sink):
    nb, t, d = x.shape
    lc = ctx.shape[1]
    depth = w_mod.shape[0]
    assert t % ROW_BLOCK == 0 and lc % ROW_BLOCK == 0 and d == 4 * GROUP_W

    rows = t // GRID_W
    row = jnp.repeat(jnp.arange(rows, dtype=F32), GRID_W)
    col = jnp.tile(jnp.arange(GRID_W, dtype=F32), rows)
    ang2d = jnp.concatenate([_rope_freqs(row, HEAD_DIM // 2), _rope_freqs(col, HEAD_DIM // 2)], axis=-1)
    ang1d = _rope_freqs(jnp.arange(t, dtype=F32), HEAD_DIM)
    cos1, sin1 = _rope_tables(ang1d, lc)
    cos2, sin2 = _rope_tables(ang2d, lc)

    mod_rows = -(-(nb + 1) // SUBLANES) * SUBLANES
    s_rows = jnp.concatenate([c, c_ctx[None, :], jnp.zeros((mod_rows - nb - 1, d), F32)], axis=0)
    mod = _modulation(s_rows, w_mod, b_mod)

    w_in_p = _pack_w_in(w_in)
    w_out_b = w_out.astype(BF16)
    pre_g = pre_norm_g.reshape(depth, 1, d)
    post_g = post_norm_g.reshape(depth, 1, d)
    lru_cb, lru_wg, lru_bg, lru_lam = _lru_params(lru_conv_b, lru_w_r, lru_b_r, lru_w_i, lru_b_i, lru_lambda)
    gdn_avec, gdn_dtb, gdn_ng = _gdn_params(gdn_a_log, gdn_dt_bias, gdn_norm_g)
    ret_lg = jnp.repeat(ret_decay_logit, HEAD_DIM, axis=-1)
    sink = swa_sink.reshape(depth * N_HEADS)
    xs = jnp.concatenate([ctx, x], axis=1)
    for l in range(depth):
        u_lru, u_gdn, u_ret, u_swa, u_ab = _in_proj(xs, mod, pre_g, w_in_p, (cos1, sin1, cos2, sin2),
                                                    (lru_conv_w, lru_cb, gdn_conv_w), l, lc=lc)
        ya = _lru(u_lru, lru_wg, lru_bg, lru_lam, l, lc=lc)
        yb = _gdn(u_gdn, u_ab, gdn_avec, gdn_dtb, gdn_ng, l, lc=lc)
        yc = _ret(u_ret, ret_lg, l, lc=lc)
        yd = _swa(u_swa, sink, l, lc=lc)
        xs = _out_proj((ya, yb, yc, yd), xs, mod, post_g, w_out_b, l, lc=lc, latent_only=l == depth - 1)
    return xs
```

```python
import functools
import math

import jax
import jax.numpy as jnp
from jax import lax
from jax.experimental import pallas as pl
from jax.experimental.pallas import tpu as pltpu

F32 = jnp.float32
BF16 = jnp.bfloat16

HEAD_DIM = 64
GROUP_W = 256
N_HEADS = GROUP_W // HEAD_DIM
NORM_EPS = 1e-6
ROPE_BASE = 10000.0
NEG_INF = -1e30
GRID_W = 64
CONV_W = 4
LRU_C = 8.0
LRU_SCAN_UNROLL = 4
GDN_CHUNK = 64
PAIR_W = 2 * HEAD_DIM
GDN_B1_CHUNKS = 4
RET_CHUNK = 256
RET_CHUNKS_PER_STEP = 5
SWA_KV_HEADS = 2
SWA_KV_W = SWA_KV_HEADS * HEAD_DIM
WINDOW = 128
SWA_BLOCKS_PER_STEP = 2
ROW_BLOCK = 256
CONV_ROWS = 64
PROJ_ROWS_MAX = 576
SUBLANES = 8
VMEM_LIMIT_BYTES = 56 * 1024 * 1024

W_LRU = 2 * GROUP_W
W_GDN = 4 * GROUP_W
W_RET = 4 * GROUP_W
W_SWA = 2 * GROUP_W + 2 * SWA_KV_W
W_AB = 128
IN_W_PACKED = W_LRU + W_GDN + W_RET + W_SWA + W_AB


def _cparams(sem, flags=None):
    return pltpu.CompilerParams(dimension_semantics=sem, vmem_limit_bytes=VMEM_LIMIT_BYTES, flags=flags)


def _dot(a, b):
    return jnp.dot(a.astype(BF16), b.astype(BF16), preferred_element_type=F32)


def _dot_nt(a, b):
    return lax.dot_general(a.astype(BF16), b.astype(BF16), (((1,), (1,)), ((), ())),
                           preferred_element_type=F32)


def _dot_tn(a, b):
    return lax.dot_general(a.astype(BF16), b.astype(BF16), (((0,), (0,)), ((), ())),
                           preferred_element_type=F32)


def _split_dot(x, w, parts):
    acc = None
    r = x
    for _ in range(parts):
        hi = r.astype(BF16)
        t = jnp.dot(hi, w, preferred_element_type=F32)
        acc = t if acc is None else acc + t
        r = r - hi.astype(F32)
    return acc


def _split_dot_left(w, x, parts):
    acc = None
    r = x
    for _ in range(parts):
        hi = r.astype(BF16)
        t = jnp.dot(w, hi, preferred_element_type=F32)
        acc = t if acc is None else acc + t
        r = r - hi.astype(F32)
    return acc


def _sigmoid(x):
    return 0.5 * jnp.tanh(0.5 * x) + 0.5


def _silu(x):
    return x * _sigmoid(x)


def _softplus(x):
    return jnp.maximum(x, 0.0) + jnp.log1p(jnp.exp(-jnp.abs(x)))


def _head_ones():
    r = lax.broadcasted_iota(jnp.int32, (GROUP_W, GROUP_W), 0) // HEAD_DIM
    c = lax.broadcasted_iota(jnp.int32, (GROUP_W, GROUP_W), 1) // HEAD_DIM
    return (r == c).astype(BF16)


def _row_blocks(lc, total):
    out = []
    for r0 in range(0, total, ROW_BLOCK):
        out.append((r0, 0, lc) if r0 < lc else (r0, lc, total))
    return out


def _mod_body(s_ref, w_ref, b_ref, o_ref):
    s = _silu(s_ref[...])
    o_ref[0] = _dot(s, w_ref[0]) + b_ref[0]


def _modulation(s_rows, w_mod, b_mod):
    depth, d, d3 = w_mod.shape
    rows = s_rows.shape[0]
    nt = d3 // d
    return pl.pallas_call(
        _mod_body,
        grid=(depth, nt),
        in_specs=[pl.BlockSpec((rows, d), lambda l, j: (0, 0)),
                  pl.BlockSpec((1, d, d), lambda l, j: (l, 0, j)),
                  pl.BlockSpec((1, 1, d), lambda l, j: (l, 0, j))],
        out_specs=pl.BlockSpec((1, rows, d), lambda l, j: (l, 0, j)),
        out_shape=jax.ShapeDtypeStruct((depth, rows, d3), F32),
        compiler_params=_cparams(("arbitrary", "arbitrary")),
        name="modulation",
    )(s_rows, w_mod, b_mod.reshape(depth, 1, d3))


def _proj_rows(total):
    for tm in range(PROJ_ROWS_MAX, SUBLANES - 1, -SUBLANES):
        if total % tm == 0:
            return tm
    raise ValueError(total)


def _mod_rows(mod_ref, b, i, tm, lc, nb, c0, c1):
    lat = mod_ref[pl.ds(b, 1), c0:c1]
    if lc % tm == 0:
        ctx = mod_ref[nb:nb + 1, c0:c1]
        return jnp.where(i * tm < lc, ctx, lat)
    ctx = mod_ref[nb:nb + 1, c0:c1]
    row = i * tm + lax.broadcasted_iota(jnp.int32, (tm, 1), 0)
    return jnp.where(row < lc, ctx, lat)


def _inproj_body(x_ref, xp_ref, xn_ref, mod_ref, g_ref, w_ref, cos1_ref, sin1_ref, cos2_ref, sin2_ref,
                 lcw_ref, lcb_ref, gcw_ref, o_lru, o_gdn, o_ret, o_swa, o_ab, *conv_scr, lc, nb, total):
    i = pl.program_id(0)
    b = pl.program_id(1)
    d = x_ref.shape[2]
    tm = x_ref.shape[1]
    gw = GROUP_W
    scale = HEAD_DIM ** -0.5

    def prenorm(xv, m):
        ms = jnp.mean(xv * xv, axis=-1, keepdims=True)
        y = xv * lax.rsqrt(ms + NORM_EPS) * g_ref[...]
        return (y * (1.0 + m[:, d:2 * d]) + m[:, 0:d]).astype(BF16)

    h = prenorm(x_ref[0], _mod_rows(mod_ref, b, i, tm, lc, nb, 0, 2 * d))
    halo = prenorm(jnp.concatenate([xp_ref[0], xn_ref[0]], axis=0), mod_ref[pl.ds(b, 1), 0:2 * d])
    h_scr = conv_scr[4]
    h_scr[0:tm, :] = h
    h_scr[tm:tm + 2 * SUBLANES, :] = halo
    h_all = slice(0, tm + 2 * SUBLANES)

    def proj(c0, width, rows=slice(0, tm)):
        return jnp.dot(h_scr[rows, :], w_ref[:, c0:c0 + width], preferred_element_type=F32)

    nt = total // tm
    b_tile, b_row = lc // tm, lc % tm
    assert b_row % SUBLANES == 0 and (b_row == 0 or SUBLANES <= b_row <= tm - SUBLANES)
    prev_ok = i > 0
    next_ok = i < nt - 1
    if b_row == 0:
        prev_ok = prev_ok & (i != b_tile)
        next_ok = next_ok & (i != b_tile - 1)
    prev_f = jnp.where(prev_ok, 1.0, 0.0)
    next_f = jnp.where(next_ok, 1.0, 0.0)

    def dwconv_to(o_ref, oc0, u_all, scr, cw_ref, c0, post):
        scr[0:SUBLANES, :] = u_all[tm:tm + SUBLANES] * prev_f
        scr[SUBLANES:SUBLANES + tm, :] = u_all[0:tm]
        scr[SUBLANES + tm:2 * SUBLANES + tm, :] = u_all[tm + SUBLANES:tm + 2 * SUBLANES] * next_f
        for r0 in range(0, tm, CONV_ROWS):
            n = min(CONV_ROWS, tm - r0)
            near_boundary = b_row and r0 - 1 <= b_row <= r0 + n + 1
            acc = None
            for k in range(CONV_W):
                off = k - 2
                tap = scr[SUBLANES + r0 + off:SUBLANES + r0 + off + n, :]
                if near_boundary and off != 0:
                    r = r0 + off + lax.broadcasted_iota(jnp.int32, (n, 1), 0)
                    keep = ((r >= b_row) == (r - off >= b_row)) | (i != b_tile)
                    tap = jnp.where(keep, tap, 0.0)
                term = tap * cw_ref[k:k + 1, c0:c0 + gw]
                acc = term if acc is None else acc + term
            o_ref[0, r0:r0 + n, oc0:oc0 + gw] = post(acc)

    c_gdn, c_ret, c_swa, c_ab = W_LRU, W_LRU + W_GDN, W_LRU + W_GDN + W_RET, W_LRU + W_GDN + W_RET + W_SWA
    kvw = SWA_KV_W

    def ep_lru_x(u):
        dwconv_to(o_lru, 0, u, conv_scr[0], lcw_ref, 0, lambda a: a + lcb_ref[...])

    def ep_gdn_conv(p):
        return lambda u: dwconv_to(o_gdn, p * gw, u, conv_scr[1 + p], gcw_ref, p * gw, _silu)

    def ep_gates(u):
        o_lru[0, :, gw:2 * gw] = _silu(u[:, 0:gw])
        o_gdn[0, :, 3 * gw:4 * gw] = _silu(u[:, gw:2 * gw])

    def ep_ret(u):
        cos, sin = cos1_ref[...], sin1_ref[...]
        o_ret[0, :, 0:gw] = _rope(u[:, 0:gw], cos, sin)
        o_ret[0, :, gw:2 * gw] = _rope(u[:, gw:2 * gw], cos, sin) * scale
        o_ret[0, :, 2 * gw:3 * gw] = u[:, 2 * gw:3 * gw]
        o_ret[0, :, 3 * gw:4 * gw] = _silu(u[:, 3 * gw:4 * gw])

    def ep_swa(u):
        cos, sin = cos2_ref[...], sin2_ref[...]
        o_swa[0, :, 0:gw] = _rope(u[:, 0:gw], cos, sin) * scale
        o_swa[0, :, gw:gw + kvw] = _rope(u[:, gw:gw + kvw], cos[:, 0:kvw], sin[:, 0:kvw])
        o_swa[0, :, gw + kvw:gw + 2 * kvw] = u[:, gw + kvw:gw + 2 * kvw]
        o_swa[0, :, gw + 2 * kvw:2 * gw + 2 * kvw] = _silu(u[:, gw + 2 * kvw:2 * gw + 2 * kvw])

    def ep_ab(u):
        o_ab[0] = u

    def gates_proj():
        return jnp.concatenate([proj(gw, gw), proj(c_gdn + 3 * gw, gw)], axis=1)

    work = [(lambda: proj(0, gw, h_all), ep_lru_x),
            (lambda: proj(c_ret, W_RET), ep_ret),
            (lambda: proj(c_gdn, gw, h_all), ep_gdn_conv(0)),
            (lambda: proj(c_swa, W_SWA), ep_swa),
            (lambda: proj(c_gdn + gw, gw, h_all), ep_gdn_conv(1)),
            (gates_proj, ep_gates),
            (lambda: proj(c_gdn + 2 * gw, gw, h_all), ep_gdn_conv(2)),
            (lambda: proj(c_ab, W_AB), ep_ab)]
    pending = work[0][0]()
    for k, (_, epilogue) in enumerate(work):
        cur = pending
        if k + 1 < len(work):
            pending = work[k + 1][0]()
        epilogue(cur)


def _in_proj(x, mod, g, w, tables, conv_params, l, *, lc):
    nb, total, d = x.shape
    tm = _proj_rows(total)
    nt = total // tm
    per = tm // SUBLANES
    last = total // SUBLANES - 1
    widths = (W_LRU, W_GDN, W_RET, W_SWA, W_AB)
    tspec = pl.BlockSpec((tm, GROUP_W), lambda i, b: (i, 0))
    return pl.pallas_call(
        functools.partial(_inproj_body, lc=lc, nb=nb, total=total),
        grid=(nt, nb),
        in_specs=[pl.BlockSpec((1, tm, d), lambda i, b: (b, i, 0)),
                  pl.BlockSpec((1, SUBLANES, d), lambda i, b: (b, jnp.maximum(i * per - 1, 0), 0)),
                  pl.BlockSpec((1, SUBLANES, d), lambda i, b: (b, jnp.minimum((i + 1) * per, last), 0)),
                  _layer(mod, l), _layer(g, l), _layer(w, l), tspec, tspec, tspec, tspec]
                 + [_layer(p, l) for p in conv_params],
        out_specs=[pl.BlockSpec((1, tm, wd), lambda i, b: (b, i, 0)) for wd in widths],
        out_shape=[jax.ShapeDtypeStruct((nb, total, wd), F32) for wd in widths],
        scratch_shapes=[pltpu.VMEM((tm + 2 * SUBLANES, GROUP_W), F32) for _ in range(4)]
                       + [pltpu.VMEM((tm + 2 * SUBLANES, d), BF16)],
        compiler_params=_cparams(("arbitrary", "arbitrary")),
        name="in_proj",
    )(x, x, x, mod, g, w, *tables, *conv_params)


def _outproj_body(ya, yb, yc, yd, x_ref, mod_ref, g_ref, w_ref, o_ref, *, lc, nb, tile0):
    b = pl.program_id(0)
    i = pl.program_id(1) + tile0
    d = x_ref.shape[2]
    acc = None
    for k, y_ref in enumerate((ya, yb, yc, yd)):
        t = jnp.dot(y_ref[0], w_ref[k * GROUP_W:(k + 1) * GROUP_W, :], preferred_element_type=F32)
        acc = t if acc is None else acc + t
    ms = jnp.mean(acc * acc, axis=-1, keepdims=True)
    yn = acc * lax.rsqrt(ms + NORM_EPS) * g_ref[...]
    gate = _mod_rows(mod_ref, b, i, acc.shape[0], lc, nb, 2 * d, 3 * d)
    o_ref[0] = x_ref[0] + gate * yn


def _out_proj(ys, x, mod, g, w, l, *, lc, latent_only=False):
    nb, total, d = x.shape
    if latent_only:
        tm = _proj_rows(math.gcd(lc, total - lc))
        tile0, rows_out = lc // tm, total - lc
    else:
        tm = _proj_rows(total)
        tile0, rows_out = 0, total
    yspec = pl.BlockSpec((1, tm, GROUP_W), lambda b, i: (b, i + tile0, 0))
    return pl.pallas_call(
        functools.partial(_outproj_body, lc=lc, nb=nb, tile0=tile0),
        grid=(nb, rows_out // tm),
        in_specs=[yspec, yspec, yspec, yspec,
                  pl.BlockSpec((1, tm, d), lambda b, i: (b, i + tile0, 0)),
                  _layer(mod, l), _layer(g, l), _layer(w, l)],
        out_specs=pl.BlockSpec((1, tm, d), lambda b, i: (b, i, 0)),
        out_shape=jax.ShapeDtypeStruct((nb, rows_out, d), F32),
        compiler_params=_cparams(("arbitrary", "arbitrary")),
        name="out_proj",
    )(*ys, x, mod, g, w)


def _lru_scan(a_s, b_s, h_s, tile_lo, n_tiles, carry, *, rev, accumulate):
    row = lax.broadcasted_iota(jnp.int32, (SUBLANES, GROUP_W), 0)

    def body(j, carry):
        t = tile_lo + (n_tiles - 1 - j if rev else j)
        r = pl.multiple_of(t * SUBLANES, SUBLANES)
        a = a_s[pl.ds(r, SUBLANES), :]
        b = b_s[pl.ds(r, SUBLANES), :]
        for s in (1, 2, 4):
            if rev:
                ra = pltpu.roll(a, SUBLANES - s, 0)
                rb = pltpu.roll(b, SUBLANES - s, 0)
                m = row < SUBLANES - s
            else:
                ra = pltpu.roll(a, s, 0)
                rb = pltpu.roll(b, s, 0)
                m = row >= s
            b = a * jnp.where(m, rb, 0.0) + b
            a = a * jnp.where(m, ra, 1.0)
        h = a * carry + b
        if accumulate:
            h_s[pl.ds(r, SUBLANES), :] = h_s[pl.ds(r, SUBLANES), :] + h
        else:
            h_s[pl.ds(r, SUBLANES), :] = h
        return h[0:1, :] if rev else h[SUBLANES - 1:SUBLANES, :]

    return lax.fori_loop(0, n_tiles, body, carry, unroll=LRU_SCAN_UNROLL)


def _lru_body(u_ref, wg_ref, bg_ref, lam_ref, o_ref, a_s, b_s, h_s, *, lc):
    total = u_ref.shape[1]
    blocks = _row_blocks(lc, total)
    zero = jnp.zeros((1, GROUP_W), F32)
    for d in range(2):
        sp = _softplus(-lam_ref[d])
        for r0, _, _ in blocks:
            uc = u_ref[0, r0:r0 + ROW_BLOCK, 0:GROUP_W]
            gts = _dot(uc, wg_ref[d]) + bg_ref[d]
            r = _sigmoid(gts[:, 0:GROUP_W])
            ig = _sigmoid(gts[:, GROUP_W:2 * GROUP_W])
            a = jnp.exp(-LRU_C * r * sp)
            a_s[r0:r0 + ROW_BLOCK, :] = a
            b_s[r0:r0 + ROW_BLOCK, :] = jnp.sqrt(1.0 - a * a) * (ig * uc)
        ct, tt = lc // SUBLANES, total // SUBLANES
        if d == 0:
            _lru_scan(a_s, b_s, h_s, 0, tt, zero, rev=False, accumulate=False)
        else:
            carry = _lru_scan(a_s, b_s, h_s, 0, ct, zero, rev=True, accumulate=True)
            _lru_scan(a_s, b_s, h_s, ct, tt - ct, carry, rev=True, accumulate=True)
    for r0, _, _ in blocks:
        gate = u_ref[0, r0:r0 + ROW_BLOCK, GROUP_W:2 * GROUP_W]
        o_ref[0, r0:r0 + ROW_BLOCK, :] = (h_s[r0:r0 + ROW_BLOCK, :] * gate).astype(BF16)


def _block_diag(w):
    n, c = w.shape[-3], w.shape[-1]
    eye = jnp.eye(n, dtype=w.dtype)
    return (eye[:, None, :, None] * w[..., :, :, None, :]).reshape(w.shape[:-3] + (n * c, n * c))


def _layer(arr, l):
    shape = arr.shape[1:]
    return pl.BlockSpec((None,) + shape, lambda *_: (l,) + (0,) * len(shape))


def _lru_params(conv_b, w_r, b_r, w_i, b_i, lam):
    depth = conv_b.shape[0]
    wg = jnp.concatenate([_block_diag(w_r), _block_diag(w_i)], axis=-1).astype(BF16)
    bg = jnp.concatenate([b_r, b_i], axis=-1).reshape(depth, 2, 1, 2 * GROUP_W)
    return conv_b.reshape(depth, 1, GROUP_W), wg, bg, lam.reshape(depth, 2, 1, GROUP_W)


def _lru(u, wg, bg, lam, l, *, lc):
    nb, total, _ = u.shape
    return pl.pallas_call(
        functools.partial(_lru_body, lc=lc),
        grid=(nb,),
        in_specs=[pl.BlockSpec((1, total, W_LRU), lambda b: (b, 0, 0)),
                  _layer(wg, l), _layer(bg, l), _layer(lam, l)],
        out_specs=pl.BlockSpec((1, total, GROUP_W), lambda b: (b, 0, 0)),
        out_shape=jax.ShapeDtypeStruct((nb, total, GROUP_W), BF16),
        scratch_shapes=[pltpu.VMEM((total, GROUP_W), F32) for _ in range(3)],
        compiler_params=_cparams(("arbitrary",)),
        name="lru",
    )(u, wg, bg, lam)


def _rope(x, cos, sin_signed):
    half = HEAD_DIM // 2
    outs = []
    for c0 in range(0, x.shape[1], 128):
        xs = x[:, c0:c0 + 128]
        lane = lax.broadcasted_iota(jnp.int32, xs.shape, 1)
        swapped = jnp.where((lane % HEAD_DIM) < half,
                            pltpu.roll(xs, 128 - half, 1), pltpu.roll(xs, half, 1))
        outs.append(xs * cos[:, c0:c0 + 128] + swapped * sin_signed[:, c0:c0 + 128])
    return outs[0] if len(outs) == 1 else jnp.concatenate(outs, axis=1)


def _ret_body(u_ref, lg_ref, o_ref, ds_s, st_s, m_s, *, lc):
    total = u_ref.shape[1]
    c = RET_CHUNK
    nch = total // c
    n_ctx = lc // c
    gw = GROUP_W
    ones_bd = _head_ones()
    bd_mask = ones_bd > 0
    lg = -_softplus(-lg_ref[...])
    lgf, lgb = lg[0:1, :], lg[1:2, :]
    pos = lax.broadcasted_iota(jnp.int32, (c, 1), 0).astype(F32)
    qdec_f = jnp.exp((pos + 1.0) * lgf)
    qdec_b = jnp.exp((c - pos) * lgb)
    kdec_f = jnp.exp((c - 1.0 - pos) * lgf)
    kdec_b = jnp.exp(pos * lgb)
    cdec_f = jnp.exp(float(c) * lgf)
    cdec_b = jnp.exp(float(c) * lgb)
    dij = (lax.broadcasted_iota(jnp.int32, (c, c), 0)
           - lax.broadcasted_iota(jnp.int32, (c, c), 1)).astype(F32)
    for h in range(N_HEADS):
        lf = lgf[:, h * HEAD_DIM:h * HEAD_DIM + 1]
        lb = lgb[:, h * HEAD_DIM:h * HEAD_DIM + 1]
        fwd = jnp.exp(jnp.maximum(dij, 0.0) * lf)
        bwd = jnp.exp(jnp.maximum(-dij, 0.0) * lb)
        m_s[h] = jnp.where(dij > 0, fwd, jnp.where(dij < 0, bwd, 2.0))

    for ci in range(nch):
        r0 = ci * c
        kr = u_ref[0, r0:r0 + c, gw:2 * gw]
        v = u_ref[0, r0:r0 + c, 2 * gw:3 * gw]
        ds_s[0, ci] = jnp.where(bd_mask, _dot_tn(kr * kdec_f, v), 0.0)
        ds_s[1, ci] = jnp.where(bd_mask, _dot_tn(kr * kdec_b, v), 0.0)

    s = jnp.zeros((gw, gw), F32)
    for ci in range(nch):
        st_s[0, ci] = s
        s = s * cdec_f + ds_s[0, ci]
    s = jnp.zeros((gw, gw), F32)
    for ci in list(range(n_ctx - 1, -1, -1)) + list(range(nch - 1, n_ctx - 1, -1)):
        st_s[1, ci] = s
        s = s * cdec_b + ds_s[1, ci]

    hss = [slice(h * HEAD_DIM, (h + 1) * HEAD_DIM) for h in range(N_HEADS)]
    for ci0 in range(0, nch, RET_CHUNKS_PER_STEP):
        cis = list(range(ci0, min(ci0 + RET_CHUNKS_PER_STEP, nch)))
        qrs = [u_ref[0, ci * c:(ci + 1) * c, 0:gw] for ci in cis]
        krs = [u_ref[0, ci * c:(ci + 1) * c, gw:2 * gw] for ci in cis]
        vs = [u_ref[0, ci * c:(ci + 1) * c, 2 * gw:3 * gw].astype(BF16) for ci in cis]
        atts = [[_dot_nt(qr[:, hs], kr[:, hs]) for hs in hss] for qr, kr in zip(qrs, krs)]
        inters = [_dot(qr * qdec_f, st_s[0, ci]) + _dot(qr * qdec_b, st_s[1, ci]) for ci, qr in zip(cis, qrs)]
        atts = [[(a * m_s[h]).astype(BF16) for h, a in enumerate(att)] for att in atts]
        os_ = [inter + jnp.concatenate(
            [jnp.dot(a, v[:, hs], preferred_element_type=F32) for a, hs in zip(att, hss)], axis=1)
            for inter, att, v in zip(inters, atts, vs)]
        mus = [_split_dot(o, ones_bd, 2) * (1.0 / HEAD_DIM) for o in os_]
        dlts = [o - mu for o, mu in zip(os_, mus)]
        vrs = [_split_dot(dlt * dlt, ones_bd, 2) * (1.0 / HEAD_DIM) for dlt in dlts]
        for ci, dlt, var in zip(cis, dlts, vrs):
            gate = u_ref[0, ci * c:(ci + 1) * c, 3 * gw:4 * gw]
            o_ref[0, ci * c:(ci + 1) * c, :] = (dlt * lax.rsqrt(var + NORM_EPS) * gate).astype(BF16)


def _ret(u, lg, l, *, lc):
    nb, total, _ = u.shape
    nch = total // RET_CHUNK
    return pl.pallas_call(
        functools.partial(_ret_body, lc=lc),
        grid=(nb,),
        in_specs=[pl.BlockSpec((1, total, W_RET), lambda b: (b, 0, 0)), _layer(lg, l)],
        out_specs=pl.BlockSpec((1, total, GROUP_W), lambda b: (b, 0, 0)),
        out_shape=jax.ShapeDtypeStruct((nb, total, GROUP_W), BF16),
        scratch_shapes=[pltpu.VMEM((2, nch, GROUP_W, GROUP_W), F32),
                        pltpu.VMEM((2, nch, GROUP_W, GROUP_W), F32),
                        pltpu.VMEM((N_HEADS, RET_CHUNK, RET_CHUNK), F32)],
        compiler_params=_cparams(("arbitrary",)),
        name="retention",
    )(u, lg)


def _swa_body(sink_ref, u_ref, o_ref, k_s, v_s, *, lc, sink0):
    total = u_ref.shape[1]
    t_lat = total - lc
    gw = GROUP_W
    blk = WINDOW
    nblk = t_lat // blk
    kv0 = lc + blk
    zpad = jnp.zeros((blk, SWA_KV_W), BF16)
    for s_ref in (k_s, v_s):
        s_ref[lc:lc + blk, :] = zpad
        s_ref[kv0 + t_lat:kv0 + t_lat + blk, :] = zpad
    for r0, _, _ in _row_blocks(lc, total):
        dst = r0 if r0 < lc else r0 + blk
        k_s[dst:dst + ROW_BLOCK, :] = u_ref[0, r0:r0 + ROW_BLOCK, gw:gw + SWA_KV_W].astype(BF16)
        v_s[dst:dst + ROW_BLOCK, :] = (
            u_ref[0, r0:r0 + ROW_BLOCK, gw + SWA_KV_W:gw + 2 * SWA_KV_W].astype(BF16))

    grp = N_HEADS // SWA_KV_HEADS

    def attend(items):
        scores = [[_dot_nt(q2, kk) for kk in keys] for q2, keys, _, _, _ in items]
        exps, dens = [], []
        for (_, _, _, masks, sink_col), sc in zip(items, scores):
            sc = [s if mk is None else jnp.where(mk, s, NEG_INF) for s, mk in zip(sc, masks)]
            mx = sink_col
            for s in sc:
                mx = jnp.maximum(mx, jnp.max(s, axis=-1, keepdims=True))
            es = [jnp.exp(s - mx) for s in sc]
            den = jnp.exp(sink_col - mx)
            for e in es:
                den = den + jnp.sum(e, axis=-1, keepdims=True)
            exps.append([e.astype(BF16) for e in es])
            dens.append(den)
        outs = []
        for (_, _, vals, _, _), es, den in zip(items, exps, dens):
            acc = None
            for e, vv in zip(es, vals):
                t = jnp.dot(e, vv.astype(BF16), preferred_element_type=F32)
                acc = t if acc is None else acc + t
            outs.append(acc / den)
        return outs

    def sink_column(hk, rows_per_head):
        row = lax.broadcasted_iota(jnp.int32, (grp * rows_per_head, 1), 0)
        col = jnp.full((grp * rows_per_head, 1), sink_ref[sink0 + hk * grp], F32)
        for g in range(1, grp):
            col = jnp.where(row >= g * rows_per_head, sink_ref[sink0 + hk * grp + g], col)
        return col

    kvs = [slice(hk * HEAD_DIM, (hk + 1) * HEAD_DIM) for hk in range(SWA_KV_HEADS)]

    def stack_q(hk, rows):
        return jnp.concatenate(
            [u_ref[0, rows, (hk * grp + g) * HEAD_DIM:(hk * grp + g + 1) * HEAD_DIM] for g in range(grp)],
            axis=0)

    def unstack_o(o2s, n):
        return jnp.concatenate([o2s[hk][g * n:(g + 1) * n, :]
                                for hk in range(SWA_KV_HEADS) for g in range(grp)], axis=1)

    o2s = attend([(stack_q(hk, slice(0, lc)), [k_s[0:lc, kvs[hk]]], [v_s[0:lc, kvs[hk]]], [None],
                   sink_column(hk, lc)) for hk in range(SWA_KV_HEADS)])
    gate = u_ref[0, 0:lc, gw + 2 * SWA_KV_W:2 * gw + 2 * SWA_KV_W]
    o_ref[0, 0:lc, :] = (unstack_o(o2s, lc) * gate).astype(BF16)

    qi = lax.broadcasted_iota(jnp.int32, (grp * blk, 3 * blk), 0) % blk
    kj = lax.broadcasted_iota(jnp.int32, (grp * blk, 3 * blk), 1)
    in_win = jnp.abs(kj - blk - qi) <= WINDOW

    assert nblk % SWA_BLOCKS_PER_STEP == 0

    def block_body(step, carry):
        items, qrows = [], []
        for s in range(SWA_BLOCKS_PER_STEP):
            n = step * SWA_BLOCKS_PER_STEP + s
            kpos = n * blk - blk + kj
            mask = in_win & (kpos >= 0) & (kpos < t_lat)
            qrow = pl.multiple_of(lc + n * blk, blk)
            wrow = qrow
            qrows.append(qrow)
            items += [(stack_q(hk, pl.ds(qrow, blk)),
                       [k_s[pl.ds(wrow, 3 * blk), kvs[hk]], k_s[0:lc, kvs[hk]]],
                       [v_s[pl.ds(wrow, 3 * blk), kvs[hk]], v_s[0:lc, kvs[hk]]],
                       [mask, None], sink_column(hk, blk)) for hk in range(SWA_KV_HEADS)]
        o2s = attend(items)
        for s, qrow in enumerate(qrows):
            gate = u_ref[0, pl.ds(qrow, blk), gw + 2 * SWA_KV_W:2 * gw + 2 * SWA_KV_W]
            o = unstack_o(o2s[s * SWA_KV_HEADS:(s + 1) * SWA_KV_HEADS], blk)
            o_ref[0, pl.ds(qrow, blk), :] = (o * gate).astype(BF16)
        return carry

    lax.fori_loop(0, nblk // SWA_BLOCKS_PER_STEP, block_body, 0)


def _swa(u, sink, l, *, lc):
    nb, total, _ = u.shape
    grid_spec = pltpu.PrefetchScalarGridSpec(
        num_scalar_prefetch=1,
        grid=(nb,),
        in_specs=[pl.BlockSpec((1, total, W_SWA), lambda b, s: (b, 0, 0))],
        out_specs=pl.BlockSpec((1, total, GROUP_W), lambda b, s: (b, 0, 0)),
        scratch_shapes=[pltpu.VMEM((total + 2 * WINDOW, SWA_KV_W), BF16),
                        pltpu.VMEM((total + 2 * WINDOW, SWA_KV_W), BF16)],
    )
    return pl.pallas_call(
        functools.partial(_swa_body, lc=lc, sink0=l * N_HEADS),
        grid_spec=grid_spec,
        out_shape=jax.ShapeDtypeStruct((nb, total, GROUP_W), BF16),
        compiler_params=_cparams(("arbitrary",)),
        name="swa",
    )(sink, u)


def _gdn_body(u_ref, ab_ref, avec_ref, dtb_ref, ng_ref, o_ref,
              q_s, k_s, v_s, cs_s, gt_s, of_s, ob_s, st_s, *ring_refs, lc):
    ring = (ring_refs[0:4], ring_refs[4:8])
    total = u_ref.shape[1]
    gw = GROUP_W
    c = GDN_CHUNK
    nch = total // c
    n_ctx = lc // c
    per_blk = ROW_BLOCK // c
    ones_bd = _head_ones()

    ri = lax.broadcasted_iota(jnp.int32, (ROW_BLOCK, ROW_BLOCK), 0)
    ci_ = lax.broadcasted_iota(jnp.int32, (ROW_BLOCK, ROW_BLOCK), 1)
    same_chunk = (ri // c) == (ci_ // c)
    tri_lo = (same_chunk & (ri >= ci_)).astype(BF16)
    tri_up = (same_chunk & (ri <= ci_)).astype(BF16)
    lane = lax.broadcasted_iota(jnp.int32, (ROW_BLOCK, W_AB), 1)

    def phase_a(bi, r0):
        h = u_ref[0, r0:r0 + ROW_BLOCK, 0:3 * gw]
        hq, hk, hv = h[:, 0:gw], h[:, gw:2 * gw], h[:, 2 * gw:3 * gw]
        ssq = _split_dot(hq * hq, ones_bd, 2)
        q_s[r0:r0 + ROW_BLOCK, :] = hq * lax.rsqrt(ssq + NORM_EPS) * (HEAD_DIM ** -0.5)
        ssk = _split_dot(hk * hk, ones_bd, 2)
        k_s[r0:r0 + ROW_BLOCK, :] = hk * lax.rsqrt(ssk + NORM_EPS)
        v_s[r0:r0 + ROW_BLOCK, :] = hv
        ab = ab_ref[0, r0:r0 + ROW_BLOCK, :]
        g = -avec_ref[...] * _softplus(ab + dtb_ref[...])
        g = jnp.where(lane < 2 * N_HEADS, g, 0.0)
        beta = _sigmoid(ab)
        cs_lo = _split_dot_left(tri_lo, g, 3)
        cs_up = _split_dot_left(tri_up, g, 3)
        cs = jnp.where(lane < N_HEADS, cs_lo, jnp.where(lane < 2 * N_HEADS, cs_up, beta))
        cs_s[r0:r0 + ROW_BLOCK, :] = cs
        cst = cs.T
        for cc in range(per_blk):
            row0 = (bi * per_blk + cc) * SUBLANES
            piece = cst[0:SUBLANES, cc * c:(cc + 1) * c]
            rows = [jnp.concatenate([piece[hd:hd + 1, :], piece[hd + 1:hd + 2, :]], axis=1)
                    for hd in range(0, 2 * N_HEADS, 2)]
            rows.append(jnp.zeros((SUBLANES - len(rows), PAIR_W), F32))
            gt_s[row0:row0 + SUBLANES, :] = jnp.concatenate(rows, axis=0)

    def phase_a_gen(block_ids):
        for bi in block_ids:
            phase_a(bi, bi * ROW_BLOCK)
            yield

    ii = lax.broadcasted_iota(jnp.int32, (c, PAIR_W), 0)
    jj = lax.broadcasted_iota(jnp.int32, (c, PAIR_W), 1) % c
    eye = (ii == jj).astype(F32)
    bd_r = lax.broadcasted_iota(jnp.int32, (PAIR_W, PAIR_W), 0) // HEAD_DIM
    bd_c = lax.broadcasted_iota(jnp.int32, (PAIR_W, PAIR_W), 1) // HEAD_DIM
    bd_pair = bd_r == bd_c

    def bdiag(x):
        return jnp.where(bd_pair, jnp.concatenate([x, x], axis=0), jnp.zeros((), x.dtype))

    n_sq = c.bit_length() - 2
    incl = ((ii >= jj), (ii <= jj))
    strict = ((ii > jj), (ii < jj))
    last = (c - 1, 0)
    heads = [slice(h * HEAD_DIM, (h + 1) * HEAD_DIM) for h in range(N_HEADS)]
    def expander(n_terms, lane0):
        src = lax.broadcasted_iota(jnp.int32, (n_terms * W_AB, gw), 0) % W_AB
        grp = lax.broadcasted_iota(jnp.int32, (n_terms * W_AB, gw), 1) // HEAD_DIM
        return (src == lane0 + grp).astype(BF16)

    expand_g = [expander(3, d * N_HEADS) for d in range(2)]
    expand_b = [expander(2, (2 + d) * N_HEADS) for d in range(2)]

    blk_rows = GDN_B1_CHUNKS * c
    n_steps = total // blk_rows
    assert lc % blk_rows == 0 and n_steps >= 2

    def block_row0(j, d):
        if d == 0:
            return j * blk_rows
        if isinstance(j, int):
            return lc - (j + 1) * blk_rows if (j + 1) * blk_rows <= lc else total + lc - (j + 1) * blk_rows
        return pl.multiple_of(jnp.where((j + 1) * blk_rows <= lc, lc - (j + 1) * blk_rows,
                                        total + lc - (j + 1) * blk_rows), blk_rows)

    def cat_heads(items):
        return jnp.concatenate(items, axis=1)

    def b1_gen(j, slot):
        uvr_s, wqr_s, attr_s, kttr_s = ring[slot]
        probs = []
        for d in range(2):
            r0 = block_row0(j, d)
            parts = []
            rem = cs_s[pl.ds(r0, blk_rows), :]
            for _ in range(3):
                hi = rem.astype(BF16)
                parts.append(hi)
                rem = rem - hi.astype(F32)
            xp_g = jnp.dot(jnp.concatenate(parts, axis=1), expand_g[d], preferred_element_type=F32)
            xp_b = jnp.dot(jnp.concatenate(parts[0:2], axis=1), expand_b[d], preferred_element_type=F32)
            for cc in range(GDN_B1_CHUNKS):
                r = r0 + cc * c
                kc = k_s[pl.ds(r, c), :]
                qc = q_s[pl.ds(r, c), :]
                vc = v_s[pl.ds(r, c), :]
                gt = gt_s[pl.ds(pl.multiple_of((r0 // c + cc) * SUBLANES, SUBLANES), SUBLANES), :]
                sel = []
                for p2 in range(N_HEADS // 2):
                    ps = slice(p2 * PAIR_W, (p2 + 1) * PAIR_W)
                    kp, qp, vp = kc[:, ps], qc[:, ps], vc[:, ps]
                    kbd = bdiag(kp.astype(BF16))
                    kq = lax.dot_general(jnp.concatenate([kp, qp], axis=0).astype(BF16), kbd,
                                         (((1,), (1,)), ((), ())), preferred_element_type=F32)
                    kk, qk = kq[0:c, :], kq[c:2 * c, :]
                    col = xp_g[cc * c:(cc + 1) * c, ps]
                    bcol = xp_b[cc * c:(cc + 1) * c, ps]
                    rowv = gt[d * (N_HEADS // 2) + p2:d * (N_HEADS // 2) + p2 + 1, :]
                    decay = jnp.where(incl[d], jnp.exp(jnp.where(incl[d], col - rowv, 0.0)), 0.0)
                    n = -jnp.where(strict[d], kk * decay, 0.0) * bcol
                    eg = jnp.exp(col)
                    glast = col[last[d]:last[d] + 1, :]
                    rhs = jnp.concatenate([bdiag((vp * bcol).astype(BF16)),
                                           bdiag((kp * (bcol * eg)).astype(BF16))], axis=1)
                    pr = dict(cc=cc, d=d, a=(eye - n).astype(BF16), p=eye + n, rhs=rhs,
                              att=qk * decay, qe=qp * eg, kt=kp * jnp.exp(glast - col))
                    probs.append(pr)
                    sel.append(pr)
                wqr_s[d,(2 * cc + 1) * c:(2 * cc + 2) * c, :] = (
                    cat_heads([pr['qe'] for pr in sel]).astype(BF16))
                attr_s[d,cc * c:(cc + 1) * c, :] = cat_heads([pr['att'] for pr in sel]).astype(BF16)
                ktt = cat_heads([pr['kt'] for pr in sel]).T
                kttr_s[d,cc * gw:(cc + 1) * gw, :] = ktt.astype(BF16)
        yield
        for _ in range(n_sq):
            res = [eye - jnp.dot(pr['a'], bdiag(pr['p'].astype(BF16)), preferred_element_type=F32)
                   for pr in probs]
            yield
            for pr, rr in zip(probs, res):
                pr['p'] = pr['p'] + jnp.dot(pr['p'].astype(BF16), bdiag(rr.astype(BF16)),
                                            preferred_element_type=F32)
            yield
        for pr in probs:
            pr['uw'] = jnp.dot(pr['p'].astype(BF16), pr['rhs'], preferred_element_type=F32)
        yield
        for d in range(2):
            for cc in range(GDN_B1_CHUNKS):
                sel = [pr for pr in probs if pr['cc'] == cc and pr['d'] == d]
                uvr_s[d,cc * c:(cc + 1) * c, :] = cat_heads([pr['uw'][:, 0:PAIR_W] for pr in sel])
                wqr_s[d,2 * cc * c:(2 * cc + 1) * c, :] = (
                    cat_heads([pr['uw'][:, PAIR_W:2 * PAIR_W] for pr in sel]).astype(BF16))

    def b2_gen(j, slot):
        uvr_s, wqr_s, attr_s, kttr_s = ring[slot]
        r0s = [block_row0(j, d) for d in range(2)]
        states = [st_s[hd] for hd in range(2 * N_HEADS)]
        for step in range(GDN_B1_CHUNKS):
            probs = []
            for d in range(2):
                cc = step if d == 0 else GDN_B1_CHUNKS - 1 - step
                r = r0s[d] + cc * c
                wqm = wqr_s[d, 2 * cc * c:(2 * cc + 2) * c, :]
                uvm = uvr_s[d, cc * c:(cc + 1) * c, :]
                attm = attr_s[d, cc * c:(cc + 1) * c, :]
                grow = cs_s[pl.ds(r + last[d], 1), :]
                for h in range(N_HEADS):
                    hd = d * N_HEADS + h
                    kt = kttr_s[d, cc * gw + h * HEAD_DIM:cc * gw + (h + 1) * HEAD_DIM, :]
                    probs.append(dict(d=d, hd=hd, r=r, wqm=wqm[:, heads[h]],
                                      uv=uvm[:, heads[h]], att=attm[:, heads[h]], kt=kt,
                                      gl=jnp.exp(grow[:, hd:hd + 1])))
            for pr in probs:
                pr['wq'] = jnp.dot(pr['wqm'], states[pr['hd']].astype(BF16), preferred_element_type=F32)
            yield
            for pr in probs:
                vn = (pr['uv'] - pr['wq'][0:c, :]).astype(BF16)
                pr['o'] = pr['wq'][c:2 * c, :] + jnp.dot(pr['att'], vn, preferred_element_type=F32)
                states[pr['hd']] = (states[pr['hd']] * pr['gl']
                                    + jnp.dot(pr['kt'], vn, preferred_element_type=F32))
            for d, o_s in ((0, of_s), (1, ob_s)):
                sel = [pr for pr in probs if pr['d'] == d]
                o_s[pl.ds(sel[0]['r'], c), :] = cat_heads([pr['o'] for pr in sel])
            yield
        for hd in range(2 * N_HEADS):
            st_s[hd] = states[hd]

    def run_interleaved(gens):
        live = list(gens)
        while live:
            still = []
            for g in live:
                try:
                    next(g)
                    still.append(g)
                except StopIteration:
                    pass
            live = still

    assert blk_rows == ROW_BLOCK
    n_blocks = total // ROW_BLOCK
    n_pre = lc // ROW_BLOCK
    for bi in range(n_pre):
        phase_a(bi, bi * ROW_BLOCK)
    st_s[...] = jnp.zeros(st_s.shape, F32)
    run_interleaved([b1_gen(0, 0), phase_a_gen(range(n_pre, n_blocks))])

    def pipe_body(t, carry):
        j = 1 + 2 * t
        run_interleaved([b1_gen(j, 1), b2_gen(j - 1, 0)])
        run_interleaved([b1_gen(j + 1, 0), b2_gen(j, 1)])
        return carry

    lax.fori_loop(0, (n_steps - 1) // 2, pipe_body, 0)
    if (n_steps - 1) % 2:
        run_interleaved([b1_gen(n_steps - 1, (n_steps - 1) % 2), b2_gen(n_steps - 2, n_steps % 2)])
    def finish(r0):
        o = of_s[r0:r0 + ROW_BLOCK, :] + ob_s[r0:r0 + ROW_BLOCK, :]
        ms = _split_dot(o * o, ones_bd, 2) * (1.0 / HEAD_DIM)
        gate = u_ref[0, r0:r0 + ROW_BLOCK, 3 * gw:4 * gw]
        y = o * lax.rsqrt(ms + NORM_EPS) * ng_ref[...] * gate
        o_ref[0, r0:r0 + ROW_BLOCK, :] = y.astype(BF16)

    def finish_gen(rows):
        for r0 in rows:
            finish(r0)
            yield

    last_rows = [block_row0(n_steps - 1, d) for d in range(2)]
    early = [bi * ROW_BLOCK for bi in range(n_blocks) if bi * ROW_BLOCK not in last_rows]
    run_interleaved([b2_gen(n_steps - 1, (n_steps - 1) % 2), finish_gen(early)])
    for r0 in sorted(set(last_rows)):
        finish(r0)


def _gdn_params(a_log, dt_bias, norm_g):
    depth = a_log.shape[0]
    pad = ((0, 0), (0, 0), (0, W_AB - 2 * N_HEADS))
    avec = jnp.pad(jnp.exp(a_log.reshape(depth, 1, 2 * N_HEADS)), pad)
    dtb = jnp.pad(dt_bias.reshape(depth, 1, 2 * N_HEADS), pad)
    ng = jnp.tile(norm_g.reshape(depth, 1, HEAD_DIM), (1, 1, N_HEADS))
    return avec, dtb, ng


def _gdn(u, ab, avec, dtb, ng, l, *, lc):
    nb, total, _ = u.shape
    nch = total // GDN_CHUNK
    blk = GDN_B1_CHUNKS * GDN_CHUNK
    return pl.pallas_call(
        functools.partial(_gdn_body, lc=lc),
        grid=(nb,),
        in_specs=[pl.BlockSpec((1, total, W_GDN), lambda b: (b, 0, 0)),
                  pl.BlockSpec((1, total, W_AB), lambda b: (b, 0, 0)),
                  _layer(avec, l), _layer(dtb, l), _layer(ng, l)],
        out_specs=pl.BlockSpec((1, total, GROUP_W), lambda b: (b, 0, 0)),
        out_shape=jax.ShapeDtypeStruct((nb, total, GROUP_W), BF16),
        scratch_shapes=[pltpu.VMEM((total, GROUP_W), F32),
                        pltpu.VMEM((total, GROUP_W), F32),
                        pltpu.VMEM((total, GROUP_W), F32),
                        pltpu.VMEM((total, W_AB), F32),
                        pltpu.VMEM((nch * SUBLANES, PAIR_W), F32),
                        pltpu.VMEM((total, GROUP_W), F32),
                        pltpu.VMEM((total, GROUP_W), F32),
                        pltpu.VMEM((2 * N_HEADS, HEAD_DIM, HEAD_DIM), F32)] + 2 * [
                        pltpu.VMEM((2, blk, GROUP_W), F32),
                        pltpu.VMEM((2, 2 * blk, GROUP_W), BF16),
                        pltpu.VMEM((2, blk, GROUP_W), BF16),
                        pltpu.VMEM((2, GDN_B1_CHUNKS * GROUP_W, GDN_CHUNK), BF16)],
        compiler_params=_cparams(("arbitrary",)),
        name="gdn",
    )(u, ab, avec, dtb, ng)


def _pack_w_in(w_in):
    gw = GROUP_W
    o_gdn = 2 * gw
    o_ab = o_gdn + 4 * gw
    o_ret = o_ab + 4 * N_HEADS
    o_swa = o_ret + 4 * gw
    end = o_swa + W_SWA
    assert end == w_in.shape[-1]
    wb = w_in.astype(BF16)
    pad = jnp.zeros(wb.shape[:-1] + (W_AB - 4 * N_HEADS,), BF16)
    return jnp.concatenate([wb[..., 0:o_ab], wb[..., o_ret:end], wb[..., o_ab:o_ret], pad], axis=-1)


def _rope_tables(ang, lc):
    cos = jnp.cos(ang)
    sin = jnp.sin(ang)
    cos_h = jnp.concatenate([cos, cos], axis=-1)
    sin_h = jnp.concatenate([-sin, sin], axis=-1)
    cos_t = jnp.tile(cos_h, (1, N_HEADS))
    sin_t = jnp.tile(sin_h, (1, N_HEADS))
    cos_t = jnp.concatenate([jnp.ones((lc, GROUP_W), F32), cos_t], axis=0)
    sin_t = jnp.concatenate([jnp.zeros((lc, GROUP_W), F32), sin_t], axis=0)
    return cos_t, sin_t


def _rope_freqs(pos, n):
    inv = ROPE_BASE ** (-jnp.arange(0, n, 2, dtype=F32) / n)
    return pos[:, None] * inv[None, :]


def kernel(x, c, ctx, c_ctx, w_mod, b_mod, pre_norm_g, post_norm_g, w_in, w_out, lru_conv_w, lru_conv_b, lru_w_r, lru_b_r, lru_w_i, lru_b_i, lru_lambda, gdn_conv_w, gdn_a_log, gdn_dt_bias, gdn_norm_g, ret_decay_logit, swa_sink):
    nb, t, d = x.shape
    lc = ctx.shape[1]
    depth = w_mod.shape[0]
    assert t % ROW_BLOCK == 0 and lc % ROW_BLOCK == 0 and d == 4 * GROUP_W

    rows = t // GRID_W
    row = jnp.repeat(jnp.arange(rows, dtype=F32), GRID_W)
    col = jnp.tile(jnp.arange(GRID_W, dtype=F32), rows)
    ang2d = jnp.concatenate([_rope_freqs(row, HEAD_DIM // 2), _rope_freqs(col, HEAD_DIM // 2)], axis=-1)
    ang1d = _rope_freqs(jnp.arange(t, dtype=F32), HEAD_DIM)
    cos1, sin1 = _rope_tables(ang1d, lc)
    cos2, sin2 = _rope_tables(ang2d, lc)

    mod_rows = -(-(nb + 1) // SUBLANES) * SUBLANES
    s_rows = jnp.concatenate([c, c_ctx[None, :], jnp.zeros((mod_rows - nb - 1, d), F32)], axis=0)
    mod = _modulation(s_rows, w_mod, b_mod)

    w_in_p = _pack_w_in(w_in)
    w_out_b = w_out.astype(BF16)
    pre_g = pre_norm_g.reshape(depth, 1, d)
    post_g = post_norm_g.reshape(depth, 1, d)
    lru_cb, lru_wg, lru_bg, lru_lam = _lru_params(lru_conv_b, lru_w_r, lru_b_r, lru_w_i, lru_b_i, lru_lambda)
    gdn_avec, gdn_dtb, gdn_ng = _gdn_params(gdn_a_log, gdn_dt_bias, gdn_norm_g)
    ret_lg = jnp.repeat(ret_decay_logit, HEAD_DIM, axis=-1)
    sink = swa_sink.reshape(depth * N_HEADS)
    xs = jnp.concatenate([ctx, x], axis=1)
    for l in range(depth):
        u_lru, u_gdn, u_ret, u_swa, u_ab = _in_proj(xs, mod, pre_g, w_in_p, (cos1, sin1, cos2, sin2),
                                                    (lru_conv_w, lru_cb, gdn_conv_w), l, lc=lc)
        ya = _lru(u_lru, lru_wg, lru_bg, lru_lam, l, lc=lc)
        yb = _gdn(u_gdn, u_ab, gdn_avec, gdn_dtb, gdn_ng, l, lc=lc)
        yc = _ret(u_ret, ret_lg, l, lc=lc)
        yd = _swa(u_swa, sink, l, lc=lc)
        xs = _out_proj((ya, yb, yc, yd), xs, mod, post_g, w_out_b, l, lc=lc, latent_only=l == depth - 1)
    return xs
```

```python
import functools
import math

import jax
import jax.numpy as jnp
from jax import lax
from jax.experimental import pallas as pl
from jax.experimental.pallas import tpu as pltpu

F32 = jnp.float32
BF16 = jnp.bfloat16

HEAD_DIM = 64
GROUP_W = 256
N_HEADS = GROUP_W // HEAD_DIM
NORM_EPS = 1e-6
ROPE_BASE = 10000.0
NEG_INF = -1e30
GRID_W = 64
CONV_W = 4
LRU_C = 8.0
LRU_SCAN_UNROLL = 4
GDN_CHUNK = 64
PAIR_W = 2 * HEAD_DIM
GDN_B1_CHUNKS = 4
RET_CHUNK = 256
RET_CHUNKS_PER_STEP = 5
SWA_KV_HEADS = 2
SWA_KV_W = SWA_KV_HEADS * HEAD_DIM
WINDOW = 128
SWA_BLOCKS_PER_STEP = 2
ROW_BLOCK = 256
CONV_ROWS = 64
PROJ_ROWS_MAX = 576
SUBLANES = 8
VMEM_LIMIT_BYTES = 56 * 1024 * 1024

W_LRU = 2 * GROUP_W
W_GDN = 4 * GROUP_W
W_RET = 4 * GROUP_W
W_SWA = 2 * GROUP_W + 2 * SWA_KV_W
W_AB = 128
IN_W_PACKED = W_LRU + W_GDN + W_RET + W_SWA + W_AB


def _cparams(sem, flags=None):
    return pltpu.CompilerParams(dimension_semantics=sem, vmem_limit_bytes=VMEM_LIMIT_BYTES, flags=flags)


def _dot(a, b):
    return jnp.dot(a.astype(BF16), b.astype(BF16), preferred_element_type=F32)


def _dot_nt(a, b):
    return lax.dot_general(a.astype(BF16), b.astype(BF16), (((1,), (1,)), ((), ())),
                           preferred_element_type=F32)


def _dot_tn(a, b):
    return lax.dot_general(a.astype(BF16), b.astype(BF16), (((0,), (0,)), ((), ())),
                           preferred_element_type=F32)


def _split_dot(x, w, parts):
    acc = None
    r = x
    for _ in range(parts):
        hi = r.astype(BF16)
        t = jnp.dot(hi, w, preferred_element_type=F32)
        acc = t if acc is None else acc + t
        r = r - hi.astype(F32)
    return acc


def _split_dot_left(w, x, parts):
    acc = None
    r = x
    for _ in range(parts):
        hi = r.astype(BF16)
        t = jnp.dot(w, hi, preferred_element_type=F32)
        acc = t if acc is None else acc + t
        r = r - hi.astype(F32)
    return acc


def _sigmoid(x):
    return 0.5 * jnp.tanh(0.5 * x) + 0.5


def _silu(x):
    return x * _sigmoid(x)


def _softplus(x):
    return jnp.maximum(x, 0.0) + jnp.log1p(jnp.exp(-jnp.abs(x)))


def _head_ones():
    r = lax.broadcasted_iota(jnp.int32, (GROUP_W, GROUP_W), 0) // HEAD_DIM
    c = lax.broadcasted_iota(jnp.int32, (GROUP_W, GROUP_W), 1) // HEAD_DIM
    return (r == c).astype(BF16)


def _row_blocks(lc, total):
    out = []
    for r0 in range(0, total, ROW_BLOCK):
        out.append((r0, 0, lc) if r0 < lc else (r0, lc, total))
    return out


def _mod_body(s_ref, w_ref, b_ref, o_ref):
    s = _silu(s_ref[...])
    o_ref[0] = _dot(s, w_ref[0]) + b_ref[0]


def _modulation(s_rows, w_mod, b_mod):
    depth, d, d3 = w_mod.shape
    rows = s_rows.shape[0]
    nt = d3 // d
    return pl.pallas_call(
        _mod_body,
        grid=(depth, nt),
        in_specs=[pl.BlockSpec((rows, d), lambda l, j: (0, 0)),
                  pl.BlockSpec((1, d, d), lambda l, j: (l, 0, j)),
                  pl.BlockSpec((1, 1, d), lambda l, j: (l, 0, j))],
        out_specs=pl.BlockSpec((1, rows, d), lambda l, j: (l, 0, j)),
        out_shape=jax.ShapeDtypeStruct((depth, rows, d3), F32),
        compiler_params=_cparams(("arbitrary", "arbitrary")),
        name="modulation",
    )(s_rows, w_mod, b_mod.reshape(depth, 1, d3))


def _proj_rows(total):
    for tm in range(PROJ_ROWS_MAX, SUBLANES - 1, -SUBLANES):
        if total % tm == 0:
            return tm
    raise ValueError(total)


def _mod_rows(mod_ref, b, i, tm, lc, nb, c0, c1):
    lat = mod_ref[pl.ds(b, 1), c0:c1]
    if lc % tm == 0:
        ctx = mod_ref[nb:nb + 1, c0:c1]
        return jnp.where(i * tm < lc, ctx, lat)
    ctx = mod_ref[nb:nb + 1, c0:c1]
    row = i * tm + lax.broadcasted_iota(jnp.int32, (tm, 1), 0)
    return jnp.where(row < lc, ctx, lat)


def _inproj_body(x_ref, xp_ref, xn_ref, mod_ref, g_ref, w_ref, cos1_ref, sin1_ref, cos2_ref, sin2_ref,
                 lcw_ref, lcb_ref, gcw_ref, o_lru, o_gdn, o_ret, o_swa, o_ab, *conv_scr, lc, nb, total):
    i = pl.program_id(0)
    b = pl.program_id(1)
    d = x_ref.shape[2]
    tm = x_ref.shape[1]
    gw = GROUP_W
    scale = HEAD_DIM ** -0.5

    def prenorm(xv, m):
        ms = jnp.mean(xv * xv, axis=-1, keepdims=True)
        y = xv * lax.rsqrt(ms + NORM_EPS) * g_ref[...]
        return (y * (1.0 + m[:, d:2 * d]) + m[:, 0:d]).astype(BF16)

    h = prenorm(x_ref[0], _mod_rows(mod_ref, b, i, tm, lc, nb, 0, 2 * d))
    halo = prenorm(jnp.concatenate([xp_ref[0], xn_ref[0]], axis=0), mod_ref[pl.ds(b, 1), 0:2 * d])
    h_scr = conv_scr[4]
    h_scr[0:tm, :] = h
    h_scr[tm:tm + 2 * SUBLANES, :] = halo
    h_all = slice(0, tm + 2 * SUBLANES)

    def proj(c0, width, rows=slice(0, tm)):
        return jnp.dot(h_scr[rows, :], w_ref[:, c0:c0 + width], preferred_element_type=F32)

    nt = total // tm
    b_tile, b_row = lc // tm, lc % tm
    assert b_row % SUBLANES == 0 and (b_row == 0 or SUBLANES <= b_row <= tm - SUBLANES)
    prev_ok = i > 0
    next_ok = i < nt - 1
    if b_row == 0:
        prev_ok = prev_ok & (i != b_tile)
        next_ok = next_ok & (i != b_tile - 1)
    prev_f = jnp.where(prev_ok, 1.0, 0.0)
    next_f = jnp.where(next_ok, 1.0, 0.0)

    def dwconv_to(o_ref, oc0, u_all, scr, cw_ref, c0, post):
        scr[0:SUBLANES, :] = u_all[tm:tm + SUBLANES] * prev_f
        scr[SUBLANES:SUBLANES + tm, :] = u_all[0:tm]
        scr[SUBLANES + tm:2 * SUBLANES + tm, :] = u_all[tm + SUBLANES:tm + 2 * SUBLANES] * next_f
        for r0 in range(0, tm, CONV_ROWS):
            n = min(CONV_ROWS, tm - r0)
            near_boundary = b_row and r0 - 1 <= b_row <= r0 + n + 1
            acc = None
            for k in range(CONV_W):
                off = k - 2
                tap = scr[SUBLANES + r0 + off:SUBLANES + r0 + off + n, :]
                if near_boundary and off != 0:
                    r = r0 + off + lax.broadcasted_iota(jnp.int32, (n, 1), 0)
                    keep = ((r >= b_row) == (r - off >= b_row)) | (i != b_tile)
                    tap = jnp.where(keep, tap, 0.0)
                term = tap * cw_ref[k:k + 1, c0:c0 + gw]
                acc = term if acc is None else acc + term
            o_ref[0, r0:r0 + n, oc0:oc0 + gw] = post(acc)

    c_gdn, c_ret, c_swa, c_ab = W_LRU, W_LRU + W_GDN, W_LRU + W_GDN + W_RET, W_LRU + W_GDN + W_RET + W_SWA
    kvw = SWA_KV_W

    def ep_lru_x(u):
        dwconv_to(o_lru, 0, u, conv_scr[0], lcw_ref, 0, lambda a: a + lcb_ref[...])

    def ep_gdn_conv(p):
        return lambda u: dwconv_to(o_gdn, p * gw, u, conv_scr[1 + p], gcw_ref, p * gw, _silu)

    def ep_gates(u):
        o_lru[0, :, gw:2 * gw] = _silu(u[:, 0:gw])
        o_gdn[0, :, 3 * gw:4 * gw] = _silu(u[:, gw:2 * gw])

    def ep_ret(u):
        cos, sin = cos1_ref[...], sin1_ref[...]
        o_ret[0, :, 0:gw] = _rope(u[:, 0:gw], cos, sin)
        o_ret[0, :, gw:2 * gw] = _rope(u[:, gw:2 * gw], cos, sin) * scale
        o_ret[0, :, 2 * gw:3 * gw] = u[:, 2 * gw:3 * gw]
        o_ret[0, :, 3 * gw:4 * gw] = _silu(u[:, 3 * gw:4 * gw])

    def ep_swa(u):
        cos, sin = cos2_ref[...], sin2_ref[...]
        o_swa[0, :, 0:gw] = _rope(u[:, 0:gw], cos, sin) * scale
        o_swa[0, :, gw:gw + kvw] = _rope(u[:, gw:gw + kvw], cos[:, 0:kvw], sin[:, 0:kvw])
        o_swa[0, :, gw + kvw:gw + 2 * kvw] = u[:, gw + kvw:gw + 2 * kvw]
        o_swa[0, :, gw + 2 * kvw:2 * gw + 2 * kvw] = _silu(u[:, gw + 2 * kvw:2 * gw + 2 * kvw])

    def ep_ab(u):
        o_ab[0] = u

    def gates_proj():
        return jnp.concatenate([proj(gw, gw), proj(c_gdn + 3 * gw, gw)], axis=1)

    work = [(lambda: proj(0, gw, h_all), ep_lru_x),
            (lambda: proj(c_ret, W_RET), ep_ret),
            (lambda: proj(c_gdn, gw, h_all), ep_gdn_conv(0)),
            (lambda: proj(c_swa, W_SWA), ep_swa),
            (lambda: proj(c_gdn + gw, gw, h_all), ep_gdn_conv(1)),
            (gates_proj, ep_gates),
            (lambda: proj(c_gdn + 2 * gw, gw, h_all), ep_gdn_conv(2)),
            (lambda: proj(c_ab, W_AB), ep_ab)]
    pending = work[0][0]()
    for k, (_, epilogue) in enumerate(work):
        cur = pending
        if k + 1 < len(work):
            pending = work[k + 1][0]()
        epilogue(cur)


def _in_proj(x, mod, g, w, tables, conv_params, l, *, lc):
    nb, total, d = x.shape
    tm = _proj_rows(total)
    nt = total // tm
    per = tm // SUBLANES
    last = total // SUBLANES - 1
    widths = (W_LRU, W_GDN, W_RET, W_SWA, W_AB)
    tspec = pl.BlockSpec((tm, GROUP_W), lambda i, b: (i, 0))
    return pl.pallas_call(
        functools.partial(_inproj_body, lc=lc, nb=nb, total=total),
        grid=(nt, nb),
        in_specs=[pl.BlockSpec((1, tm, d), lambda i, b: (b, i, 0)),
                  pl.BlockSpec((1, SUBLANES, d), lambda i, b: (b, jnp.maximum(i * per - 1, 0), 0)),
                  pl.BlockSpec((1, SUBLANES, d), lambda i, b: (b, jnp.minimum((i + 1) * per, last), 0)),
                  _layer(mod, l), _layer(g, l), _layer(w, l), tspec, tspec, tspec, tspec]
                 + [_layer(p, l) for p in conv_params],
        out_specs=[pl.BlockSpec((1, tm, wd), lambda i, b: (b, i, 0)) for wd in widths],
        out_shape=[jax.ShapeDtypeStruct((nb, total, wd), F32) for wd in widths],
        scratch_shapes=[pltpu.VMEM((tm + 2 * SUBLANES, GROUP_W), F32) for _ in range(4)]
                       + [pltpu.VMEM((tm + 2 * SUBLANES, d), BF16)],
        compiler_params=_cparams(("arbitrary", "arbitrary")),
        name="in_proj",
    )(x, x, x, mod, g, w, *tables, *conv_params)


def _outproj_body(ya, yb, yc, yd, x_ref, mod_ref, g_ref, w_ref, o_ref, *, lc, nb, tile0):
    b = pl.program_id(0)
    i = pl.program_id(1) + tile0
    d = x_ref.shape[2]
    acc = None
    for k, y_ref in enumerate((ya, yb, yc, yd)):
        t = jnp.dot(y_ref[0], w_ref[k * GROUP_W:(k + 1) * GROUP_W, :], preferred_element_type=F32)
        acc = t if acc is None else acc + t
    ms = jnp.mean(acc * acc, axis=-1, keepdims=True)
    yn = acc * lax.rsqrt(ms + NORM_EPS) * g_ref[...]
    gate = _mod_rows(mod_ref, b, i, acc.shape[0], lc, nb, 2 * d, 3 * d)
    o_ref[0] = x_ref[0] + gate * yn


def _out_proj(ys, x, mod, g, w, l, *, lc, latent_only=False):
    nb, total, d = x.shape
    if latent_only:
        tm = _proj_rows(math.gcd(lc, total - lc))
        tile0, rows_out = lc // tm, total - lc
    else:
        tm = _proj_rows(total)
        tile0, rows_out = 0, total
    yspec = pl.BlockSpec((1, tm, GROUP_W), lambda b, i: (b, i + tile0, 0))
    return pl.pallas_call(
        functools.partial(_outproj_body, lc=lc, nb=nb, tile0=tile0),
        grid=(nb, rows_out // tm),
        in_specs=[yspec, yspec, yspec, yspec,
                  pl.BlockSpec((1, tm, d), lambda b, i: (b, i + tile0, 0)),
                  _layer(mod, l), _layer(g, l), _layer(w, l)],
        out_specs=pl.BlockSpec((1, tm, d), lambda b, i: (b, i, 0)),
        out_shape=jax.ShapeDtypeStruct((nb, rows_out, d), F32),
        compiler_params=_cparams(("arbitrary", "arbitrary")),
        name="out_proj",
    )(*ys, x, mod, g, w)


def _lru_scan(a_s, b_s, h_s, tile_lo, n_tiles, carry, *, rev, accumulate):
    row = lax.broadcasted_iota(jnp.int32, (SUBLANES, GROUP_W), 0)

    def body(j, carry):
        t = tile_lo + (n_tiles - 1 - j if rev else j)
        r = pl.multiple_of(t * SUBLANES, SUBLANES)
        a = a_s[pl.ds(r, SUBLANES), :]
        b = b_s[pl.ds(r, SUBLANES), :]
        for s in (1, 2, 4):
            if rev:
                ra = pltpu.roll(a, SUBLANES - s, 0)
                rb = pltpu.roll(b, SUBLANES - s, 0)
                m = row < SUBLANES - s
            else:
                ra = pltpu.roll(a, s, 0)
                rb = pltpu.roll(b, s, 0)
                m = row >= s
            b = a * jnp.where(m, rb, 0.0) + b
            a = a * jnp.where(m, ra, 1.0)
        h = a * carry + b
        if accumulate:
            h_s[pl.ds(r, SUBLANES), :] = h_s[pl.ds(r, SUBLANES), :] + h
        else:
            h_s[pl.ds(r, SUBLANES), :] = h
        return h[0:1, :] if rev else h[SUBLANES - 1:SUBLANES, :]

    return lax.fori_loop(0, n_tiles, body, carry, unroll=LRU_SCAN_UNROLL)


def _lru_body(u_ref, wg_ref, bg_ref, lam_ref, o_ref, a_s, b_s, h_s, *, lc):
    total = u_ref.shape[1]
    blocks = _row_blocks(lc, total)
    zero = jnp.zeros((1, GROUP_W), F32)
    for d in range(2):
        sp = _softplus(-lam_ref[d])
        for r0, _, _ in blocks:
            uc = u_ref[0, r0:r0 + ROW_BLOCK, 0:GROUP_W]
            gts = _dot(uc, wg_ref[d]) + bg_ref[d]
            r = _sigmoid(gts[:, 0:GROUP_W])
            ig = _sigmoid(gts[:, GROUP_W:2 * GROUP_W])
            a = jnp.exp(-LRU_C * r * sp)
            a_s[r0:r0 + ROW_BLOCK, :] = a
            b_s[r0:r0 + ROW_BLOCK, :] = jnp.sqrt(1.0 - a * a) * (ig * uc)
        ct, tt = lc // SUBLANES, total // SUBLANES
        if d == 0:
            _lru_scan(a_s, b_s, h_s, 0, tt, zero, rev=False, accumulate=False)
        else:
            carry = _lru_scan(a_s, b_s, h_s, 0, ct, zero, rev=True, accumulate=True)
            _lru_scan(a_s, b_s, h_s, ct, tt - ct, carry, rev=True, accumulate=True)
    for r0, _, _ in blocks:
        gate = u_ref[0, r0:r0 + ROW_BLOCK, GROUP_W:2 * GROUP_W]
        o_ref[0, r0:r0 + ROW_BLOCK, :] = (h_s[r0:r0 + ROW_BLOCK, :] * gate).astype(BF16)


def _block_diag(w):
    n, c = w.shape[-3], w.shape[-1]
    eye = jnp.eye(n, dtype=w.dtype)
    return (eye[:, None, :, None] * w[..., :, :, None, :]).reshape(w.shape[:-3] + (n * c, n * c))


def _layer(arr, l):
    shape = arr.shape[1:]
    return pl.BlockSpec((None,) + shape, lambda *_: (l,) + (0,) * len(shape))


def _lru_params(conv_b, w_r, b_r, w_i, b_i, lam):
    depth = conv_b.shape[0]
    wg = jnp.concatenate([_block_diag(w_r), _block_diag(w_i)], axis=-1).astype(BF16)
    bg = jnp.concatenate([b_r, b_i], axis=-1).reshape(depth, 2, 1, 2 * GROUP_W)
    return conv_b.reshape(depth, 1, GROUP_W), wg, bg, lam.reshape(depth, 2, 1, GROUP_W)


def _lru(u, wg, bg, lam, l, *, lc):
    nb, total, _ = u.shape
    return pl.pallas_call(
        functools.partial(_lru_body, lc=lc),
        grid=(nb,),
        in_specs=[pl.BlockSpec((1, total, W_LRU), lambda b: (b, 0, 0)),
                  _layer(wg, l), _layer(bg, l), _layer(lam, l)],
        out_specs=pl.BlockSpec((1, total, GROUP_W), lambda b: (b, 0, 0)),
        out_shape=jax.ShapeDtypeStruct((nb, total, GROUP_W), BF16),
        scratch_shapes=[pltpu.VMEM((total, GROUP_W), F32) for _ in range(3)],
        compiler_params=_cparams(("arbitrary",)),
        name="lru",
    )(u, wg, bg, lam)


def _rope(x, cos, sin_signed):
    half = HEAD_DIM // 2
    outs = []
    for c0 in range(0, x.shape[1], 128):
        xs = x[:, c0:c0 + 128]
        lane = lax.broadcasted_iota(jnp.int32, xs.shape, 1)
        swapped = jnp.where((lane % HEAD_DIM) < half,
                            pltpu.roll(xs, 128 - half, 1), pltpu.roll(xs, half, 1))
        outs.append(xs * cos[:, c0:c0 + 128] + swapped * sin_signed[:, c0:c0 + 128])
    return outs[0] if len(outs) == 1 else jnp.concatenate(outs, axis=1)


def _ret_body(u_ref, lg_ref, o_ref, ds_s, st_s, m_s, *, lc):
    total = u_ref.shape[1]
    c = RET_CHUNK
    nch = total // c
    n_ctx = lc // c
    gw = GROUP_W
    ones_bd = _head_ones()
    bd_mask = ones_bd > 0
    lg = -_softplus(-lg_ref[...])
    lgf, lgb = lg[0:1, :], lg[1:2, :]
    pos = lax.broadcasted_iota(jnp.int32, (c, 1), 0).astype(F32)
    qdec_f = jnp.exp((pos + 1.0) * lgf)
    qdec_b = jnp.exp((c - pos) * lgb)
    kdec_f = jnp.exp((c - 1.0 - pos) * lgf)
    kdec_b = jnp.exp(pos * lgb)
    cdec_f = jnp.exp(float(c) * lgf)
    cdec_b = jnp.exp(float(c) * lgb)
    dij = (lax.broadcasted_iota(jnp.int32, (c, c), 0)
           - lax.broadcasted_iota(jnp.int32, (c, c), 1)).astype(F32)
    for h in range(N_HEADS):
        lf = lgf[:, h * HEAD_DIM:h * HEAD_DIM + 1]
        lb = lgb[:, h * HEAD_DIM:h * HEAD_DIM + 1]
        fwd = jnp.exp(jnp.maximum(dij, 0.0) * lf)
        bwd = jnp.exp(jnp.maximum(-dij, 0.0) * lb)
        m_s[h] = jnp.where(dij > 0, fwd, jnp.where(dij < 0, bwd, 2.0))

    for ci in range(nch):
        r0 = ci * c
        kr = u_ref[0, r0:r0 + c, gw:2 * gw]
        v = u_ref[0, r0:r0 + c, 2 * gw:3 * gw]
        ds_s[0, ci] = jnp.where(bd_mask, _dot_tn(kr * kdec_f, v), 0.0)
        ds_s[1, ci] = jnp.where(bd_mask, _dot_tn(kr * kdec_b, v), 0.0)

    s = jnp.zeros((gw, gw), F32)
    for ci in range(nch):
        st_s[0, ci] = s
        s = s * cdec_f + ds_s[0, ci]
    s = jnp.zeros((gw, gw), F32)
    for ci in list(range(n_ctx - 1, -1, -1)) + list(range(nch - 1, n_ctx - 1, -1)):
        st_s[1, ci] = s
        s = s * cdec_b + ds_s[1, ci]

    hss = [slice(h * HEAD_DIM, (h + 1) * HEAD_DIM) for h in range(N_HEADS)]
    for ci0 in range(0, nch, RET_CHUNKS_PER_STEP):
        cis = list(range(ci0, min(ci0 + RET_CHUNKS_PER_STEP, nch)))
        qrs = [u_ref[0, ci * c:(ci + 1) * c, 0:gw] for ci in cis]
        krs = [u_ref[0, ci * c:(ci + 1) * c, gw:2 * gw] for ci in cis]
        vs = [u_ref[0, ci * c:(ci + 1) * c, 2 * gw:3 * gw].astype(BF16) for ci in cis]
        atts = [[_dot_nt(qr[:, hs], kr[:, hs]) for hs in hss] for qr, kr in zip(qrs, krs)]
        inters = [_dot(qr * qdec_f, st_s[0, ci]) + _dot(qr * qdec_b, st_s[1, ci]) for ci, qr in zip(cis, qrs)]
        atts = [[(a * m_s[h]).astype(BF16) for h, a in enumerate(att)] for att in atts]
        os_ = [inter + jnp.concatenate(
            [jnp.dot(a, v[:, hs], preferred_element_type=F32) for a, hs in zip(att, hss)], axis=1)
            for inter, att, v in zip(inters, atts, vs)]
        mus = [_split_dot(o, ones_bd, 2) * (1.0 / HEAD_DIM) for o in os_]
        dlts = [o - mu for o, mu in zip(os_, mus)]
        vrs = [_split_dot(dlt * dlt, ones_bd, 2) * (1.0 / HEAD_DIM) for dlt in dlts]
        for ci, dlt, var in zip(cis, dlts, vrs):
            gate = u_ref[0, ci * c:(ci + 1) * c, 3 * gw:4 * gw]
            o_ref[0, ci * c:(ci + 1) * c, :] = (dlt * lax.rsqrt(var + NORM_EPS) * gate).astype(BF16)


def _ret(u, lg, l, *, lc):
    nb, total, _ = u.shape
    nch = total // RET_CHUNK
    return pl.pallas_call(
        functools.partial(_ret_body, lc=lc),
        grid=(nb,),
        in_specs=[pl.BlockSpec((1, total, W_RET), lambda b: (b, 0, 0)), _layer(lg, l)],
        out_specs=pl.BlockSpec((1, total, GROUP_W), lambda b: (b, 0, 0)),
        out_shape=jax.ShapeDtypeStruct((nb, total, GROUP_W), BF16),
        scratch_shapes=[pltpu.VMEM((2, nch, GROUP_W, GROUP_W), F32),
                        pltpu.VMEM((2, nch, GROUP_W, GROUP_W), F32),
                        pltpu.VMEM((N_HEADS, RET_CHUNK, RET_CHUNK), F32)],
        compiler_params=_cparams(("arbitrary",)),
        name="retention",
    )(u, lg)


def _swa_body(sink_ref, u_ref, o_ref, k_s, v_s, *, lc, sink0):
    total = u_ref.shape[1]
    t_lat = total - lc
    gw = GROUP_W
    blk = WINDOW
    nblk = t_lat // blk
    kv0 = lc + blk
    zpad = jnp.zeros((blk, SWA_KV_W), BF16)
    for s_ref in (k_s, v_s):
        s_ref[lc:lc + blk, :] = zpad
        s_ref[kv0 + t_lat:kv0 + t_lat + blk, :] = zpad
    for r0, _, _ in _row_blocks(lc, total):
        dst = r0 if r0 < lc else r0 + blk
        k_s[dst:dst + ROW_BLOCK, :] = u_ref[0, r0:r0 + ROW_BLOCK, gw:gw + SWA_KV_W].astype(BF16)
        v_s[dst:dst + ROW_BLOCK, :] = (
            u_ref[0, r0:r0 + ROW_BLOCK, gw + SWA_KV_W:gw + 2 * SWA_KV_W].astype(BF16))

    grp = N_HEADS // SWA_KV_HEADS

    def attend(items):
        scores = [[_dot_nt(q2, kk) for kk in keys] for q2, keys, _, _, _ in items]
        exps, dens = [], []
        for (_, _, _, masks, sink_col), sc in zip(items, scores):
            sc = [s if mk is None else jnp.where(mk, s, NEG_INF) for s, mk in zip(sc, masks)]
            mx = sink_col
            for s in sc:
                mx = jnp.maximum(mx, jnp.max(s, axis=-1, keepdims=True))
            es = [jnp.exp(s - mx) for s in sc]
            den = jnp.exp(sink_col - mx)
            for e in es:
                den = den + jnp.sum(e, axis=-1, keepdims=True)
            exps.append([e.astype(BF16) for e in es])
            dens.append(den)
        outs = []
        for (_, _, vals, _, _), es, den in zip(items, exps, dens):
            acc = None
            for e, vv in zip(es, vals):
                t = jnp.dot(e, vv.astype(BF16), preferred_element_type=F32)
                acc = t if acc is None else acc + t
            outs.append(acc / den)
        return outs

    def sink_column(hk, rows_per_head):
        row = lax.broadcasted_iota(jnp.int32, (grp * rows_per_head, 1), 0)
        col = jnp.full((grp * rows_per_head, 1), sink_ref[sink0 + hk * grp], F32)
        for g in range(1, grp):
            col = jnp.where(row >= g * rows_per_head, sink_ref[sink0 + hk * grp + g], col)
        return col

    kvs = [slice(hk * HEAD_DIM, (hk + 1) * HEAD_DIM) for hk in range(SWA_KV_HEADS)]

    def stack_q(hk, rows):
        return jnp.concatenate(
            [u_ref[0, rows, (hk * grp + g) * HEAD_DIM:(hk * grp + g + 1) * HEAD_DIM] for g in range(grp)],
            axis=0)

    def unstack_o(o2s, n):
        return jnp.concatenate([o2s[hk][g * n:(g + 1) * n, :]
                                for hk in range(SWA_KV_HEADS) for g in range(grp)], axis=1)

    o2s = attend([(stack_q(hk, slice(0, lc)), [k_s[0:lc, kvs[hk]]], [v_s[0:lc, kvs[hk]]], [None],
                   sink_column(hk, lc)) for hk in range(SWA_KV_HEADS)])
    gate = u_ref[0, 0:lc, gw + 2 * SWA_KV_W:2 * gw + 2 * SWA_KV_W]
    o_ref[0, 0:lc, :] = (unstack_o(o2s, lc) * gate).astype(BF16)

    qi = lax.broadcasted_iota(jnp.int32, (grp * blk, 3 * blk), 0) % blk
    kj = lax.broadcasted_iota(jnp.int32, (grp * blk, 3 * blk), 1)
    in_win = jnp.abs(kj - blk - qi) <= WINDOW

    assert nblk % SWA_BLOCKS_PER_STEP == 0

    def block_body(step, carry):
        items, qrows = [], []
        for s in range(SWA_BLOCKS_PER_STEP):
            n = step * SWA_BLOCKS_PER_STEP + s
            kpos = n * blk - blk + kj
            mask = in_win & (kpos >= 0) & (kpos < t_lat)
            qrow = pl.multiple_of(lc + n * blk, blk)
            wrow = qrow
            qrows.append(qrow)
            items += [(stack_q(hk, pl.ds(qrow, blk)),
                       [k_s[pl.ds(wrow, 3 * blk), kvs[hk]], k_s[0:lc, kvs[hk]]],
                       [v_s[pl.ds(wrow, 3 * blk), kvs[hk]], v_s[0:lc, kvs[hk]]],
                       [mask, None], sink_column(hk, blk)) for hk in range(SWA_KV_HEADS)]
        o2s = attend(items)
        for s, qrow in enumerate(qrows):
            gate = u_ref[0, pl.ds(qrow, blk), gw + 2 * SWA_KV_W:2 * gw + 2 * SWA_KV_W]
            o = unstack_o(o2s[s * SWA_KV_HEADS:(s + 1) * SWA_KV_HEADS], blk)
            o_ref[0, pl.ds(qrow, blk), :] = (o * gate).astype(BF16)
        return carry

    lax.fori_loop(0, nblk // SWA_BLOCKS_PER_STEP, block_body, 0)


def _swa(u, sink, l, *, lc):
    nb, total, _ = u.shape
    grid_spec = pltpu.PrefetchScalarGridSpec(
        num_scalar_prefetch=1,
        grid=(nb,),
        in_specs=[pl.BlockSpec((1, total, W_SWA), lambda b, s: (b, 0, 0))],
        out_specs=pl.BlockSpec((1, total, GROUP_W), lambda b, s: (b, 0, 0)),
        scratch_shapes=[pltpu.VMEM((total + 2 * WINDOW, SWA_KV_W), BF16),
                        pltpu.VMEM((total + 2 * WINDOW, SWA_KV_W), BF16)],
    )
    return pl.pallas_call(
        functools.partial(_swa_body, lc=lc, sink0=l * N_HEADS),
        grid_spec=grid_spec,
        out_shape=jax.ShapeDtypeStruct((nb, total, GROUP_W), BF16),
        compiler_params=_cparams(("arbitrary",)),
        name="swa",
    )(sink, u)


def _gdn_body(u_ref, ab_ref, avec_ref, dtb_ref, ng_ref, o_ref,
              q_s, k_s, v_s, cs_s, gt_s, of_s, ob_s, st_s, *ring_refs, lc):
    ring = (ring_refs[0:4], ring_refs[4:8])
    total = u_ref.shape[1]
    gw = GROUP_W
    c = GDN_CHUNK
    nch = total // c
    n_ctx = lc // c
    per_blk = ROW_BLOCK // c
    ones_bd = _head_ones()

    ri = lax.broadcasted_iota(jnp.int32, (ROW_BLOCK, ROW_BLOCK), 0)
    ci_ = lax.broadcasted_iota(jnp.int32, (ROW_BLOCK, ROW_BLOCK), 1)
    same_chunk = (ri // c) == (ci_ // c)
    tri_lo = (same_chunk & (ri >= ci_)).astype(BF16)
    tri_up = (same_chunk & (ri <= ci_)).astype(BF16)
    lane = lax.broadcasted_iota(jnp.int32, (ROW_BLOCK, W_AB), 1)

    def phase_a(bi, r0):
        h = u_ref[0, r0:r0 + ROW_BLOCK, 0:3 * gw]
        hq, hk, hv = h[:, 0:gw], h[:, gw:2 * gw], h[:, 2 * gw:3 * gw]
        ssq = _split_dot(hq * hq, ones_bd, 2)
        q_s[r0:r0 + ROW_BLOCK, :] = hq * lax.rsqrt(ssq + NORM_EPS) * (HEAD_DIM ** -0.5)
        ssk = _split_dot(hk * hk, ones_bd, 2)
        k_s[r0:r0 + ROW_BLOCK, :] = hk * lax.rsqrt(ssk + NORM_EPS)
        v_s[r0:r0 + ROW_BLOCK, :] = hv
        ab = ab_ref[0, r0:r0 + ROW_BLOCK, :]
        g = -avec_ref[...] * _softplus(ab + dtb_ref[...])
        g = jnp.where(lane < 2 * N_HEADS, g, 0.0)
        beta = _sigmoid(ab)
        cs_lo = _split_dot_left(tri_lo, g, 3)
        cs_up = _split_dot_left(tri_up, g, 3)
        cs = jnp.where(lane < N_HEADS, cs_lo, jnp.where(lane < 2 * N_HEADS, cs_up, beta))
        cs_s[r0:r0 + ROW_BLOCK, :] = cs
        cst = cs.T
        for cc in range(per_blk):
            row0 = (bi * per_blk + cc) * SUBLANES
            piece = cst[0:SUBLANES, cc * c:(cc + 1) * c]
            rows = [jnp.concatenate([piece[hd:hd + 1, :], piece[hd + 1:hd + 2, :]], axis=1)
                    for hd in range(0, 2 * N_HEADS, 2)]
            rows.append(jnp.zeros((SUBLANES - len(rows), PAIR_W), F32))
            gt_s[row0:row0 + SUBLANES, :] = jnp.concatenate(rows, axis=0)

    def phase_a_gen(block_ids):
        for bi in block_ids:
            phase_a(bi, bi * ROW_BLOCK)
            yield

    ii = lax.broadcasted_iota(jnp.int32, (c, PAIR_W), 0)
    jj = lax.broadcasted_iota(jnp.int32, (c, PAIR_W), 1) % c
    eye = (ii == jj).astype(F32)
    bd_r = lax.broadcasted_iota(jnp.int32, (PAIR_W, PAIR_W), 0) // HEAD_DIM
    bd_c = lax.broadcasted_iota(jnp.int32, (PAIR_W, PAIR_W), 1) // HEAD_DIM
    bd_pair = bd_r == bd_c

    def bdiag(x):
        return jnp.where(bd_pair, jnp.concatenate([x, x], axis=0), jnp.zeros((), x.dtype))

    n_sq = c.bit_length() - 2
    incl = ((ii >= jj), (ii <= jj))
    strict = ((ii > jj), (ii < jj))
    last = (c - 1, 0)
    heads = [slice(h * HEAD_DIM, (h + 1) * HEAD_DIM) for h in range(N_HEADS)]
    def expander(n_terms, lane0):
        src = lax.broadcasted_iota(jnp.int32, (n_terms * W_AB, gw), 0) % W_AB
        grp = lax.broadcasted_iota(jnp.int32, (n_terms * W_AB, gw), 1) // HEAD_DIM
        return (src == lane0 + grp).astype(BF16)

    expand_g = [expander(3, d * N_HEADS) for d in range(2)]
    expand_b = [expander(2, (2 + d) * N_HEADS) for d in range(2)]

    blk_rows = GDN_B1_CHUNKS * c
    n_steps = total // blk_rows
    assert lc % blk_rows == 0 and n_steps >= 2

    def block_row0(j, d):
        if d == 0:
            return j * blk_rows
        if isinstance(j, int):
            return lc - (j + 1) * blk_rows if (j + 1) * blk_rows <= lc else total + lc - (j + 1) * blk_rows
        return pl.multiple_of(jnp.where((j + 1) * blk_rows <= lc, lc - (j + 1) * blk_rows,
                                        total + lc - (j + 1) * blk_rows), blk_rows)

    def cat_heads(items):
        return jnp.concatenate(items, axis=1)

    def b1_gen(j, slot):
        uvr_s, wqr_s, attr_s, kttr_s = ring[slot]
        probs = []
        for d in range(2):
            r0 = block_row0(j, d)
            parts = []
            rem = cs_s[pl.ds(r0, blk_rows), :]
            for _ in range(3):
                hi = rem.astype(BF16)
                parts.append(hi)
                rem = rem - hi.astype(F32)
            xp_g = jnp.dot(jnp.concatenate(parts, axis=1), expand_g[d], preferred_element_type=F32)
            xp_b = jnp.dot(jnp.concatenate(parts[0:2], axis=1), expand_b[d], preferred_element_type=F32)
            for cc in range(GDN_B1_CHUNKS):
                r = r0 + cc * c
                kc = k_s[pl.ds(r, c), :]
                qc = q_s[pl.ds(r, c), :]
                vc = v_s[pl.ds(r, c), :]
                gt = gt_s[pl.ds(pl.multiple_of((r0 // c + cc) * SUBLANES, SUBLANES), SUBLANES), :]
                sel = []
                for p2 in range(N_HEADS // 2):
                    ps = slice(p2 * PAIR_W, (p2 + 1) * PAIR_W)
                    kp, qp, vp = kc[:, ps], qc[:, ps], vc[:, ps]
                    kbd = bdiag(kp.astype(BF16))
                    kq = lax.dot_general(jnp.concatenate([kp, qp], axis=0).astype(BF16), kbd,
                                         (((1,), (1,)), ((), ())), preferred_element_type=F32)
                    kk, qk = kq[0:c, :], kq[c:2 * c, :]
                    col = xp_g[cc * c:(cc + 1) * c, ps]
                    bcol = xp_b[cc * c:(cc + 1) * c, ps]
                    rowv = gt[d * (N_HEADS // 2) + p2:d * (N_HEADS // 2) + p2 + 1, :]
                    decay = jnp.where(incl[d], jnp.exp(jnp.where(incl[d], col - rowv, 0.0)), 0.0)
                    n = -jnp.where(strict[d], kk * decay, 0.0) * bcol
                    eg = jnp.exp(col)
                    glast = col[last[d]:last[d] + 1, :]
                    rhs = jnp.concatenate([bdiag((vp * bcol).astype(BF16)),
                                           bdiag((kp * (bcol * eg)).astype(BF16))], axis=1)
                    pr = dict(cc=cc, d=d, a=(eye - n).astype(BF16), p=eye + n, rhs=rhs,
                              att=qk * decay, qe=qp * eg, kt=kp * jnp.exp(glast - col))
                    probs.append(pr)
                    sel.append(pr)
                wqr_s[d,(2 * cc + 1) * c:(2 * cc + 2) * c, :] = (
                    cat_heads([pr['qe'] for pr in sel]).astype(BF16))
                attr_s[d,cc * c:(cc + 1) * c, :] = cat_heads([pr['att'] for pr in sel]).astype(BF16)
                ktt = cat_heads([pr['kt'] for pr in sel]).T
                kttr_s[d,cc * gw:(cc + 1) * gw, :] = ktt.astype(BF16)
        yield
        for _ in range(n_sq):
            res = [eye - jnp.dot(pr['a'], bdiag(pr['p'].astype(BF16)), preferred_element_type=F32)
                   for pr in probs]
            yield
            for pr, rr in zip(probs, res):
                pr['p'] = pr['p'] + jnp.dot(pr['p'].astype(BF16), bdiag(rr.astype(BF16)),
                                            preferred_element_type=F32)
            yield
        for pr in probs:
            pr['uw'] = jnp.dot(pr['p'].astype(BF16), pr['rhs'], preferred_element_type=F32)
        yield
        for d in range(2):
            for cc in range(GDN_B1_CHUNKS):
                sel = [pr for pr in probs if pr['cc'] == cc and pr['d'] == d]
                uvr_s[d,cc * c:(cc + 1) * c, :] = cat_heads([pr['uw'][:, 0:PAIR_W] for pr in sel])
                wqr_s[d,2 * cc * c:(2 * cc + 1) * c, :] = (
                    cat_heads([pr['uw'][:, PAIR_W:2 * PAIR_W] for pr in sel]).astype(BF16))

    def b2_gen(j, slot):
        uvr_s, wqr_s, attr_s, kttr_s = ring[slot]
        r0s = [block_row0(j, d) for d in range(2)]
        n_pairs = N_HEADS // 2
        states = [st_s[i] for i in range(2 * n_pairs)]
        first_head = lax.broadcasted_iota(jnp.int32, (1, PAIR_W), 1) < HEAD_DIM
        for step in range(GDN_B1_CHUNKS):
            probs = []
            for d in range(2):
                cc = step if d == 0 else GDN_B1_CHUNKS - 1 - step
                r = r0s[d] + cc * c
                wqm = wqr_s[d, 2 * cc * c:(2 * cc + 2) * c, :]
                uvm = uvr_s[d, cc * c:(cc + 1) * c, :]
                attm = attr_s[d, cc * c:(cc + 1) * c, :]
                grow = cs_s[pl.ds(r + last[d], 1), :]
                for p2 in range(n_pairs):
                    ps = slice(p2 * PAIR_W, (p2 + 1) * PAIR_W)
                    hd = d * N_HEADS + 2 * p2
                    kt = kttr_s[d, cc * gw + p2 * PAIR_W:cc * gw + (p2 + 1) * PAIR_W, :]
                    gl = jnp.exp(jnp.where(first_head, grow[:, hd:hd + 1], grow[:, hd + 1:hd + 2]))
                    probs.append(dict(d=d, si=d * n_pairs + p2, r=r, wqm=wqm[:, ps], uv=uvm[:, ps],
                                      att=attm[:, ps], kt=kt, gl=gl))
            for pr in probs:
                pr['wq'] = jnp.dot(pr['wqm'], bdiag(states[pr['si']].astype(BF16)),
                                   preferred_element_type=F32)
            yield
            for pr in probs:
                vn = (pr['uv'] - pr['wq'][0:c, :]).astype(BF16)
                pr['o'] = pr['wq'][c:2 * c, :] + jnp.dot(pr['att'], bdiag(vn), preferred_element_type=F32)
                upd = jnp.dot(pr['kt'], vn, preferred_element_type=F32)
                upd = jnp.where(first_head, upd[0:HEAD_DIM, :], upd[HEAD_DIM:PAIR_W, :])
                states[pr['si']] = states[pr['si']] * pr['gl'] + upd
            for d, o_s in ((0, of_s), (1, ob_s)):
                sel = [pr for pr in probs if pr['d'] == d]
                o_s[pl.ds(sel[0]['r'], c), :] = cat_heads([pr['o'] for pr in sel])
            yield
        for i in range(2 * n_pairs):
            st_s[i] = states[i]

    def run_interleaved(gens):
        live = list(gens)
        while live:
            still = []
            for g in live:
                try:
                    next(g)
                    still.append(g)
                except StopIteration:
                    pass
            live = still

    assert blk_rows == ROW_BLOCK
    n_blocks = total // ROW_BLOCK
    n_pre = lc // ROW_BLOCK
    for bi in range(n_pre):
        phase_a(bi, bi * ROW_BLOCK)
    st_s[...] = jnp.zeros(st_s.shape, F32)
    run_interleaved([b1_gen(0, 0), phase_a_gen(range(n_pre, n_blocks))])

    def pipe_body(t, carry):
        j = 1 + 2 * t
        run_interleaved([b1_gen(j, 1), b2_gen(j - 1, 0)])
        run_interleaved([b1_gen(j + 1, 0), b2_gen(j, 1)])
        return carry

    lax.fori_loop(0, (n_steps - 1) // 2, pipe_body, 0)
    if (n_steps - 1) % 2:
        run_interleaved([b1_gen(n_steps - 1, (n_steps - 1) % 2), b2_gen(n_steps - 2, n_steps % 2)])
    def finish(r0):
        o = of_s[r0:r0 + ROW_BLOCK, :] + ob_s[r0:r0 + ROW_BLOCK, :]
        ms = _split_dot(o * o, ones_bd, 2) * (1.0 / HEAD_DIM)
        gate = u_ref[0, r0:r0 + ROW_BLOCK, 3 * gw:4 * gw]
        y = o * lax.rsqrt(ms + NORM_EPS) * ng_ref[...] * gate
        o_ref[0, r0:r0 + ROW_BLOCK, :] = y.astype(BF16)

    def finish_gen(rows):
        for r0 in rows:
            finish(r0)
            yield

    last_rows = [block_row0(n_steps - 1, d) for d in range(2)]
    early = [bi * ROW_BLOCK for bi in range(n_blocks) if bi * ROW_BLOCK not in last_rows]
    run_interleaved([b2_gen(n_steps - 1, (n_steps - 1) % 2), finish_gen(early)])
    for r0 in sorted(set(last_rows)):
        finish(r0)


def _gdn_params(a_log, dt_bias, norm_g):
    depth = a_log.shape[0]
    pad = ((0, 0), (0, 0), (0, W_AB - 2 * N_HEADS))
    avec = jnp.pad(jnp.exp(a_log.reshape(depth, 1, 2 * N_HEADS)), pad)
    dtb = jnp.pad(dt_bias.reshape(depth, 1, 2 * N_HEADS), pad)
    ng = jnp.tile(norm_g.reshape(depth, 1, HEAD_DIM), (1, 1, N_HEADS))
    return avec, dtb, ng


def _gdn(u, ab, avec, dtb, ng, l, *, lc):
    nb, total, _ = u.shape
    nch = total // GDN_CHUNK
    blk = GDN_B1_CHUNKS * GDN_CHUNK
    return pl.pallas_call(
        functools.partial(_gdn_body, lc=lc),
        grid=(nb,),
        in_specs=[pl.BlockSpec((1, total, W_GDN), lambda b: (b, 0, 0)),
                  pl.BlockSpec((1, total, W_AB), lambda b: (b, 0, 0)),
                  _layer(avec, l), _layer(dtb, l), _layer(ng, l)],
        out_specs=pl.BlockSpec((1, total, GROUP_W), lambda b: (b, 0, 0)),
        out_shape=jax.ShapeDtypeStruct((nb, total, GROUP_W), BF16),
        scratch_shapes=[pltpu.VMEM((total, GROUP_W), F32),
                        pltpu.VMEM((total, GROUP_W), F32),
                        pltpu.VMEM((total, GROUP_W), F32),
                        pltpu.VMEM((total, W_AB), F32),
                        pltpu.VMEM((nch * SUBLANES, PAIR_W), F32),
                        pltpu.VMEM((total, GROUP_W), F32),
                        pltpu.VMEM((total, GROUP_W), F32),
                        pltpu.VMEM((N_HEADS, HEAD_DIM, PAIR_W), F32)] + 2 * [
                        pltpu.VMEM((2, blk, GROUP_W), F32),
                        pltpu.VMEM((2, 2 * blk, GROUP_W), BF16),
                        pltpu.VMEM((2, blk, GROUP_W), BF16),
                        pltpu.VMEM((2, GDN_B1_CHUNKS * GROUP_W, GDN_CHUNK), BF16)],
        compiler_params=_cparams(("arbitrary",)),
        name="gdn",
    )(u, ab, avec, dtb, ng)


def _pack_w_in(w_in):
    gw = GROUP_W
    o_gdn = 2 * gw
    o_ab = o_gdn + 4 * gw
    o_ret = o_ab + 4 * N_HEADS
    o_swa = o_ret + 4 * gw
    end = o_swa + W_SWA
    assert end == w_in.shape[-1]
    wb = w_in.astype(BF16)
    pad = jnp.zeros(wb.shape[:-1] + (W_AB - 4 * N_HEADS,), BF16)
    return jnp.concatenate([wb[..., 0:o_ab], wb[..., o_ret:end], wb[..., o_ab:o_ret], pad], axis=-1)


def _rope_tables(ang, lc):
    cos = jnp.cos(ang)
    sin = jnp.sin(ang)
    cos_h = jnp.concatenate([cos, cos], axis=-1)
    sin_h = jnp.concatenate([-sin, sin], axis=-1)
    cos_t = jnp.tile(cos_h, (1, N_HEADS))
    sin_t = jnp.tile(sin_h, (1, N_HEADS))
    cos_t = jnp.concatenate([jnp.ones((lc, GROUP_W), F32), cos_t], axis=0)
    sin_t = jnp.concatenate([jnp.zeros((lc, GROUP_W), F32), sin_t], axis=0)
    return cos_t, sin_t


def _rope_freqs(pos, n):
    inv = ROPE_BASE ** (-jnp.arange(0, n, 2, dtype=F32) / n)
    return pos[:, None] * inv[None, :]


def kernel(x, c, ctx, c_ctx, w_mod, b_mod, pre_norm_g, post_norm_g, w_in, w_out, lru_conv_w, lru_conv_b, lru_w_r, lru_b_r, lru_w_i, lru_b_i, lru_lambda, gdn_conv_w, gdn_a_log, gdn_dt_bias, gdn_norm_g, ret_decay_logit, swa_sink):
    nb, t, d = x.shape
    lc = ctx.shape[1]
    depth = w_mod.shape[0]
    assert t % ROW_BLOCK == 0 and lc % ROW_BLOCK == 0 and d == 4 * GROUP_W

    rows = t // GRID_W
    row = jnp.repeat(jnp.arange(rows, dtype=F32), GRID_W)
    col = jnp.tile(jnp.arange(GRID_W, dtype=F32), rows)
    ang2d = jnp.concatenate([_rope_freqs(row, HEAD_DIM // 2), _rope_freqs(col, HEAD_DIM // 2)], axis=-1)
    ang1d = _rope_freqs(jnp.arange(t, dtype=F32), HEAD_DIM)
    cos1, sin1 = _rope_tables(ang1d, lc)
    cos2, sin2 = _rope_tables(ang2d, lc)

    mod_rows = -(-(nb + 1) // SUBLANES) * SUBLANES
    s_rows = jnp.concatenate([c, c_ctx[None, :], jnp.zeros((mod_rows - nb - 1, d), F32)], axis=0)
    mod = _modulation(s_rows, w_mod, b_mod)

    w_in_p = _pack_w_in(w_in)
    w_out_b = w_out.astype(BF16)
    pre_g = pre_norm_g.reshape(depth, 1, d)
    post_g = post_norm_g.reshape(depth, 1, d)
    lru_cb, lru_wg, lru_bg, lru_lam = _lru_params(lru_conv_b, lru_w_r, lru_b_r, lru_w_i, lru_b_i, lru_lambda)
    gdn_avec, gdn_dtb, gdn_ng = _gdn_params(gdn_a_log, gdn_dt_bias, gdn_norm_g)
    ret_lg = jnp.repeat(ret_decay_logit, HEAD_DIM, axis=-1)
    sink = swa_sink.reshape(depth * N_HEADS)
    xs = jnp.concatenate([ctx, x], axis=1)
    for l in range(depth):
        u_lru, u_gdn, u_ret, u_swa, u_ab = _in_proj(xs, mod, pre_g, w_in_p, (cos1, sin1, cos2, sin2),
                                                    (lru_conv_w, lru_cb, gdn_conv_w), l, lc=lc)
        ya = _lru(u_lru, lru_wg, lru_bg, lru_lam, l, lc=lc)
        yb = _gdn(u_gdn, u_ab, gdn_avec, gdn_dtb, gdn_ng, l, lc=lc)
        yc = _ret(u_ret, ret_lg, l, lc=lc)
        yd = _swa(u_swa, sink, l, lc=lc)
        xs = _out_proj((ya, yb, yc, yd), xs, mod, post_g, w_out_b, l, lc=lc, latent_only=l == depth - 1)
    return xs
```

```python
import functools
import math

import jax
import jax.numpy as jnp
from jax import lax
from jax.experimental import pallas as pl
from jax.experimental.pallas import tpu as pltpu

F32 = jnp.float32
BF16 = jnp.bfloat16

HEAD_DIM = 64
GROUP_W = 256
N_HEADS = GROUP_W // HEAD_DIM
NORM_EPS = 1e-6
ROPE_BASE = 10000.0
NEG_INF = -1e30
GRID_W = 64
CONV_W = 4
LRU_C = 8.0
LRU_SCAN_UNROLL = 4
GDN_CHUNK = 64
PAIR_W = 2 * HEAD_DIM
GDN_B1_CHUNKS = 4
RET_CHUNK = 256
RET_CHUNKS_PER_STEP = 5
SWA_KV_HEADS = 2
SWA_KV_W = SWA_KV_HEADS * HEAD_DIM
WINDOW = 128
SWA_BLOCKS_PER_STEP = 2
ROW_BLOCK = 256
CONV_ROWS = 64
PROJ_ROWS_MAX = 576
SUBLANES = 8
VMEM_LIMIT_BYTES = 56 * 1024 * 1024

W_LRU = 2 * GROUP_W
W_GDN = 4 * GROUP_W
W_RET = 4 * GROUP_W
W_SWA = 2 * GROUP_W + 2 * SWA_KV_W
W_AB = 128
IN_W_PACKED = W_LRU + W_GDN + W_RET + W_SWA + W_AB


def _cparams(sem, flags=None):
    return pltpu.CompilerParams(dimension_semantics=sem, vmem_limit_bytes=VMEM_LIMIT_BYTES, flags=flags)


def _dot(a, b):
    return jnp.dot(a.astype(BF16), b.astype(BF16), preferred_element_type=F32)


def _dot_nt(a, b):
    return lax.dot_general(a.astype(BF16), b.astype(BF16), (((1,), (1,)), ((), ())),
                           preferred_element_type=F32)


def _dot_tn(a, b):
    return lax.dot_general(a.astype(BF16), b.astype(BF16), (((0,), (0,)), ((), ())),
                           preferred_element_type=F32)


def _split_dot(x, w, parts):
    acc = None
    r = x
    for _ in range(parts):
        hi = r.astype(BF16)
        t = jnp.dot(hi, w, preferred_element_type=F32)
        acc = t if acc is None else acc + t
        r = r - hi.astype(F32)
    return acc


def _split_dot_left(w, x, parts):
    acc = None
    r = x
    for _ in range(parts):
        hi = r.astype(BF16)
        t = jnp.dot(w, hi, preferred_element_type=F32)
        acc = t if acc is None else acc + t
        r = r - hi.astype(F32)
    return acc


def _sigmoid(x):
    return 0.5 * jnp.tanh(0.5 * x) + 0.5


def _silu(x):
    return x * _sigmoid(x)


def _softplus(x):
    return jnp.maximum(x, 0.0) + jnp.log1p(jnp.exp(-jnp.abs(x)))


def _head_ones():
    r = lax.broadcasted_iota(jnp.int32, (GROUP_W, GROUP_W), 0) // HEAD_DIM
    c = lax.broadcasted_iota(jnp.int32, (GROUP_W, GROUP_W), 1) // HEAD_DIM
    return (r == c).astype(BF16)


def _row_blocks(lc, total):
    out = []
    for r0 in range(0, total, ROW_BLOCK):
        out.append((r0, 0, lc) if r0 < lc else (r0, lc, total))
    return out


def _mod_body(s_ref, w_ref, b_ref, o_ref):
    s = _silu(s_ref[...])
    o_ref[0] = _dot(s, w_ref[0]) + b_ref[0]


def _modulation(s_rows, w_mod, b_mod):
    depth, d, d3 = w_mod.shape
    rows = s_rows.shape[0]
    nt = d3 // d
    return pl.pallas_call(
        _mod_body,
        grid=(depth, nt),
        in_specs=[pl.BlockSpec((rows, d), lambda l, j: (0, 0)),
                  pl.BlockSpec((1, d, d), lambda l, j: (l, 0, j)),
                  pl.BlockSpec((1, 1, d), lambda l, j: (l, 0, j))],
        out_specs=pl.BlockSpec((1, rows, d), lambda l, j: (l, 0, j)),
        out_shape=jax.ShapeDtypeStruct((depth, rows, d3), F32),
        compiler_params=_cparams(("arbitrary", "arbitrary")),
        name="modulation",
    )(s_rows, w_mod, b_mod.reshape(depth, 1, d3))


def _proj_rows(total):
    for tm in range(PROJ_ROWS_MAX, SUBLANES - 1, -SUBLANES):
        if total % tm == 0:
            return tm
    raise ValueError(total)


def _mod_rows(mod_ref, b, i, tm, lc, nb, c0, c1):
    lat = mod_ref[pl.ds(b, 1), c0:c1]
    if lc % tm == 0:
        ctx = mod_ref[nb:nb + 1, c0:c1]
        return jnp.where(i * tm < lc, ctx, lat)
    ctx = mod_ref[nb:nb + 1, c0:c1]
    row = i * tm + lax.broadcasted_iota(jnp.int32, (tm, 1), 0)
    return jnp.where(row < lc, ctx, lat)


def _inproj_body(x_ref, xp_ref, xn_ref, mod_ref, g_ref, w_ref, cos1_ref, sin1_ref, cos2_ref, sin2_ref,
                 lcw_ref, lcb_ref, gcw_ref, o_lru, o_gdn, o_ret, o_swa, o_ab, *conv_scr, lc, nb, total):
    i = pl.program_id(0)
    b = pl.program_id(1)
    d = x_ref.shape[2]
    tm = x_ref.shape[1]
    gw = GROUP_W
    scale = HEAD_DIM ** -0.5

    def prenorm(xv, m):
        ms = jnp.mean(xv * xv, axis=-1, keepdims=True)
        y = xv * lax.rsqrt(ms + NORM_EPS) * g_ref[...]
        return (y * (1.0 + m[:, d:2 * d]) + m[:, 0:d]).astype(BF16)

    h = prenorm(x_ref[0], _mod_rows(mod_ref, b, i, tm, lc, nb, 0, 2 * d))
    halo = prenorm(jnp.concatenate([xp_ref[0], xn_ref[0]], axis=0), mod_ref[pl.ds(b, 1), 0:2 * d])
    h_scr = conv_scr[4]
    h_scr[0:tm, :] = h
    h_scr[tm:tm + 2 * SUBLANES, :] = halo
    h_all = slice(0, tm + 2 * SUBLANES)

    def proj(c0, width, rows=slice(0, tm)):
        return jnp.dot(h_scr[rows, :], w_ref[:, c0:c0 + width], preferred_element_type=F32)

    nt = total // tm
    b_tile, b_row = lc // tm, lc % tm
    assert b_row % SUBLANES == 0 and (b_row == 0 or SUBLANES <= b_row <= tm - SUBLANES)
    prev_ok = i > 0
    next_ok = i < nt - 1
    if b_row == 0:
        prev_ok = prev_ok & (i != b_tile)
        next_ok = next_ok & (i != b_tile - 1)
    prev_f = jnp.where(prev_ok, 1.0, 0.0)
    next_f = jnp.where(next_ok, 1.0, 0.0)

    def dwconv_to(o_ref, oc0, u_all, scr, cw_ref, c0, post):
        scr[0:SUBLANES, :] = u_all[tm:tm + SUBLANES] * prev_f
        scr[SUBLANES:SUBLANES + tm, :] = u_all[0:tm]
        scr[SUBLANES + tm:2 * SUBLANES + tm, :] = u_all[tm + SUBLANES:tm + 2 * SUBLANES] * next_f
        for r0 in range(0, tm, CONV_ROWS):
            n = min(CONV_ROWS, tm - r0)
            near_boundary = b_row and r0 - 1 <= b_row <= r0 + n + 1
            acc = None
            for k in range(CONV_W):
                off = k - 2
                tap = scr[SUBLANES + r0 + off:SUBLANES + r0 + off + n, :]
                if near_boundary and off != 0:
                    r = r0 + off + lax.broadcasted_iota(jnp.int32, (n, 1), 0)
                    keep = ((r >= b_row) == (r - off >= b_row)) | (i != b_tile)
                    tap = jnp.where(keep, tap, 0.0)
                term = tap * cw_ref[k:k + 1, c0:c0 + gw]
                acc = term if acc is None else acc + term
            o_ref[0, r0:r0 + n, oc0:oc0 + gw] = post(acc)

    c_gdn, c_ret, c_swa, c_ab = W_LRU, W_LRU + W_GDN, W_LRU + W_GDN + W_RET, W_LRU + W_GDN + W_RET + W_SWA
    kvw = SWA_KV_W

    def ep_lru_x(u):
        dwconv_to(o_lru, 0, u, conv_scr[0], lcw_ref, 0, lambda a: a + lcb_ref[...])

    def ep_gdn_conv(p):
        return lambda u: dwconv_to(o_gdn, p * gw, u, conv_scr[1 + p], gcw_ref, p * gw, _silu)

    def ep_gates(u):
        o_lru[0, :, gw:2 * gw] = _silu(u[:, 0:gw])
        o_gdn[0, :, 3 * gw:4 * gw] = _silu(u[:, gw:2 * gw])

    def ep_ret(u):
        cos, sin = cos1_ref[...], sin1_ref[...]
        o_ret[0, :, 0:gw] = _rope(u[:, 0:gw], cos, sin)
        o_ret[0, :, gw:2 * gw] = _rope(u[:, gw:2 * gw], cos, sin) * scale
        o_ret[0, :, 2 * gw:3 * gw] = u[:, 2 * gw:3 * gw]
        o_ret[0, :, 3 * gw:4 * gw] = _silu(u[:, 3 * gw:4 * gw])

    def ep_swa(u):
        cos, sin = cos2_ref[...], sin2_ref[...]
        o_swa[0, :, 0:gw] = _rope(u[:, 0:gw], cos, sin) * scale
        o_swa[0, :, gw:gw + kvw] = _rope(u[:, gw:gw + kvw], cos[:, 0:kvw], sin[:, 0:kvw])
        o_swa[0, :, gw + kvw:gw + 2 * kvw] = u[:, gw + kvw:gw + 2 * kvw]
        o_swa[0, :, gw + 2 * kvw:2 * gw + 2 * kvw] = _silu(u[:, gw + 2 * kvw:2 * gw + 2 * kvw])

    def ep_ab(u):
        o_ab[0] = u

    def gates_proj():
        return jnp.concatenate([proj(gw, gw), proj(c_gdn + 3 * gw, gw)], axis=1)

    work = [(lambda: proj(0, gw, h_all), ep_lru_x),
            (lambda: proj(c_ret, W_RET), ep_ret),
            (lambda: proj(c_gdn, gw, h_all), ep_gdn_conv(0)),
            (lambda: proj(c_swa, W_SWA), ep_swa),
            (lambda: proj(c_gdn + gw, gw, h_all), ep_gdn_conv(1)),
            (gates_proj, ep_gates),
            (lambda: proj(c_gdn + 2 * gw, gw, h_all), ep_gdn_conv(2)),
            (lambda: proj(c_ab, W_AB), ep_ab)]
    pending = work[0][0]()
    for k, (_, epilogue) in enumerate(work):
        cur = pending
        if k + 1 < len(work):
            pending = work[k + 1][0]()
        epilogue(cur)


def _in_proj(x, mod, g, w, tables, conv_params, l, *, lc):
    nb, total, d = x.shape
    tm = _proj_rows(total)
    nt = total // tm
    per = tm // SUBLANES
    last = total // SUBLANES - 1
    widths = (W_LRU, W_GDN, W_RET, W_SWA, W_AB)
    tspec = pl.BlockSpec((tm, GROUP_W), lambda i, b: (i, 0))
    return pl.pallas_call(
        functools.partial(_inproj_body, lc=lc, nb=nb, total=total),
        grid=(nt, nb),
        in_specs=[pl.BlockSpec((1, tm, d), lambda i, b: (b, i, 0)),
                  pl.BlockSpec((1, SUBLANES, d), lambda i, b: (b, jnp.maximum(i * per - 1, 0), 0)),
                  pl.BlockSpec((1, SUBLANES, d), lambda i, b: (b, jnp.minimum((i + 1) * per, last), 0)),
                  _layer(mod, l), _layer(g, l), _layer(w, l), tspec, tspec, tspec, tspec]
                 + [_layer(p, l) for p in conv_params],
        out_specs=[pl.BlockSpec((1, tm, wd), lambda i, b: (b, i, 0)) for wd in widths],
        out_shape=[jax.ShapeDtypeStruct((nb, total, wd), F32) for wd in widths],
        scratch_shapes=[pltpu.VMEM((tm + 2 * SUBLANES, GROUP_W), F32) for _ in range(4)]
                       + [pltpu.VMEM((tm + 2 * SUBLANES, d), BF16)],
        compiler_params=_cparams(("arbitrary", "arbitrary")),
        name="in_proj",
    )(x, x, x, mod, g, w, *tables, *conv_params)


def _outproj_body(ya, yb, yc, yd, x_ref, mod_ref, g_ref, w_ref, o_ref, *, lc, nb, tile0):
    b = pl.program_id(0)
    i = pl.program_id(1) + tile0
    d = x_ref.shape[2]
    acc = None
    for k, y_ref in enumerate((ya, yb, yc, yd)):
        t = jnp.dot(y_ref[0], w_ref[k * GROUP_W:(k + 1) * GROUP_W, :], preferred_element_type=F32)
        acc = t if acc is None else acc + t
    ms = jnp.mean(acc * acc, axis=-1, keepdims=True)
    yn = acc * lax.rsqrt(ms + NORM_EPS) * g_ref[...]
    gate = _mod_rows(mod_ref, b, i, acc.shape[0], lc, nb, 2 * d, 3 * d)
    o_ref[0] = x_ref[0] + gate * yn


def _out_proj(ys, x, mod, g, w, l, *, lc, latent_only=False):
    nb, total, d = x.shape
    if latent_only:
        tm = _proj_rows(math.gcd(lc, total - lc))
        tile0, rows_out = lc // tm, total - lc
    else:
        tm = _proj_rows(total)
        tile0, rows_out = 0, total
    yspec = pl.BlockSpec((1, tm, GROUP_W), lambda b, i: (b, i + tile0, 0))
    return pl.pallas_call(
        functools.partial(_outproj_body, lc=lc, nb=nb, tile0=tile0),
        grid=(nb, rows_out // tm),
        in_specs=[yspec, yspec, yspec, yspec,
                  pl.BlockSpec((1, tm, d), lambda b, i: (b, i + tile0, 0)),
                  _layer(mod, l), _layer(g, l), _layer(w, l)],
        out_specs=pl.BlockSpec((1, tm, d), lambda b, i: (b, i, 0)),
        out_shape=jax.ShapeDtypeStruct((nb, rows_out, d), F32),
        compiler_params=_cparams(("arbitrary", "arbitrary")),
        name="out_proj",
    )(*ys, x, mod, g, w)


def _lru_scan(a_s, b_s, h_s, tile_lo, n_tiles, carry, *, rev, accumulate):
    row = lax.broadcasted_iota(jnp.int32, (SUBLANES, GROUP_W), 0)

    def body(j, carry):
        t = tile_lo + (n_tiles - 1 - j if rev else j)
        r = pl.multiple_of(t * SUBLANES, SUBLANES)
        a = a_s[pl.ds(r, SUBLANES), :]
        b = b_s[pl.ds(r, SUBLANES), :]
        for s in (1, 2, 4):
            if rev:
                ra = pltpu.roll(a, SUBLANES - s, 0)
                rb = pltpu.roll(b, SUBLANES - s, 0)
                m = row < SUBLANES - s
            else:
                ra = pltpu.roll(a, s, 0)
                rb = pltpu.roll(b, s, 0)
                m = row >= s
            b = a * jnp.where(m, rb, 0.0) + b
            a = a * jnp.where(m, ra, 1.0)
        h = a * carry + b
        if accumulate:
            h_s[pl.ds(r, SUBLANES), :] = h_s[pl.ds(r, SUBLANES), :] + h
        else:
            h_s[pl.ds(r, SUBLANES), :] = h
        return h[0:1, :] if rev else h[SUBLANES - 1:SUBLANES, :]

    return lax.fori_loop(0, n_tiles, body, carry, unroll=LRU_SCAN_UNROLL)


def _lru_body(u_ref, wg_ref, bg_ref, lam_ref, o_ref, a_s, b_s, h_s, *, lc):
    total = u_ref.shape[1]
    blocks = _row_blocks(lc, total)
    zero = jnp.zeros((1, GROUP_W), F32)
    for d in range(2):
        sp = _softplus(-lam_ref[d])
        for r0, _, _ in blocks:
            uc = u_ref[0, r0:r0 + ROW_BLOCK, 0:GROUP_W]
            gts = _dot(uc, wg_ref[d]) + bg_ref[d]
            r = _sigmoid(gts[:, 0:GROUP_W])
            ig = _sigmoid(gts[:, GROUP_W:2 * GROUP_W])
            a = jnp.exp(-LRU_C * r * sp)
            a_s[r0:r0 + ROW_BLOCK, :] = a
            om = 1.0 - a * a
            b_s[r0:r0 + ROW_BLOCK, :] = jnp.where(om > 0.0, om * lax.rsqrt(om), 0.0) * (ig * uc)
        ct, tt = lc // SUBLANES, total // SUBLANES
        if d == 0:
            _lru_scan(a_s, b_s, h_s, 0, tt, zero, rev=False, accumulate=False)
        else:
            carry = _lru_scan(a_s, b_s, h_s, 0, ct, zero, rev=True, accumulate=True)
            _lru_scan(a_s, b_s, h_s, ct, tt - ct, carry, rev=True, accumulate=True)
    for r0, _, _ in blocks:
        gate = u_ref[0, r0:r0 + ROW_BLOCK, GROUP_W:2 * GROUP_W]
        o_ref[0, r0:r0 + ROW_BLOCK, :] = (h_s[r0:r0 + ROW_BLOCK, :] * gate).astype(BF16)


def _block_diag(w):
    n, c = w.shape[-3], w.shape[-1]
    eye = jnp.eye(n, dtype=w.dtype)
    return (eye[:, None, :, None] * w[..., :, :, None, :]).reshape(w.shape[:-3] + (n * c, n * c))


def _layer(arr, l):
    shape = arr.shape[1:]
    return pl.BlockSpec((None,) + shape, lambda *_: (l,) + (0,) * len(shape))


def _lru_params(conv_b, w_r, b_r, w_i, b_i, lam):
    depth = conv_b.shape[0]
    wg = jnp.concatenate([_block_diag(w_r), _block_diag(w_i)], axis=-1).astype(BF16)
    bg = jnp.concatenate([b_r, b_i], axis=-1).reshape(depth, 2, 1, 2 * GROUP_W)
    return conv_b.reshape(depth, 1, GROUP_W), wg, bg, lam.reshape(depth, 2, 1, GROUP_W)


def _lru(u, wg, bg, lam, l, *, lc):
    nb, total, _ = u.shape
    return pl.pallas_call(
        functools.partial(_lru_body, lc=lc),
        grid=(nb,),
        in_specs=[pl.BlockSpec((1, total, W_LRU), lambda b: (b, 0, 0)),
                  _layer(wg, l), _layer(bg, l), _layer(lam, l)],
        out_specs=pl.BlockSpec((1, total, GROUP_W), lambda b: (b, 0, 0)),
        out_shape=jax.ShapeDtypeStruct((nb, total, GROUP_W), BF16),
        scratch_shapes=[pltpu.VMEM((total, GROUP_W), F32) for _ in range(3)],
        compiler_params=_cparams(("arbitrary",)),
        name="lru",
    )(u, wg, bg, lam)


def _rope(x, cos, sin_signed):
    half = HEAD_DIM // 2
    outs = []
    for c0 in range(0, x.shape[1], 128):
        xs = x[:, c0:c0 + 128]
        lane = lax.broadcasted_iota(jnp.int32, xs.shape, 1)
        swapped = jnp.where((lane % HEAD_DIM) < half,
                            pltpu.roll(xs, 128 - half, 1), pltpu.roll(xs, half, 1))
        outs.append(xs * cos[:, c0:c0 + 128] + swapped * sin_signed[:, c0:c0 + 128])
    return outs[0] if len(outs) == 1 else jnp.concatenate(outs, axis=1)


def _ret_body(u_ref, lg_ref, o_ref, ds_s, st_s, m_s, *, lc):
    total = u_ref.shape[1]
    c = RET_CHUNK
    nch = total // c
    n_ctx = lc // c
    gw = GROUP_W
    ones_bd = _head_ones()
    bd_mask = ones_bd > 0
    lg = -_softplus(-lg_ref[...])
    lgf, lgb = lg[0:1, :], lg[1:2, :]
    pos = lax.broadcasted_iota(jnp.int32, (c, 1), 0).astype(F32)
    qdec_f = jnp.exp((pos + 1.0) * lgf)
    qdec_b = jnp.exp((c - pos) * lgb)
    kdec_f = jnp.exp((c - 1.0 - pos) * lgf)
    kdec_b = jnp.exp(pos * lgb)
    cdec_f = jnp.exp(float(c) * lgf)
    cdec_b = jnp.exp(float(c) * lgb)
    dij = (lax.broadcasted_iota(jnp.int32, (c, c), 0)
           - lax.broadcasted_iota(jnp.int32, (c, c), 1)).astype(F32)
    for h in range(N_HEADS):
        lf = lgf[:, h * HEAD_DIM:h * HEAD_DIM + 1]
        lb = lgb[:, h * HEAD_DIM:h * HEAD_DIM + 1]
        fwd = jnp.exp(jnp.maximum(dij, 0.0) * lf)
        bwd = jnp.exp(jnp.maximum(-dij, 0.0) * lb)
        m_s[h] = jnp.where(dij > 0, fwd, jnp.where(dij < 0, bwd, 2.0))

    for ci in range(nch):
        r0 = ci * c
        kr = u_ref[0, r0:r0 + c, gw:2 * gw]
        v = u_ref[0, r0:r0 + c, 2 * gw:3 * gw]
        ds_s[0, ci] = jnp.where(bd_mask, _dot_tn(kr * kdec_f, v), 0.0)
        ds_s[1, ci] = jnp.where(bd_mask, _dot_tn(kr * kdec_b, v), 0.0)

    s = jnp.zeros((gw, gw), F32)
    for ci in range(nch):
        st_s[0, ci] = s
        s = s * cdec_f + ds_s[0, ci]
    s = jnp.zeros((gw, gw), F32)
    for ci in list(range(n_ctx - 1, -1, -1)) + list(range(nch - 1, n_ctx - 1, -1)):
        st_s[1, ci] = s
        s = s * cdec_b + ds_s[1, ci]

    hss = [slice(h * HEAD_DIM, (h + 1) * HEAD_DIM) for h in range(N_HEADS)]
    for ci0 in range(0, nch, RET_CHUNKS_PER_STEP):
        cis = list(range(ci0, min(ci0 + RET_CHUNKS_PER_STEP, nch)))
        qrs = [u_ref[0, ci * c:(ci + 1) * c, 0:gw] for ci in cis]
        krs = [u_ref[0, ci * c:(ci + 1) * c, gw:2 * gw] for ci in cis]
        vs = [u_ref[0, ci * c:(ci + 1) * c, 2 * gw:3 * gw].astype(BF16) for ci in cis]
        atts = [[_dot_nt(qr[:, hs], kr[:, hs]) for hs in hss] for qr, kr in zip(qrs, krs)]
        inters = [_dot(qr * qdec_f, st_s[0, ci]) + _dot(qr * qdec_b, st_s[1, ci]) for ci, qr in zip(cis, qrs)]
        atts = [[(a * m_s[h]).astype(BF16) for h, a in enumerate(att)] for att in atts]
        os_ = [inter + jnp.concatenate(
            [jnp.dot(a, v[:, hs], preferred_element_type=F32) for a, hs in zip(att, hss)], axis=1)
            for inter, att, v in zip(inters, atts, vs)]
        mus = [_split_dot(o, ones_bd, 2) * (1.0 / HEAD_DIM) for o in os_]
        dlts = [o - mu for o, mu in zip(os_, mus)]
        vrs = [_split_dot(dlt * dlt, ones_bd, 2) * (1.0 / HEAD_DIM) for dlt in dlts]
        for ci, dlt, var in zip(cis, dlts, vrs):
            gate = u_ref[0, ci * c:(ci + 1) * c, 3 * gw:4 * gw]
            o_ref[0, ci * c:(ci + 1) * c, :] = (dlt * lax.rsqrt(var + NORM_EPS) * gate).astype(BF16)


def _ret(u, lg, l, *, lc):
    nb, total, _ = u.shape
    nch = total // RET_CHUNK
    return pl.pallas_call(
        functools.partial(_ret_body, lc=lc),
        grid=(nb,),
        in_specs=[pl.BlockSpec((1, total, W_RET), lambda b: (b, 0, 0)), _layer(lg, l)],
        out_specs=pl.BlockSpec((1, total, GROUP_W), lambda b: (b, 0, 0)),
        out_shape=jax.ShapeDtypeStruct((nb, total, GROUP_W), BF16),
        scratch_shapes=[pltpu.VMEM((2, nch, GROUP_W, GROUP_W), F32),
                        pltpu.VMEM((2, nch, GROUP_W, GROUP_W), F32),
                        pltpu.VMEM((N_HEADS, RET_CHUNK, RET_CHUNK), F32)],
        compiler_params=_cparams(("arbitrary",)),
        name="retention",
    )(u, lg)


def _swa_body(sink_ref, u_ref, o_ref, k_s, v_s, *, lc, sink0):
    total = u_ref.shape[1]
    t_lat = total - lc
    gw = GROUP_W
    blk = WINDOW
    nblk = t_lat // blk
    kv0 = lc + blk
    zpad = jnp.zeros((blk, SWA_KV_W), BF16)
    for s_ref in (k_s, v_s):
        s_ref[lc:lc + blk, :] = zpad
        s_ref[kv0 + t_lat:kv0 + t_lat + blk, :] = zpad
    for r0, _, _ in _row_blocks(lc, total):
        dst = r0 if r0 < lc else r0 + blk
        k_s[dst:dst + ROW_BLOCK, :] = u_ref[0, r0:r0 + ROW_BLOCK, gw:gw + SWA_KV_W].astype(BF16)
        v_s[dst:dst + ROW_BLOCK, :] = (
            u_ref[0, r0:r0 + ROW_BLOCK, gw + SWA_KV_W:gw + 2 * SWA_KV_W].astype(BF16))

    grp = N_HEADS // SWA_KV_HEADS

    def attend(items):
        scores = [[_dot_nt(q2, kk) for kk in keys] for q2, keys, _, _, _ in items]
        exps, dens = [], []
        for (_, _, _, masks, sink_col), sc in zip(items, scores):
            sc = [s if mk is None else jnp.where(mk, s, NEG_INF) for s, mk in zip(sc, masks)]
            mx = sink_col
            for s in sc:
                mx = jnp.maximum(mx, jnp.max(s, axis=-1, keepdims=True))
            es = [jnp.exp(s - mx) for s in sc]
            den = jnp.exp(sink_col - mx)
            for e in es:
                den = den + jnp.sum(e, axis=-1, keepdims=True)
            exps.append([e.astype(BF16) for e in es])
            dens.append(den)
        outs = []
        for (_, _, vals, _, _), es, den in zip(items, exps, dens):
            acc = None
            for e, vv in zip(es, vals):
                t = jnp.dot(e, vv.astype(BF16), preferred_element_type=F32)
                acc = t if acc is None else acc + t
            outs.append(acc / den)
        return outs

    def sink_column(hk, rows_per_head):
        row = lax.broadcasted_iota(jnp.int32, (grp * rows_per_head, 1), 0)
        col = jnp.full((grp * rows_per_head, 1), sink_ref[sink0 + hk * grp], F32)
        for g in range(1, grp):
            col = jnp.where(row >= g * rows_per_head, sink_ref[sink0 + hk * grp + g], col)
        return col

    kvs = [slice(hk * HEAD_DIM, (hk + 1) * HEAD_DIM) for hk in range(SWA_KV_HEADS)]

    def stack_q(hk, rows):
        return jnp.concatenate(
            [u_ref[0, rows, (hk * grp + g) * HEAD_DIM:(hk * grp + g + 1) * HEAD_DIM] for g in range(grp)],
            axis=0)

    def unstack_o(o2s, n):
        return jnp.concatenate([o2s[hk][g * n:(g + 1) * n, :]
                                for hk in range(SWA_KV_HEADS) for g in range(grp)], axis=1)

    o2s = attend([(stack_q(hk, slice(0, lc)), [k_s[0:lc, kvs[hk]]], [v_s[0:lc, kvs[hk]]], [None],
                   sink_column(hk, lc)) for hk in range(SWA_KV_HEADS)])
    gate = u_ref[0, 0:lc, gw + 2 * SWA_KV_W:2 * gw + 2 * SWA_KV_W]
    o_ref[0, 0:lc, :] = (unstack_o(o2s, lc) * gate).astype(BF16)

    qi = lax.broadcasted_iota(jnp.int32, (grp * blk, 3 * blk), 0) % blk
    kj = lax.broadcasted_iota(jnp.int32, (grp * blk, 3 * blk), 1)
    in_win = jnp.abs(kj - blk - qi) <= WINDOW

    assert nblk % SWA_BLOCKS_PER_STEP == 0

    def block_body(step, carry):
        items, qrows = [], []
        for s in range(SWA_BLOCKS_PER_STEP):
            n = step * SWA_BLOCKS_PER_STEP + s
            kpos = n * blk - blk + kj
            mask = in_win & (kpos >= 0) & (kpos < t_lat)
            qrow = pl.multiple_of(lc + n * blk, blk)
            wrow = qrow
            qrows.append(qrow)
            items += [(stack_q(hk, pl.ds(qrow, blk)),
                       [k_s[pl.ds(wrow, 3 * blk), kvs[hk]], k_s[0:lc, kvs[hk]]],
                       [v_s[pl.ds(wrow, 3 * blk), kvs[hk]], v_s[0:lc, kvs[hk]]],
                       [mask, None], sink_column(hk, blk)) for hk in range(SWA_KV_HEADS)]
        o2s = attend(items)
        for s, qrow in enumerate(qrows):
            gate = u_ref[0, pl.ds(qrow, blk), gw + 2 * SWA_KV_W:2 * gw + 2 * SWA_KV_W]
            o = unstack_o(o2s[s * SWA_KV_HEADS:(s + 1) * SWA_KV_HEADS], blk)
            o_ref[0, pl.ds(qrow, blk), :] = (o * gate).astype(BF16)
        return carry

    lax.fori_loop(0, nblk // SWA_BLOCKS_PER_STEP, block_body, 0)


def _swa(u, sink, l, *, lc):
    nb, total, _ = u.shape
    grid_spec = pltpu.PrefetchScalarGridSpec(
        num_scalar_prefetch=1,
        grid=(nb,),
        in_specs=[pl.BlockSpec((1, total, W_SWA), lambda b, s: (b, 0, 0))],
        out_specs=pl.BlockSpec((1, total, GROUP_W), lambda b, s: (b, 0, 0)),
        scratch_shapes=[pltpu.VMEM((total + 2 * WINDOW, SWA_KV_W), BF16),
                        pltpu.VMEM((total + 2 * WINDOW, SWA_KV_W), BF16)],
    )
    return pl.pallas_call(
        functools.partial(_swa_body, lc=lc, sink0=l * N_HEADS),
        grid_spec=grid_spec,
        out_shape=jax.ShapeDtypeStruct((nb, total, GROUP_W), BF16),
        compiler_params=_cparams(("arbitrary",)),
        name="swa",
    )(sink, u)


def _gdn_body(u_ref, ab_ref, avec_ref, dtb_ref, ng_ref, o_ref,
              q_s, k_s, v_s, cs_s, gt_s, of_s, ob_s, st_s, *ring_refs, lc):
    ring = (ring_refs[0:4], ring_refs[4:8])
    total = u_ref.shape[1]
    gw = GROUP_W
    c = GDN_CHUNK
    nch = total // c
    n_ctx = lc // c
    per_blk = ROW_BLOCK // c
    ones_bd = _head_ones()

    ri = lax.broadcasted_iota(jnp.int32, (ROW_BLOCK, ROW_BLOCK), 0)
    ci_ = lax.broadcasted_iota(jnp.int32, (ROW_BLOCK, ROW_BLOCK), 1)
    same_chunk = (ri // c) == (ci_ // c)
    tri_lo = (same_chunk & (ri >= ci_)).astype(BF16)
    tri_up = (same_chunk & (ri <= ci_)).astype(BF16)
    lane = lax.broadcasted_iota(jnp.int32, (ROW_BLOCK, W_AB), 1)

    def phase_a(bi, r0):
        h = u_ref[0, r0:r0 + ROW_BLOCK, 0:3 * gw]
        hq, hk, hv = h[:, 0:gw], h[:, gw:2 * gw], h[:, 2 * gw:3 * gw]
        ssq = _split_dot(hq * hq, ones_bd, 2)
        q_s[r0:r0 + ROW_BLOCK, :] = hq * lax.rsqrt(ssq + NORM_EPS) * (HEAD_DIM ** -0.5)
        ssk = _split_dot(hk * hk, ones_bd, 2)
        k_s[r0:r0 + ROW_BLOCK, :] = hk * lax.rsqrt(ssk + NORM_EPS)
        v_s[r0:r0 + ROW_BLOCK, :] = hv
        ab = ab_ref[0, r0:r0 + ROW_BLOCK, :]
        g = -avec_ref[...] * _softplus(ab + dtb_ref[...])
        g = jnp.where(lane < 2 * N_HEADS, g, 0.0)
        beta = _sigmoid(ab)
        cs_lo = _split_dot_left(tri_lo, g, 3)
        cs_up = _split_dot_left(tri_up, g, 3)
        cs = jnp.where(lane < N_HEADS, cs_lo, jnp.where(lane < 2 * N_HEADS, cs_up, beta))
        cs_s[r0:r0 + ROW_BLOCK, :] = cs
        cst = cs.T
        for cc in range(per_blk):
            row0 = (bi * per_blk + cc) * SUBLANES
            piece = cst[0:SUBLANES, cc * c:(cc + 1) * c]
            rows = [jnp.concatenate([piece[hd:hd + 1, :], piece[hd + 1:hd + 2, :]], axis=1)
                    for hd in range(0, 2 * N_HEADS, 2)]
            rows.append(jnp.zeros((SUBLANES - len(rows), PAIR_W), F32))
            gt_s[row0:row0 + SUBLANES, :] = jnp.concatenate(rows, axis=0)

    def phase_a_gen(block_ids):
        for bi in block_ids:
            phase_a(bi, bi * ROW_BLOCK)
            yield

    ii = lax.broadcasted_iota(jnp.int32, (c, PAIR_W), 0)
    jj = lax.broadcasted_iota(jnp.int32, (c, PAIR_W), 1) % c
    eye = (ii == jj).astype(F32)
    bd_r = lax.broadcasted_iota(jnp.int32, (PAIR_W, PAIR_W), 0) // HEAD_DIM
    bd_c = lax.broadcasted_iota(jnp.int32, (PAIR_W, PAIR_W), 1) // HEAD_DIM
    bd_pair = bd_r == bd_c

    def bdiag(x):
        return jnp.where(bd_pair, jnp.concatenate([x, x], axis=0), jnp.zeros((), x.dtype))

    n_sq = c.bit_length() - 2
    incl = ((ii >= jj), (ii <= jj))
    strict = ((ii > jj), (ii < jj))
    last = (c - 1, 0)
    heads = [slice(h * HEAD_DIM, (h + 1) * HEAD_DIM) for h in range(N_HEADS)]
    def expander(n_terms, lane0):
        src = lax.broadcasted_iota(jnp.int32, (n_terms * W_AB, gw), 0) % W_AB
        grp = lax.broadcasted_iota(jnp.int32, (n_terms * W_AB, gw), 1) // HEAD_DIM
        return (src == lane0 + grp).astype(BF16)

    expand_g = [expander(3, d * N_HEADS) for d in range(2)]
    expand_b = [expander(2, (2 + d) * N_HEADS) for d in range(2)]

    blk_rows = GDN_B1_CHUNKS * c
    n_steps = total // blk_rows
    assert lc % blk_rows == 0 and n_steps >= 2

    def block_row0(j, d):
        if d == 0:
            return j * blk_rows
        if isinstance(j, int):
            return lc - (j + 1) * blk_rows if (j + 1) * blk_rows <= lc else total + lc - (j + 1) * blk_rows
        return pl.multiple_of(jnp.where((j + 1) * blk_rows <= lc, lc - (j + 1) * blk_rows,
                                        total + lc - (j + 1) * blk_rows), blk_rows)

    def cat_heads(items):
        return jnp.concatenate(items, axis=1)

    def b1_gen(j, slot):
        uvr_s, wqr_s, attr_s, kttr_s = ring[slot]
        probs = []
        for d in range(2):
            r0 = block_row0(j, d)
            parts = []
            rem = cs_s[pl.ds(r0, blk_rows), :]
            for _ in range(3):
                hi = rem.astype(BF16)
                parts.append(hi)
                rem = rem - hi.astype(F32)
            xp_g = jnp.dot(jnp.concatenate(parts, axis=1), expand_g[d], preferred_element_type=F32)
            xp_b = jnp.dot(jnp.concatenate(parts[0:2], axis=1), expand_b[d], preferred_element_type=F32)
            for cc in range(GDN_B1_CHUNKS):
                r = r0 + cc * c
                kc = k_s[pl.ds(r, c), :]
                qc = q_s[pl.ds(r, c), :]
                vc = v_s[pl.ds(r, c), :]
                gt = gt_s[pl.ds(pl.multiple_of((r0 // c + cc) * SUBLANES, SUBLANES), SUBLANES), :]
                sel = []
                for p2 in range(N_HEADS // 2):
                    ps = slice(p2 * PAIR_W, (p2 + 1) * PAIR_W)
                    kp, qp, vp = kc[:, ps], qc[:, ps], vc[:, ps]
                    kbd = bdiag(kp.astype(BF16))
                    kq = lax.dot_general(jnp.concatenate([kp, qp], axis=0).astype(BF16), kbd,
                                         (((1,), (1,)), ((), ())), preferred_element_type=F32)
                    kk, qk = kq[0:c, :], kq[c:2 * c, :]
                    col = xp_g[cc * c:(cc + 1) * c, ps]
                    bcol = xp_b[cc * c:(cc + 1) * c, ps]
                    rowv = gt[d * (N_HEADS // 2) + p2:d * (N_HEADS // 2) + p2 + 1, :]
                    decay = jnp.where(incl[d], jnp.exp(jnp.where(incl[d], col - rowv, 0.0)), 0.0)
                    n = -jnp.where(strict[d], kk * decay, 0.0) * bcol
                    eg = jnp.exp(col)
                    glast = col[last[d]:last[d] + 1, :]
                    rhs = jnp.concatenate([bdiag((vp * bcol).astype(BF16)),
                                           bdiag((kp * (bcol * eg)).astype(BF16))], axis=1)
                    pr = dict(cc=cc, d=d, a=(eye - n).astype(BF16), p=eye + n, rhs=rhs,
                              att=qk * decay, qe=qp * eg, kt=kp * jnp.exp(glast - col))
                    probs.append(pr)
                    sel.append(pr)
                wqr_s[d,(2 * cc + 1) * c:(2 * cc + 2) * c, :] = (
                    cat_heads([pr['qe'] for pr in sel]).astype(BF16))
                attr_s[d,cc * c:(cc + 1) * c, :] = cat_heads([pr['att'] for pr in sel]).astype(BF16)
                ktt = cat_heads([pr['kt'] for pr in sel]).T
                kttr_s[d,cc * gw:(cc + 1) * gw, :] = ktt.astype(BF16)
        yield
        for _ in range(n_sq):
            res = [eye - jnp.dot(pr['a'], bdiag(pr['p'].astype(BF16)), preferred_element_type=F32)
                   for pr in probs]
            yield
            for pr, rr in zip(probs, res):
                pr['p'] = pr['p'] + jnp.dot(pr['p'].astype(BF16), bdiag(rr.astype(BF16)),
                                            preferred_element_type=F32)
            yield
        for pr in probs:
            pr['uw'] = jnp.dot(pr['p'].astype(BF16), pr['rhs'], preferred_element_type=F32)
        yield
        for d in range(2):
            for cc in range(GDN_B1_CHUNKS):
                sel = [pr for pr in probs if pr['cc'] == cc and pr['d'] == d]
                uvr_s[d,cc * c:(cc + 1) * c, :] = cat_heads([pr['uw'][:, 0:PAIR_W] for pr in sel])
                wqr_s[d,2 * cc * c:(2 * cc + 1) * c, :] = (
                    cat_heads([pr['uw'][:, PAIR_W:2 * PAIR_W] for pr in sel]).astype(BF16))

    def b2_gen(j, slot):
        uvr_s, wqr_s, attr_s, kttr_s = ring[slot]
        r0s = [block_row0(j, d) for d in range(2)]
        n_pairs = N_HEADS // 2
        states = [st_s[i] for i in range(2 * n_pairs)]
        first_head = lax.broadcasted_iota(jnp.int32, (1, PAIR_W), 1) < HEAD_DIM
        for step in range(GDN_B1_CHUNKS):
            probs = []
            for d in range(2):
                cc = step if d == 0 else GDN_B1_CHUNKS - 1 - step
                r = r0s[d] + cc * c
                wqm = wqr_s[d, 2 * cc * c:(2 * cc + 2) * c, :]
                uvm = uvr_s[d, cc * c:(cc + 1) * c, :]
                attm = attr_s[d, cc * c:(cc + 1) * c, :]
                grow = cs_s[pl.ds(r + last[d], 1), :]
                for p2 in range(n_pairs):
                    ps = slice(p2 * PAIR_W, (p2 + 1) * PAIR_W)
                    hd = d * N_HEADS + 2 * p2
                    kt = kttr_s[d, cc * gw + p2 * PAIR_W:cc * gw + (p2 + 1) * PAIR_W, :]
                    gl = jnp.exp(jnp.where(first_head, grow[:, hd:hd + 1], grow[:, hd + 1:hd + 2]))
                    probs.append(dict(d=d, si=d * n_pairs + p2, r=r, wqm=wqm[:, ps], uv=uvm[:, ps],
                                      att=attm[:, ps], kt=kt, gl=gl))
            for pr in probs:
                pr['wq'] = jnp.dot(pr['wqm'], bdiag(states[pr['si']].astype(BF16)),
                                   preferred_element_type=F32)
            yield
            for pr in probs:
                vn = (pr['uv'] - pr['wq'][0:c, :]).astype(BF16)
                pr['o'] = pr['wq'][c:2 * c, :] + jnp.dot(pr['att'], bdiag(vn), preferred_element_type=F32)
                upd = jnp.dot(pr['kt'], vn, preferred_element_type=F32)
                upd = jnp.where(first_head, upd[0:HEAD_DIM, :], upd[HEAD_DIM:PAIR_W, :])
                states[pr['si']] = states[pr['si']] * pr['gl'] + upd
            for d, o_s in ((0, of_s), (1, ob_s)):
                sel = [pr for pr in probs if pr['d'] == d]
                o_s[pl.ds(sel[0]['r'], c), :] = cat_heads([pr['o'] for pr in sel])
            yield
        for i in range(2 * n_pairs):
            st_s[i] = states[i]

    def run_interleaved(gens):
        live = list(gens)
        while live:
            still = []
            for g in live:
                try:
                    next(g)
                    still.append(g)
                except StopIteration:
                    pass
            live = still

    assert blk_rows == ROW_BLOCK
    n_blocks = total // ROW_BLOCK
    n_pre = lc // ROW_BLOCK
    for bi in range(n_pre):
        phase_a(bi, bi * ROW_BLOCK)
    st_s[...] = jnp.zeros(st_s.shape, F32)
    run_interleaved([b1_gen(0, 0), phase_a_gen(range(n_pre, n_blocks))])

    def pipe_body(t, carry):
        j = 1 + 2 * t
        run_interleaved([b1_gen(j, 1), b2_gen(j - 1, 0)])
        run_interleaved([b1_gen(j + 1, 0), b2_gen(j, 1)])
        return carry

    lax.fori_loop(0, (n_steps - 1) // 2, pipe_body, 0)
    if (n_steps - 1) % 2:
        run_interleaved([b1_gen(n_steps - 1, (n_steps - 1) % 2), b2_gen(n_steps - 2, n_steps % 2)])
    def finish(r0):
        o = of_s[r0:r0 + ROW_BLOCK, :] + ob_s[r0:r0 + ROW_BLOCK, :]
        ms = _split_dot(o * o, ones_bd, 2) * (1.0 / HEAD_DIM)
        gate = u_ref[0, r0:r0 + ROW_BLOCK, 3 * gw:4 * gw]
        y = o * lax.rsqrt(ms + NORM_EPS) * ng_ref[...] * gate
        o_ref[0, r0:r0 + ROW_BLOCK, :] = y.astype(BF16)

    def finish_gen(rows):
        for r0 in rows:
            finish(r0)
            yield

    last_rows = [block_row0(n_steps - 1, d) for d in range(2)]
    early = [bi * ROW_BLOCK for bi in range(n_blocks) if bi * ROW_BLOCK not in last_rows]
    run_interleaved([b2_gen(n_steps - 1, (n_steps - 1) % 2), finish_gen(early)])
    for r0 in sorted(set(last_rows)):
        finish(r0)


def _gdn_params(a_log, dt_bias, norm_g):
    depth = a_log.shape[0]
    pad = ((0, 0), (0, 0), (0, W_AB - 2 * N_HEADS))
    avec = jnp.pad(jnp.exp(a_log.reshape(depth, 1, 2 * N_HEADS)), pad)
    dtb = jnp.pad(dt_bias.reshape(depth, 1, 2 * N_HEADS), pad)
    ng = jnp.tile(norm_g.reshape(depth, 1, HEAD_DIM), (1, 1, N_HEADS))
    return avec, dtb, ng


def _gdn(u, ab, avec, dtb, ng, l, *, lc):
    nb, total, _ = u.shape
    nch = total // GDN_CHUNK
    blk = GDN_B1_CHUNKS * GDN_CHUNK
    return pl.pallas_call(
        functools.partial(_gdn_body, lc=lc),
        grid=(nb,),
        in_specs=[pl.BlockSpec((1, total, W_GDN), lambda b: (b, 0, 0)),
                  pl.BlockSpec((1, total, W_AB), lambda b: (b, 0, 0)),
                  _layer(avec, l), _layer(dtb, l), _layer(ng, l)],
        out_specs=pl.BlockSpec((1, total, GROUP_W), lambda b: (b, 0, 0)),
        out_shape=jax.ShapeDtypeStruct((nb, total, GROUP_W), BF16),
        scratch_shapes=[pltpu.VMEM((total, GROUP_W), F32),
                        pltpu.VMEM((total, GROUP_W), F32),
                        pltpu.VMEM((total, GROUP_W), F32),
                        pltpu.VMEM((total, W_AB), F32),
                        pltpu.VMEM((nch * SUBLANES, PAIR_W), F32),
                        pltpu.VMEM((total, GROUP_W), F32),
                        pltpu.VMEM((total, GROUP_W), F32),
                        pltpu.VMEM((N_HEADS, HEAD_DIM, PAIR_W), F32)] + 2 * [
                        pltpu.VMEM((2, blk, GROUP_W), F32),
                        pltpu.VMEM((2, 2 * blk, GROUP_W), BF16),
                        pltpu.VMEM((2, blk, GROUP_W), BF16),
                        pltpu.VMEM((2, GDN_B1_CHUNKS * GROUP_W, GDN_CHUNK), BF16)],
        compiler_params=_cparams(("arbitrary",)),
        name="gdn",
    )(u, ab, avec, dtb, ng)


def _pack_w_in(w_in):
    gw = GROUP_W
    o_gdn = 2 * gw
    o_ab = o_gdn + 4 * gw
    o_ret = o_ab + 4 * N_HEADS
    o_swa = o_ret + 4 * gw
    end = o_swa + W_SWA
    assert end == w_in.shape[-1]
    wb = w_in.astype(BF16)
    pad = jnp.zeros(wb.shape[:-1] + (W_AB - 4 * N_HEADS,), BF16)
    return jnp.concatenate([wb[..., 0:o_ab], wb[..., o_ret:end], wb[..., o_ab:o_ret], pad], axis=-1)


def _rope_tables(ang, lc):
    cos = jnp.cos(ang)
    sin = jnp.sin(ang)
    cos_h = jnp.concatenate([cos, cos], axis=-1)
    sin_h = jnp.concatenate([-sin, sin], axis=-1)
    cos_t = jnp.tile(cos_h, (1, N_HEADS))
    sin_t = jnp.tile(sin_h, (1, N_HEADS))
    cos_t = jnp.concatenate([jnp.ones((lc, GROUP_W), F32), cos_t], axis=0)
    sin_t = jnp.concatenate([jnp.zeros((lc, GROUP_W), F32), sin_t], axis=0)
    return cos_t, sin_t


def _rope_freqs(pos, n):
    inv = ROPE_BASE ** (-jnp.arange(0, n, 2, dtype=F32) / n)
    return pos[:, None] * inv[None, :]


def kernel(x, c, ctx, c_ctx, w_mod, b_mod, pre_norm_g, post_norm_g, w_in, w_out, lru_conv_w, lru_conv_b, lru_w_r, lru_b_r, lru_w_i, lru_b_i, lru_lambda, gdn_conv_w, gdn_a_log, gdn_dt_bias, gdn_norm_g, ret_decay_logit, swa_sink):
    nb, t, d = x.shape
    lc = ctx.shape[1]
    depth = w_mod.shape[0]
    assert t % ROW_BLOCK == 0 and lc % ROW_BLOCK == 0 and d == 4 * GROUP_W

    rows = t // GRID_W
    row = jnp.repeat(jnp.arange(rows, dtype=F32), GRID_W)
    col = jnp.tile(jnp.arange(GRID_W, dtype=F32), rows)
    ang2d = jnp.concatenate([_rope_freqs(row, HEAD_DIM // 2), _rope_freqs(col, HEAD_DIM // 2)], axis=-1)
    ang1d = _rope_freqs(jnp.arange(t, dtype=F32), HEAD_DIM)
    cos1, sin1 = _rope_tables(ang1d, lc)
    cos2, sin2 = _rope_tables(ang2d, lc)

    mod_rows = -(-(nb + 1) // SUBLANES) * SUBLANES
    s_rows = jnp.concatenate([c, c_ctx[None, :], jnp.zeros((mod_rows - nb - 1, d), F32)], axis=0)
    mod = _modulation(s_rows, w_mod, b_mod)

    w_in_p = _pack_w_in(w_in)
    w_out_b = w_out.astype(BF16)
    pre_g = pre_norm_g.reshape(depth, 1, d)
    post_g = post_norm_g.reshape(depth, 1, d)
    lru_cb, lru_wg, lru_bg, lru_lam = _lru_params(lru_conv_b, lru_w_r, lru_b_r, lru_w_i, lru_b_i, lru_lambda)
    gdn_avec, gdn_dtb, gdn_ng = _gdn_params(gdn_a_log, gdn_dt_bias, gdn_norm_g)
    ret_lg = jnp.repeat(ret_decay_logit, HEAD_DIM, axis=-1)
    sink = swa_sink.reshape(depth * N_HEADS)
    xs = jnp.concatenate([ctx, x], axis=1)
    for l in range(depth):
        u_lru, u_gdn, u_ret, u_swa, u_ab = _in_proj(xs, mod, pre_g, w_in_p, (cos1, sin1, cos2, sin2),
                                                    (lru_conv_w, lru_cb, gdn_conv_w), l, lc=lc)
        ya = _lru(u_lru, lru_wg, lru_bg, lru_lam, l, lc=lc)
        yb = _gdn(u_gdn, u_ab, gdn_avec, gdn_dtb, gdn_ng, l, lc=lc)
        yc = _ret(u_ret, ret_lg, l, lc=lc)
        yd = _swa(u_swa, sink, l, lc=lc)
        xs = _out_proj((ya, yb, yc, yd), xs, mod, post_g, w_out_b, l, lc=lc, latent_only=l == depth - 1)
    return xs
```
